```python
import math
import jax
import jax.numpy as jnp
from jax import lax
import numpy as np


D_MODEL = 1024
BATCH = 16
SEQ = 4096
DEPTH = 2

N_MEM = 256
BLOCK_Q = 128
EPS = 1e-6
ROPE_THETA = 10000.0

DA_HEADS = 4
DA_DIM = 64
DA_QK = DA_HEADS * 2 * DA_DIM
DA_V = DA_HEADS * 2 * DA_DIM
SB_HEADS = 4
SB_DIM = 128
SB_W = SB_HEADS * SB_DIM
GD_HEADS = 4
GD_DIM = 128
GD_W = GD_HEADS * GD_DIM
GD_CONV = 4
GD_CHUNK = 64
N_BRANCH = 3
IN_SPLITS = (DA_QK, DA_QK, DA_V, SB_W, SB_W, SB_W, GD_W, GD_W, GD_W, GD_W, GD_HEADS, GD_HEADS, N_BRANCH * D_MODEL)
D_IN = 2 * DA_QK + DA_V + 3 * SB_W + 4 * GD_W + 2 * GD_HEADS + N_BRANCH * D_MODEL
X_HEADS = 4
X_DIM = 128
X_W = X_HEADS * X_DIM
D_FF_DENSE = 2816
N_EXPERTS = 8
TOP_K = 2
D_FF_EXPERT = 3584
N_DENSE_LAYERS = (DEPTH + 1) // 2
N_MOE_LAYERS = DEPTH // 2

kernel_name = 'hybrid_diff_stickbreak_deltanet_moe'


def rmsnorm(x, g):
    xf = x.astype(jnp.float32)
    y = xf * lax.rsqrt(jnp.mean(xf * xf, axis=-1, keepdims=True) + EPS)
    return (y * g.astype(jnp.float32)).astype(x.dtype)


def l2norm(x):
    return x * lax.rsqrt(jnp.sum(x * x, axis=-1, keepdims=True) + EPS)


def rope_tables(seq, dim):
    inv_freq = 1.0 / (ROPE_THETA ** (jnp.arange(0, dim, 2, dtype=jnp.float32) / dim))
    ang = jnp.arange(seq, dtype=jnp.float32)[:, None] * inv_freq[None, :]
    return jnp.cos(ang), jnp.sin(ang)


def apply_rope(t, cos, sin):
    t1, t2 = jnp.split(t, 2, axis=-1)
    c, s = cos.astype(t.dtype), sin.astype(t.dtype)
    return jnp.concatenate([t1 * c - t2 * s, t2 * c + t1 * s], axis=-1)


def to_heads(t, n_heads):
    b, s, w = t.shape
    return t.reshape(b, s, n_heads, w // n_heads).transpose(0, 2, 1, 3)


def from_heads(t):
    b, h, s, d = t.shape
    return t.transpose(0, 2, 1, 3).reshape(b, s, h * d)


def causal_depthwise_conv(t, w):
    ch = t.shape[-1]
    return lax.conv_general_dilated(t, w[:, None, :].astype(t.dtype), window_strides=(1,),
                                    padding=[(GD_CONV - 1, 0)],
                                    dimension_numbers=('NWC', 'WIO', 'NWC'),
                                    feature_group_count=ch)


def swiglu(h, w_gate, w_up, w_down):
    return (jax.nn.silu(h @ w_gate) * (h @ w_up)) @ w_down


def diff_attention(q, k, v, lam, subln, lam_init, cos, sin):
    b, s, _ = q.shape
    q = q.reshape(b, s, DA_HEADS, 2, DA_DIM).transpose(0, 2, 3, 1, 4)
    k = k.reshape(b, s, DA_HEADS, 2, DA_DIM).transpose(0, 2, 3, 1, 4)
    q = apply_rope(q, cos, sin) * DA_DIM ** -0.5
    k = apply_rope(k, cos, sin)
    v = to_heads(v, DA_HEADS)
    outs = []
    for i in range(s // BLOCK_Q):
        lo, hi = i * BLOCK_Q, (i + 1) * BLOCK_Q
        sc = jnp.einsum('bhcqd,bhckd->bhcqk', q[:, :, :, lo:hi], k[:, :, :, :hi]).astype(jnp.float32)
        causal = jnp.arange(hi)[None, :] <= jnp.arange(lo, hi)[:, None]
        p = jax.nn.softmax(jnp.where(causal, sc, -jnp.inf), axis=-1)
        attn = (p[:, :, 0] - lam * p[:, :, 1]).astype(v.dtype)
        outs.append(jnp.einsum('bhqk,bhkd->bhqd', attn, v[:, :, :hi]))
    o = rmsnorm(jnp.concatenate(outs, axis=2), subln) * (1.0 - lam_init)
    return from_heads(o)


def stick_breaking_attention(q, k, v):
    q = to_heads(q, SB_HEADS) * SB_DIM ** -0.5
    k = to_heads(k, SB_HEADS)
    v = to_heads(v, SB_HEADS)
    s = q.shape[2]
    outs = []
    for i in range(s // BLOCK_Q):
        lo, hi = i * BLOCK_Q, (i + 1) * BLOCK_Q
        z = jnp.einsum('bhqd,bhkd->bhqk', q[:, :, lo:hi], k[:, :, :hi]).astype(jnp.float32)
        strict = jnp.arange(hi)[None, :] < jnp.arange(lo, hi)[:, None]
        log_fail = jnp.where(strict, jax.nn.log_sigmoid(-z), 0.0)
        log_remain = lax.cumsum(log_fail, axis=3, reverse=True) - log_fail
        attn = jnp.where(strict, jnp.exp(jax.nn.log_sigmoid(z) + log_remain), 0.0)
        outs.append(jnp.einsum('bhqk,bhkd->bhqd', attn.astype(v.dtype), v[:, :, :hi]))
    return from_heads(jnp.concatenate(outs, axis=2))


def gated_deltanet(q, k, v, z, a, bg, conv_w, a_log, dt_bias, norm_g):
    dtype = q.dtype
    f32 = jnp.float32
    qkv = jax.nn.silu(causal_depthwise_conv(jnp.concatenate([q, k, v], axis=-1), conv_w))
    q, k, v = jnp.split(qkv, 3, axis=-1)
    q = l2norm(to_heads(q, GD_HEADS).astype(f32)) * GD_DIM ** -0.5
    k = l2norm(to_heads(k, GD_HEADS).astype(f32))
    v = to_heads(v, GD_HEADS).astype(f32)
    g = -jnp.exp(a_log.astype(f32))[:, None] * jax.nn.softplus(a.astype(f32).transpose(0, 2, 1) + dt_bias.astype(f32)[:, None])
    beta = jax.nn.sigmoid(bg.astype(f32)).transpose(0, 2, 1)
    b, h, s, d = q.shape
    nc = s // GD_CHUNK

    def chunk(t):
        return t.reshape(b, h, nc, GD_CHUNK, *t.shape[3:])

    qc, kc, vc, bc = chunk(q), chunk(k), chunk(v), chunk(beta)
    gc = lax.cumsum(chunk(g), axis=3)
    tril = jnp.tril(jnp.ones((GD_CHUNK, GD_CHUNK), dtype=bool))
    strict = jnp.tril(jnp.ones((GD_CHUNK, GD_CHUNK), dtype=bool), k=-1)
    diff = gc[..., :, None] - gc[..., None, :]
    decay = jnp.where(tril, jnp.exp(jnp.where(tril, diff, 0.0)), 0.0)
    kb = kc * bc[..., None]
    vb = vc * bc[..., None]
    lmat = jnp.where(strict, jnp.einsum('bhnid,bhnjd->bhnij', kb, kc) * decay, 0.0)
    eye = jnp.eye(GD_CHUNK, dtype=f32)
    tmat = lax.linalg.triangular_solve(eye + lmat, jnp.broadcast_to(eye, lmat.shape),
                                       left_side=True, lower=True, unit_diagonal=True)
    u = tmat @ vb
    w = tmat @ (kb * jnp.exp(gc)[..., None])
    intra = jnp.where(tril, jnp.einsum('bhnid,bhnjd->bhnij', qc, kc) * decay, 0.0)

    def step(state, xs):
        qi, ki, ui, wi, ai, gi = xs
        v_new = ui - wi @ state
        out = (qi * jnp.exp(gi)[..., None]) @ state + ai @ v_new
        g_last = gi[..., -1]
        state = state * jnp.exp(g_last)[..., None, None] + jnp.einsum(
            'bhcd,bhce->bhde', ki * jnp.exp(g_last[..., None] - gi)[..., None], v_new)
        return state, out

    xs = tuple(jnp.moveaxis(t, 2, 0) for t in (qc, kc, u, w, intra, gc))
    _, o = lax.scan(step, jnp.zeros((b, h, d, d), f32), xs)
    o = jnp.moveaxis(o, 0, 2).reshape(b, h, s, d)
    o = rmsnorm(o, norm_g) * jax.nn.silu(to_heads(z, GD_HEADS).astype(f32))
    return from_heads(o).astype(dtype)


def hybrid_mixer(h, w_in, gd_conv, lam_q1, lam_k1, lam_q2, lam_k2, da_subln, gd_a_log, gd_dt_bias,
                 gd_norm, w_br_a, w_br_b, w_br_c, w_out, lam_init, cos, sin):
    cuts = [int(c) for c in np.cumsum(IN_SPLITS)[:-1]]
    (da_q, da_k, da_v, sb_q, sb_k, sb_v, gd_q, gd_k, gd_v, gd_z, gd_a, gd_b, gates) = jnp.split(h @ w_in, cuts, axis=-1)
    f32 = jnp.float32
    lam = (jnp.exp(jnp.sum(lam_q1.astype(f32) * lam_k1.astype(f32)))
           - jnp.exp(jnp.sum(lam_q2.astype(f32) * lam_k2.astype(f32))) + lam_init)
    o_a = diff_attention(da_q, da_k, da_v, lam, da_subln, lam_init, cos, sin)
    o_b = stick_breaking_attention(sb_q, sb_k, sb_v)
    o_c = gated_deltanet(gd_q, gd_k, gd_v, gd_z, gd_a, gd_b, gd_conv, gd_a_log, gd_dt_bias, gd_norm)
    g_a, g_b, g_c = jnp.split(jax.nn.sigmoid(gates), 3, axis=-1)
    merged = g_a * (o_a @ w_br_a) + g_b * (o_b @ w_br_b) + g_c * (o_c @ w_br_c)
    return merged @ w_out


def memory_cross_attention(h, mem_n, w_q, w_kv, w_o):
    q = to_heads(h @ w_q, X_HEADS) * X_DIM ** -0.5
    k, v = jnp.split(mem_n @ w_kv, 2, axis=-1)
    k, v = to_heads(k, X_HEADS), to_heads(v, X_HEADS)
    b, nh, s, d = q.shape
    qb = q.reshape(b, nh, s // BLOCK_Q, BLOCK_Q, d).transpose(2, 0, 1, 3, 4)

    def block(qi):
        p = jax.nn.softmax(jnp.einsum('bhqd,bhmd->bhqm', qi, k).astype(jnp.float32), axis=-1)
        return jnp.einsum('bhqm,bhmd->bhqd', p.astype(v.dtype), v)

    o = lax.map(block, qb).transpose(1, 2, 0, 3, 4).reshape(b, nh, s, d)
    return from_heads(o) @ w_o


def moe_swiglu(h, w_router, w_gate, w_up, w_down):
    b, s, d = h.shape
    t = h.reshape(b * s, d)
    logits = (t @ w_router).astype(jnp.float32)
    top_val, top_idx = lax.top_k(logits, TOP_K)
    top_w = jax.nn.softmax(top_val, axis=-1)
    combine = jnp.sum(jax.nn.one_hot(top_idx, N_EXPERTS, dtype=jnp.float32) * top_w[..., None], axis=1)
    y = jnp.zeros_like(t)
    for e in range(N_EXPERTS):
        y = y + combine[:, e:e + 1].astype(t.dtype) * swiglu(t, w_gate[e], w_up[e], w_down[e])
    return y.reshape(b, s, d)


def setup_inputs(seed: int = 0) -> dict:
    key = jax.random.key(seed)
    counter = [0]

    def nk():
        counter[0] += 1
        return jax.random.fold_in(key, counter[0])

    def nrm(shape, scale):
        return jax.random.normal(nk(), shape, jnp.float32) * scale

    def gain(shape):
        return 1.0 + 0.01 * jax.random.normal(nk(), shape, jnp.float32)

    dt = jnp.exp(jax.random.uniform(nk(), (DEPTH, GD_HEADS), jnp.float32, math.log(0.001), math.log(0.1)))
    return {
        'x': nrm((BATCH, SEQ, D_MODEL), 1.0),
        'mem': nrm((BATCH, N_MEM, D_MODEL), 1.0),
        'norm_mix': gain((DEPTH, D_MODEL)),
        'w_in': nrm((DEPTH, D_MODEL, D_IN), D_MODEL ** -0.5),
        'gd_conv': nrm((DEPTH, GD_CONV, 3 * GD_W), GD_CONV ** -0.5),
        'lam_q1': nrm((DEPTH, DA_DIM), 0.1),
        'lam_k1': nrm((DEPTH, DA_DIM), 0.1),
        'lam_q2': nrm((DEPTH, DA_DIM), 0.1),
        'lam_k2': nrm((DEPTH, DA_DIM), 0.1),
        'da_subln': gain((DEPTH, 2 * DA_DIM)),
        'gd_a_log': jnp.log(jax.random.uniform(nk(), (DEPTH, GD_HEADS), jnp.float32, 1.0, 16.0)),
        'gd_dt_bias': dt + jnp.log(-jnp.expm1(-dt)),
        'gd_norm': gain((DEPTH, GD_DIM)),
        'w_br_a': nrm((DEPTH, DA_V, D_MODEL), DA_V ** -0.5),
        'w_br_b': nrm((DEPTH, SB_W, D_MODEL), SB_W ** -0.5),
        'w_br_c': nrm((DEPTH, GD_W, D_MODEL), GD_W ** -0.5),
        'w_out': nrm((DEPTH, D_MODEL, D_MODEL), D_MODEL ** -0.5),
        'norm_x': gain((DEPTH, D_MODEL)),
        'norm_mem': gain((DEPTH, D_MODEL)),
        'w_q_x': nrm((DEPTH, D_MODEL, X_W), D_MODEL ** -0.5),
        'w_kv_x': nrm((DEPTH, D_MODEL, 2 * X_W), D_MODEL ** -0.5),
        'w_o_x': nrm((DEPTH, X_W, D_MODEL), X_W ** -0.5),
        'norm_ffn': gain((DEPTH, D_MODEL)),
        'w_gate_dense': nrm((N_DENSE_LAYERS, D_MODEL, D_FF_DENSE), D_MODEL ** -0.5),
        'w_up_dense': nrm((N_DENSE_LAYERS, D_MODEL, D_FF_DENSE), D_MODEL ** -0.5),
        'w_down_dense': nrm((N_DENSE_LAYERS, D_FF_DENSE, D_MODEL), D_FF_DENSE ** -0.5),
        'w_router': nrm((N_MOE_LAYERS, D_MODEL, N_EXPERTS), D_MODEL ** -0.5),
        'w_gate_exp': nrm((N_MOE_LAYERS, N_EXPERTS, D_MODEL, D_FF_EXPERT), D_MODEL ** -0.5),
        'w_up_exp': nrm((N_MOE_LAYERS, N_EXPERTS, D_MODEL, D_FF_EXPERT), D_MODEL ** -0.5),
        'w_down_exp': nrm((N_MOE_LAYERS, N_EXPERTS, D_FF_EXPERT, D_MODEL), D_FF_EXPERT ** -0.5),
        'norm_final': gain((D_MODEL,)),
    }


def reference(x, mem, norm_mix, w_in, gd_conv, lam_q1, lam_k1, lam_q2, lam_k2, da_subln, gd_a_log,
              gd_dt_bias, gd_norm, w_br_a, w_br_b, w_br_c, w_out, norm_x, norm_mem, w_q_x, w_kv_x,
              w_o_x, norm_ffn, w_gate_dense, w_up_dense, w_down_dense, w_router, w_gate_exp,
              w_up_exp, w_down_exp, norm_final):
    cos, sin = rope_tables(x.shape[1], DA_DIM)
    for l in range(DEPTH):
        lam_init = 0.8 - 0.6 * math.exp(-0.3 * l)
        x = x + hybrid_mixer(rmsnorm(x, norm_mix[l]), w_in[l], gd_conv[l], lam_q1[l], lam_k1[l],
                             lam_q2[l], lam_k2[l], da_subln[l], gd_a_log[l], gd_dt_bias[l], gd_norm[l],
                             w_br_a[l], w_br_b[l], w_br_c[l], w_out[l], lam_init, cos, sin)
        x = x + memory_cross_attention(rmsnorm(x, norm_x[l]), rmsnorm(mem, norm_mem[l]),
                                       w_q_x[l], w_kv_x[l], w_o_x[l])
        h = rmsnorm(x, norm_ffn[l])
        if l % 2 == 0:
            x = x + swiglu(h, w_gate_dense[l // 2], w_up_dense[l // 2], w_down_dense[l // 2])
        else:
            x = x + moe_swiglu(h, w_router[l // 2], w_gate_exp[l // 2], w_up_exp[l // 2], w_down_exp[l // 2])
    return rmsnorm(x, norm_final)
```

```python
import functools
import math

import jax
import jax.numpy as jnp
from jax import lax
from jax.experimental import pallas as pl
from jax.experimental.pallas import tpu as pltpu

EPS = 1e-6
ROPE_THETA = 10000.0
LANES = 128

DA_HEADS = 4
DA_DIM = 64
SB_HEADS = 4
SB_DIM = 128
GD_HEADS = 4
GD_DIM = 128
GD_CONV = 4
GD_CHUNK = 64
X_HEADS = 4
X_DIM = 128
N_EXPERTS = 8
HEAD_W = 512

COL_DA_Q, COL_DA_K, COL_DA_V = 0, 4, 8
COL_SB_Q, COL_SB_K, COL_SB_V = 12, 16, 20
COL_GD_Q, COL_GD_Z = 24, 36
MIX_W = 5120

SB_LOG_FLOOR = -110.0
NEG_BIG = -1e30
VMEM_LIMIT = 48 * 1024 * 1024

_NT = (((1,), (1,)), ((), ()))
_TN = (((0,), (0,)), ((), ()))


def _cparams(*sem):
    return pltpu.CompilerParams(dimension_semantics=sem, vmem_limit_bytes=VMEM_LIMIT)


def _tile(n, pref):
    t = min(n, pref)
    while n % t:
        t //= 2
    return t


def _bdot(a, b, dims=None):
    a = a.astype(jnp.bfloat16)
    b = b.astype(jnp.bfloat16)
    if dims is None:
        return jnp.dot(a, b, preferred_element_type=jnp.float32)
    return lax.dot_general(a, b, dims, preferred_element_type=jnp.float32)


def _dot_split(a, b_exact):
    hi = a.astype(jnp.bfloat16)
    lo = (a - hi.astype(jnp.float32)).astype(jnp.bfloat16)
    return (jnp.dot(hi, b_exact, preferred_element_type=jnp.float32)
            + jnp.dot(lo, b_exact, preferred_element_type=jnp.float32))


def _rms(x, g):
    return x * lax.rsqrt(jnp.mean(x * x, axis=-1, keepdims=True) + EPS) * g


def _softplus(z):
    return jnp.maximum(z, 0.0) + jnp.log1p(jnp.exp(-jnp.abs(z)))


def _silu(x):
    return x * jax.nn.sigmoid(x)


def _lane_col(x, lane):
    ids = lax.broadcasted_iota(jnp.int32, x.shape, 1)
    return jnp.sum(jnp.where(ids == lane, x, 0.0), axis=1, keepdims=True)


def _rms_matmul_kernel(x_ref, g_ref, w_ref, o_ref, h_ref, *, exact):
    @pl.when(pl.program_id(1) == 0)
    def _():
        h_ref[...] = _rms(x_ref[...], g_ref[...]).astype(h_ref.dtype)

    if exact:
        acc = jnp.dot(h_ref[...], w_ref[...], preferred_element_type=jnp.float32,
                      precision=lax.Precision.HIGHEST)
    else:
        acc = jnp.dot(h_ref[...], w_ref[...], preferred_element_type=jnp.float32)
    o_ref[...] = acc.astype(o_ref.dtype)


def rms_matmul(x, g, w, out_dtype, *, exact=False, tm=1024, tn=1024):
    n, d = x.shape
    nout = w.shape[1]
    tm, tn = _tile(n, tm), _tile(nout, tn)
    return pl.pallas_call(
        functools.partial(_rms_matmul_kernel, exact=exact),
        grid=(n // tm, nout // tn),
        in_specs=[pl.BlockSpec((tm, d), lambda i, j: (i, 0)),
                  pl.BlockSpec((1, d), lambda i, j: (0, 0)),
                  pl.BlockSpec((d, tn), lambda i, j: (0, j))],
        out_specs=pl.BlockSpec((tm, tn), lambda i, j: (i, j)),
        out_shape=jax.ShapeDtypeStruct((n, nout), out_dtype),
        scratch_shapes=[pltpu.VMEM((tm, d), w.dtype)],
        compiler_params=_cparams("parallel", "arbitrary"),
        name="rms_matmul",
    )(x, g.reshape(1, d), w)


def _rope_kernel(q_ref, k_ref, cos_ref, sin_ref, qo_ref, ko_ref):
    cos, sin = cos_ref[...], sin_ref[...]
    lane = lax.broadcasted_iota(jnp.int32, cos.shape, 1)
    first_half = (lane % DA_DIM) < (DA_DIM // 2)

    def rot(t):
        partner = jnp.where(first_half, pltpu.roll(t, LANES - DA_DIM // 2, 1),
                            pltpu.roll(t, DA_DIM // 2, 1))
        return t * cos + partner * sin

    for h in range(DA_HEADS):
        sl = slice(h * LANES, (h + 1) * LANES)
        qo_ref[:, sl] = (rot(q_ref[:, sl].astype(jnp.float32)) * DA_DIM ** -0.5).astype(qo_ref.dtype)
        ko_ref[:, sl] = rot(k_ref[:, sl].astype(jnp.float32)).astype(ko_ref.dtype)


def rope_qk(proj, cos, sin, seq, *, ts=1024):
    n = proj.shape[0]
    ts = _tile(seq, ts)
    per_seq = seq // ts
    out = jax.ShapeDtypeStruct((n, HEAD_W), jnp.bfloat16)
    return pl.pallas_call(
        _rope_kernel,
        grid=(n // ts,),
        in_specs=[pl.BlockSpec((ts, HEAD_W), lambda i: (i, COL_DA_Q // 4)),
                  pl.BlockSpec((ts, HEAD_W), lambda i: (i, COL_DA_K // 4)),
                  pl.BlockSpec((ts, LANES), lambda i: (i % per_seq, 0)),
                  pl.BlockSpec((ts, LANES), lambda i: (i % per_seq, 0))],
        out_specs=[pl.BlockSpec((ts, HEAD_W), lambda i: (i, 0)),
                   pl.BlockSpec((ts, HEAD_W), lambda i: (i, 0))],
        out_shape=[out, out],
        compiler_params=_cparams("parallel"),
        name="rope_qk",
    )(proj, proj, cos, sin)


def _diff_attn_kernel(lam_ref, q_ref, k_ref, v_ref, subln_ref, o_ref,
                      qs_ref, m_ref, l_ref, acc_ref, *, tq, tk, lam_init):
    i = pl.program_id(2)
    q = q_ref[...]
    lane = lax.broadcasted_iota(jnp.int32, q.shape, 1)
    zero = jnp.zeros_like(q)
    qs_ref[:tq, :] = jnp.where(lane < DA_DIM, q, zero)
    qs_ref[tq:, :] = jnp.where(lane >= DA_DIM, q, zero)
    m_ref[...] = jnp.full(m_ref.shape, NEG_BIG, jnp.float32)
    l_ref[...] = jnp.zeros(l_ref.shape, jnp.float32)
    acc_ref[...] = jnp.zeros(acc_ref.shape, jnp.float32)

    def step(start, masked):
        k = k_ref[pl.ds(start, tk), :]
        v = v_ref[pl.ds(start, tk), :]
        s = lax.dot_general(qs_ref[...], k, _NT, preferred_element_type=jnp.float32)
        if masked:
            row = lax.broadcasted_iota(jnp.int32, s.shape, 0) % tq + i * tq
            col = lax.broadcasted_iota(jnp.int32, s.shape, 1) + start
            s = jnp.where(col <= row, s, NEG_BIG)
        m_old = m_ref[...]
        m_new = jnp.maximum(m_old, jnp.max(s, axis=1, keepdims=True))
        alpha = jnp.exp(m_old - m_new)
        p = jnp.exp(s - m_new)
        l_ref[...] = alpha * l_ref[...] + jnp.sum(p, axis=1, keepdims=True)
        acc_ref[...] = alpha * acc_ref[...] + jnp.dot(
            p.astype(jnp.bfloat16), v, preferred_element_type=jnp.float32)
        m_ref[...] = m_new

    def full_tile(j, carry):
        step(pl.multiple_of(j * tk, tk), False)
        return carry

    lax.fori_loop(0, i * (tq // tk), full_tile, 0)
    for d in range(tq // tk):
        step(pl.multiple_of(i * tq + d * tk, tk), True)

    lq1, lk1, lq2, lk2 = (lam_ref[r:r + 1, :] for r in range(4))
    lam = (jnp.exp(jnp.sum(lq1 * lk1, axis=1, keepdims=True))
           - jnp.exp(jnp.sum(lq2 * lk2, axis=1, keepdims=True)) + lam_init)
    acc = acc_ref[...] / l_ref[...]
    o = acc[:tq] - lam * acc[tq:]
    o_ref[...] = (_rms(o, subln_ref[...]) * (1.0 - lam_init)).astype(o_ref.dtype)


def diff_attention(q_rope, k_rope, proj, lam_params, subln, batch, seq, lam_init, *, tq=256, tk=256):
    n = q_rope.shape[0]
    tq = _tile(seq, tq)
    tk = _tile(tq, tk)
    nq = seq // tq
    return pl.pallas_call(
        functools.partial(_diff_attn_kernel, tq=tq, tk=tk, lam_init=lam_init),
        grid=(batch, DA_HEADS, nq),
        in_specs=[pl.BlockSpec((4, DA_DIM), lambda b, h, i: (0, 0)),
                  pl.BlockSpec((tq, LANES), lambda b, h, i: (b * nq + i, h)),
                  pl.BlockSpec((seq, LANES), lambda b, h, i: (b, h)),
                  pl.BlockSpec((seq, LANES), lambda b, h, i: (b, COL_DA_V + h)),
                  pl.BlockSpec((1, LANES), lambda b, h, i: (0, 0))],
        out_specs=pl.BlockSpec((tq, LANES), lambda b, h, i: (b * nq + i, h)),
        out_shape=jax.ShapeDtypeStruct((n, HEAD_W), jnp.bfloat16),
        scratch_shapes=[pltpu.VMEM((2 * tq, LANES), jnp.bfloat16),
                        pltpu.VMEM((2 * tq, 1), jnp.float32),
                        pltpu.VMEM((2 * tq, 1), jnp.float32),
                        pltpu.VMEM((2 * tq, LANES), jnp.float32)],
        compiler_params=_cparams("parallel", "parallel", "arbitrary"),
        name="diff_attention",
    )(lam_params, q_rope, k_rope, proj, subln.reshape(1, LANES))


def _sb_attn_kernel(q_ref, k_ref, v_ref, o_ref, carry_ref, acc_ref, *, tq):
    i = pl.program_id(2)
    q = (q_ref[...].astype(jnp.float32) * SB_DIM ** -0.5).astype(jnp.bfloat16)
    carry_ref[...] = jnp.zeros(carry_ref.shape, jnp.float32)
    acc_ref[...] = jnp.zeros(acc_ref.shape, jnp.float32)
    r = lax.broadcasted_iota(jnp.int32, (tq, tq), 0)
    c = lax.broadcasted_iota(jnp.int32, (tq, tq), 1)
    later = (r > c).astype(jnp.bfloat16)

    def step(j, masked):
        start = pl.multiple_of(j * tq, tq)
        k = k_ref[pl.ds(start, tq), :]
        v = v_ref[pl.ds(start, tq), :]
        z = lax.dot_general(q, k, _NT, preferred_element_type=jnp.float32)
        sp = _softplus(z)
        log_fail = jnp.where(c < r, -sp, 0.0) if masked else -sp
        log_remain = carry_ref[...] + _dot_split(log_fail, later)
        attn = jnp.exp(z - sp + log_remain)
        if masked:
            attn = jnp.where(c < r, attn, 0.0)
        acc_ref[...] += jnp.dot(attn.astype(jnp.bfloat16), v, preferred_element_type=jnp.float32)
        carry_ref[...] += jnp.sum(log_fail, axis=1, keepdims=True)

    step(i, True)

    def cond(state):
        j, alive = state
        return jnp.logical_and(j >= 0, alive)

    def body(state):
        j, _ = state
        step(j, False)
        return j - 1, jnp.max(carry_ref[...]) > SB_LOG_FLOOR

    lax.while_loop(cond, body, (i - 1, True))
    o_ref[...] = acc_ref[...].astype(o_ref.dtype)


def stick_breaking_attention(proj, batch, seq, *, tq=256):
    n = proj.shape[0]
    tq = _tile(seq, tq)
    nq = seq // tq
    return pl.pallas_call(
        functools.partial(_sb_attn_kernel, tq=tq),
        grid=(batch, SB_HEADS, nq),
        in_specs=[pl.BlockSpec((tq, LANES), lambda b, h, i: (b * nq + i, COL_SB_Q + h)),
                  pl.BlockSpec((seq, LANES), lambda b, h, i: (b, COL_SB_K + h)),
                  pl.BlockSpec((seq, LANES), lambda b, h, i: (b, COL_SB_V + h))],
        out_specs=pl.BlockSpec((tq, LANES), lambda b, h, i: (b * nq + i, h)),
        out_shape=jax.ShapeDtypeStruct((n, HEAD_W), jnp.bfloat16),
        scratch_shapes=[pltpu.VMEM((tq, 1), jnp.float32),
                        pltpu.VMEM((tq, LANES), jnp.float32)],
        compiler_params=_cparams("parallel", "parallel", "arbitrary"),
        name="stick_breaking_attention",
    )(proj, proj, proj)


def _gd_conv_kernel(x_ref, w_ref, o_ref):
    c = pl.program_id(1)
    x = x_ref[...].astype(jnp.float32)
    w = w_ref[...]
    row = lax.broadcasted_iota(jnp.int32, x.shape, 0)
    y = x * w[GD_CONV - 1:GD_CONV, :]
    for back in range(1, GD_CONV):
        shifted = jnp.where(row >= back, pltpu.roll(x, back, 0), 0.0)
        y = y + shifted * w[GD_CONV - 1 - back:GD_CONV - back, :]
    y = _silu(y)
    inv = lax.rsqrt(jnp.sum(y * y, axis=1, keepdims=True) + EPS)
    is_q = c < GD_HEADS
    is_v = c >= 2 * GD_HEADS
    scale = jnp.where(is_v, 1.0, inv * jnp.where(is_q, GD_DIM ** -0.5, 1.0))
    o_ref[...] = (y * scale).astype(o_ref.dtype)


def gd_conv_silu(proj, conv_w, batch, seq):
    n = proj.shape[0]
    ncol = 3 * GD_HEADS
    return pl.pallas_call(
        _gd_conv_kernel,
        grid=(batch, ncol),
        in_specs=[pl.BlockSpec((seq, LANES), lambda b, c: (b, COL_GD_Q + c)),
                  pl.BlockSpec((GD_CONV, LANES), lambda b, c: (0, c))],
        out_specs=pl.BlockSpec((seq, LANES), lambda b, c: (b, c)),
        out_shape=jax.ShapeDtypeStruct((n, ncol * LANES), jnp.float32),
        compiler_params=_cparams("parallel", "parallel"),
        name="gd_conv",
    )(proj, conv_w)


def _gd_gate_kernel(ab_ref, alog_ref, dtb_ref, o_ref, *, rows):
    ab = ab_ref[...]
    is_decay_lane = lax.broadcasted_iota(jnp.int32, (GD_CHUNK, LANES), 1) < GD_HEADS
    g = -jnp.exp(alog_ref[...]) * _softplus(ab + dtb_ref[...])
    beta = jax.nn.sigmoid(ab)
    r = lax.broadcasted_iota(jnp.int32, (GD_CHUNK, GD_CHUNK), 0)
    c = lax.broadcasted_iota(jnp.int32, (GD_CHUNK, GD_CHUNK), 1)
    tril = (c <= r).astype(jnp.bfloat16)
    for ci in range(rows // GD_CHUNK):
        sl = slice(ci * GD_CHUNK, (ci + 1) * GD_CHUNK)
        hi = g[sl].astype(jnp.bfloat16)
        mid_f = g[sl] - hi.astype(jnp.float32)
        mid = mid_f.astype(jnp.bfloat16)
        lo = (mid_f - mid.astype(jnp.float32)).astype(jnp.bfloat16)
        gc = (jnp.dot(tril, hi, preferred_element_type=jnp.float32)
              + jnp.dot(tril, mid, preferred_element_type=jnp.float32)
              + jnp.dot(tril, lo, preferred_element_type=jnp.float32))
        o_ref[sl, :] = jnp.where(is_decay_lane, gc, beta[sl])


def gd_gates(ab, a_log, dt_bias, *, rows=512):
    n = ab.shape[0]
    rows = _tile(n, rows)
    pad = jnp.zeros((LANES - GD_HEADS,), jnp.float32)
    alog = jnp.concatenate([a_log, pad]).reshape(1, LANES)
    dtb = jnp.concatenate([dt_bias, pad]).reshape(1, LANES)
    return pl.pallas_call(
        functools.partial(_gd_gate_kernel, rows=rows),
        grid=(n // rows,),
        in_specs=[pl.BlockSpec((rows, LANES), lambda i: (i, 0)),
                  pl.BlockSpec((1, LANES), lambda i: (0, 0)),
                  pl.BlockSpec((1, LANES), lambda i: (0, 0))],
        out_specs=pl.BlockSpec((rows, LANES), lambda i: (i, 0)),
        out_shape=jax.ShapeDtypeStruct((n, LANES), jnp.float32),
        compiler_params=_cparams("parallel"),
        name="gd_gates",
    )(ab, alog, dtb)


def _unit_lower_inverse(lmat, r, c):
    eye = (r == c).astype(jnp.float32)
    diag16 = (r // 16) == (c // 16)
    d = jnp.where(diag16, lmat, 0.0)
    t = eye - d
    p = d
    for _ in range(3):
        p = _bdot(p, p)
        t = t + _bdot(t, p)
    for blk in (32, 64):
        band = jnp.logical_and((r // blk) == (c // blk), (r // (blk // 2)) != (c // (blk // 2)))
        cmat = jnp.where(band, lmat, 0.0)
        t = t - _bdot(_bdot(t, cmat), t)
    return t


def _gd_local_kernel(q_ref, k_ref, v_ref, gate_ref, u_ref, w_ref, qg_ref, kd_ref, a_ref, *, rows):
    h = pl.program_id(1)
    gate = gate_ref[...]
    gc_all = _lane_col(gate, h)
    beta_all = _lane_col(gate, GD_HEADS + h)
    r = lax.broadcasted_iota(jnp.int32, (GD_CHUNK, GD_CHUNK), 0)
    c = lax.broadcasted_iota(jnp.int32, (GD_CHUNK, GD_CHUNK), 1)
    for ci in range(rows // GD_CHUNK):
        sl = slice(ci * GD_CHUNK, (ci + 1) * GD_CHUNK)
        q, k, v = q_ref[sl, :], k_ref[sl, :], v_ref[sl, :]
        gc, beta = gc_all[sl], beta_all[sl]
        gc_rows = jnp.broadcast_to(gc, (GD_CHUNK, LANES))
        gc_cols = jnp.transpose(gc_rows)[0:1, :GD_CHUNK]
        decay = jnp.exp(jnp.where(c <= r, gc_rows[:, :GD_CHUNK] - gc_cols, 0.0))
        kb = k * beta
        lmat = jnp.where(c < r, _bdot(kb, k, _NT) * decay, 0.0)
        tmat = _unit_lower_inverse(lmat, r, c)
        egc = jnp.exp(gc)
        uw = _bdot(tmat, jnp.concatenate([v * beta, kb * egc], axis=1))
        u_ref[sl, :] = uw[:, :LANES]
        w_ref[sl, :] = uw[:, LANES:].astype(w_ref.dtype)
        intra = jnp.where(c <= r, _bdot(q, k, _NT) * decay, 0.0)
        a_ref[sl, :] = jnp.concatenate([intra, jnp.zeros_like(intra)], axis=1).astype(a_ref.dtype)
        qg_ref[sl, :] = (q * egc).astype(qg_ref.dtype)
        g_last = gc[GD_CHUNK - 1:GD_CHUNK, :]
        kd_ref[sl, :] = (k * jnp.exp(g_last - gc)).astype(kd_ref.dtype)


def gd_local(qkv, gate, batch, seq, *, rows=512):
    n = qkv.shape[0]
    rows = _tile(seq, rows)
    nr = seq // rows
    spec = lambda off: pl.BlockSpec((rows, LANES), lambda b, h, i: (b * nr + i, off + h))
    out_spec = pl.BlockSpec((rows, LANES), lambda b, h, i: (b * nr + i, h))
    bf = jax.ShapeDtypeStruct((n, HEAD_W), jnp.bfloat16)
    return pl.pallas_call(
        functools.partial(_gd_local_kernel, rows=rows),
        grid=(batch, GD_HEADS, nr),
        in_specs=[spec(0), spec(GD_HEADS), spec(2 * GD_HEADS),
                  pl.BlockSpec((rows, LANES), lambda b, h, i: (b * nr + i, 0))],
        out_specs=[out_spec] * 5,
        out_shape=[jax.ShapeDtypeStruct((n, HEAD_W), jnp.float32), bf, bf, bf, bf],
        compiler_params=_cparams("parallel", "parallel", "parallel"),
        name="gd_local",
    )(qkv, qkv, qkv, gate)


def _gd_scan_kernel(u_ref, w_ref, qg_ref, kd_ref, a_ref, gate_ref, z_ref, g_ref, o_ref, state_ref, *, seq):
    h = pl.program_id(1)
    state_ref[...] = jnp.zeros(state_ref.shape, jnp.float32)
    gain = g_ref[...]

    def chunk(ci, carry):
        start = pl.multiple_of(ci * GD_CHUNK, GD_CHUNK)
        sl = pl.ds(start, GD_CHUNK)
        state = state_ref[...]
        sb = state.astype(jnp.bfloat16)
        v_new = u_ref[sl, :] - jnp.dot(w_ref[sl, :], sb, preferred_element_type=jnp.float32)
        vb = v_new.astype(jnp.bfloat16)
        out = (jnp.dot(qg_ref[sl, :], sb, preferred_element_type=jnp.float32)
               + jnp.dot(a_ref[sl, :][:, :GD_CHUNK], vb, preferred_element_type=jnp.float32))
        last = gate_ref[pl.ds(start + GD_CHUNK - 1, 1), :]
        decay = jnp.exp(_lane_col(last, h))
        state_ref[...] = state * decay + lax.dot_general(
            kd_ref[sl, :], vb, _TN, preferred_element_type=jnp.float32)
        o_ref[sl, :] = (_rms(out, gain) * _silu(z_ref[sl, :].astype(jnp.float32))).astype(o_ref.dtype)
        return carry

    lax.fori_loop(0, seq // GD_CHUNK, chunk, 0)


def gd_scan(u, w, qg, kd, amat, gate, proj, norm_g, batch, seq):
    n = u.shape[0]
    spec = pl.BlockSpec((seq, LANES), lambda b, h: (b, h))
    return pl.pallas_call(
        functools.partial(_gd_scan_kernel, seq=seq),
        grid=(batch, GD_HEADS),
        in_specs=[spec, spec, spec, spec, spec,
                  pl.BlockSpec((seq, LANES), lambda b, h: (b, 0)),
                  pl.BlockSpec((seq, LANES), lambda b, h: (b, COL_GD_Z + h)),
                  pl.BlockSpec((1, LANES), lambda b, h: (0, 0))],
        out_specs=spec,
        out_shape=jax.ShapeDtypeStruct((n, HEAD_W), jnp.bfloat16),
        scratch_shapes=[pltpu.VMEM((GD_DIM, GD_DIM), jnp.float32)],
        compiler_params=_cparams("parallel", "parallel"),
        name="gd_scan",
    )(u, w, qg, kd, amat, gate, proj, norm_g.reshape(1, LANES))


def _merge_kernel(x_ref, oa_ref, ob_ref, oc_ref, ga_ref, gb_ref, gc_ref,
                  wa_ref, wb_ref, wc_ref, wo_ref, o_ref):
    def branch(o, g, w):
        return jax.nn.sigmoid(g[...].astype(jnp.float32)) * jnp.dot(
            o[...], w[...], preferred_element_type=jnp.float32)

    merged = (branch(oa_ref, ga_ref, wa_ref) + branch(ob_ref, gb_ref, wb_ref)
              + branch(oc_ref, gc_ref, wc_ref))
    o_ref[...] = x_ref[...] + jnp.dot(merged.astype(jnp.bfloat16), wo_ref[...],
                                      preferred_element_type=jnp.float32)


def merge_branches(x, o_a, o_b, o_c, proj, w_a, w_b, w_c, w_out, *, tm=512):
    n, d = x.shape
    tm = _tile(n, tm)
    gate0 = MIX_W // d
    row = lambda width: pl.BlockSpec((tm, width), lambda i: (i, 0))
    full = lambda w: pl.BlockSpec(w.shape, lambda i: (0, 0))
    return pl.pallas_call(
        _merge_kernel,
        grid=(n // tm,),
        in_specs=[row(d), row(HEAD_W), row(HEAD_W), row(HEAD_W),
                  pl.BlockSpec((tm, d), lambda i: (i, gate0)),
                  pl.BlockSpec((tm, d), lambda i: (i, gate0 + 1)),
                  pl.BlockSpec((tm, d), lambda i: (i, gate0 + 2)),
                  full(w_a), full(w_b), full(w_c), full(w_out)],
        out_specs=row(d),
        out_shape=jax.ShapeDtypeStruct((n, d), jnp.float32),
        compiler_params=_cparams("parallel"),
        name="merge_branches",
    )(x, o_a, o_b, o_c, proj, proj, proj, w_a, w_b, w_c, w_out)


def _cross_attn_kernel(x_ref, g_ref, wq_ref, kv_ref, wo_ref, o_ref):
    x = x_ref[...]
    q = jnp.dot(_rms(x, g_ref[...]).astype(jnp.bfloat16), wq_ref[...],
                preferred_element_type=jnp.float32) * X_DIM ** -0.5
    heads = []
    width = X_HEADS * X_DIM
    for h in range(X_HEADS):
        sl = slice(h * X_DIM, (h + 1) * X_DIM)
        k = kv_ref[:, sl]
        v = kv_ref[:, width + h * X_DIM: width + (h + 1) * X_DIM]
        s = _bdot(q[:, sl], k, _NT)
        p = jnp.exp(s - jnp.max(s, axis=1, keepdims=True))
        p = p / jnp.sum(p, axis=1, keepdims=True)
        heads.append(jnp.dot(p.astype(jnp.bfloat16), v, preferred_element_type=jnp.float32))
    o = jnp.concatenate(heads, axis=1).astype(jnp.bfloat16)
    o_ref[...] = x + jnp.dot(o, wo_ref[...], preferred_element_type=jnp.float32)


def cross_attention(x, g, w_q, kv, w_o, batch, seq, n_mem, *, tm=512):
    n, d = x.shape
    tm = _tile(seq, tm)
    per_seq = seq // tm
    full = lambda w: pl.BlockSpec(w.shape, lambda b, i: (0, 0))
    return pl.pallas_call(
        _cross_attn_kernel,
        grid=(batch, per_seq),
        in_specs=[pl.BlockSpec((tm, d), lambda b, i: (b * per_seq + i, 0)),
                  pl.BlockSpec((1, d), lambda b, i: (0, 0)),
                  full(w_q),
                  pl.BlockSpec((n_mem, kv.shape[1]), lambda b, i: (b, 0)),
                  full(w_o)],
        out_specs=pl.BlockSpec((tm, d), lambda b, i: (b * per_seq + i, 0)),
        out_shape=jax.ShapeDtypeStruct((n, d), jnp.float32),
        compiler_params=_cparams("parallel", "parallel"),
        name="cross_attention",
    )(x, g.reshape(1, d), w_q, kv, w_o)


def _router_kernel(x_ref, g_ref, w_ref, o_ref):
    logits = jnp.dot(_rms(x_ref[...], g_ref[...]), w_ref[...],
                     preferred_element_type=jnp.float32, precision=lax.Precision.HIGHEST)
    lane = lax.broadcasted_iota(jnp.int32, logits.shape, 1)
    logits = jnp.where(lane < N_EXPERTS, logits, NEG_BIG)
    m1 = jnp.max(logits, axis=1, keepdims=True)
    i1 = jnp.min(jnp.where(logits == m1, lane, LANES), axis=1, keepdims=True)
    rest = jnp.where(lane == i1, NEG_BIG, logits)
    m2 = jnp.max(rest, axis=1, keepdims=True)
    i2 = jnp.min(jnp.where(rest == m2, lane, LANES), axis=1, keepdims=True)
    e2 = jnp.exp(m2 - m1)
    w1 = 1.0 / (1.0 + e2)
    o_ref[...] = jnp.where(lane == i1, w1, jnp.where(lane == i2, e2 * w1, 0.0))


def route_top2(x, g, w_router, *, tm=1024):
    n, d = x.shape
    tm = _tile(n, tm)
    w = jnp.pad(w_router, ((0, 0), (0, LANES - N_EXPERTS)))
    return pl.pallas_call(
        _router_kernel,
        grid=(n // tm,),
        in_specs=[pl.BlockSpec((tm, d), lambda i: (i, 0)),
                  pl.BlockSpec((1, d), lambda i: (0, 0)),
                  pl.BlockSpec((d, LANES), lambda i: (0, 0))],
        out_specs=pl.BlockSpec((tm, LANES), lambda i: (i, 0)),
        out_shape=jax.ShapeDtypeStruct((n, LANES), jnp.float32),
        compiler_params=_cparams("parallel"),
        name="route_top2",
    )(x, g.reshape(1, d), w)


def _ffn_kernel(*refs, weighted, final_norm):
    if weighted:
        x_ref, g_ref, wg_ref, wu_ref, wd_ref, gf_ref, c_ref, o_ref, h_ref, acc_ref = refs
    else:
        x_ref, g_ref, wg_ref, wu_ref, wd_ref, gf_ref, o_ref, h_ref, acc_ref = refs
    e, f = pl.program_id(1), pl.program_id(2)

    @pl.when(jnp.logical_and(e == 0, f == 0))
    def _():
        h_ref[...] = _rms(x_ref[...], g_ref[...]).astype(h_ref.dtype)
        acc_ref[...] = x_ref[...]

    h = h_ref[...]
    act = (_silu(jnp.dot(h, wg_ref[...], preferred_element_type=jnp.float32))
           * jnp.dot(h, wu_ref[...], preferred_element_type=jnp.float32))
    y = jnp.dot(act.astype(jnp.bfloat16), wd_ref[...], preferred_element_type=jnp.float32)
    if weighted:
        y = y * _lane_col(c_ref[...], e)
    acc_ref[...] += y

    @pl.when(jnp.logical_and(e == pl.num_programs(1) - 1, f == pl.num_programs(2) - 1))
    def _():
        out = acc_ref[...]
        if final_norm:
            out = _rms(out, gf_ref[...])
        o_ref[...] = out


def swiglu_ffn(x, g, w_gate, w_up, w_down, final_gain, combine=None, *, final_norm, tm=1024, tf=512):
    n, d = x.shape
    n_e, _, ff = w_gate.shape
    tm = _tile(n, tm)
    tf = tf if ff % tf == 0 else ff // 2
    weighted = combine is not None
    in_specs = [pl.BlockSpec((tm, d), lambda i, e, f: (i, 0)),
                pl.BlockSpec((1, d), lambda i, e, f: (0, 0)),
                pl.BlockSpec((None, d, tf), lambda i, e, f: (e, 0, f)),
                pl.BlockSpec((None, d, tf), lambda i, e, f: (e, 0, f)),
                pl.BlockSpec((None, tf, d), lambda i, e, f: (e, f, 0)),
                pl.BlockSpec((1, d), lambda i, e, f: (0, 0))]
    args = [x, g.reshape(1, d), w_gate, w_up, w_down, final_gain.reshape(1, d)]
    if weighted:
        in_specs.append(pl.BlockSpec((tm, LANES), lambda i, e, f: (i, 0)))
        args.append(combine)
    return pl.pallas_call(
        functools.partial(_ffn_kernel, weighted=weighted, final_norm=final_norm),
        grid=(n // tm, n_e, ff // tf),
        in_specs=in_specs,
        out_specs=pl.BlockSpec((tm, d), lambda i, e, f: (i, 0)),
        out_shape=jax.ShapeDtypeStruct((n, d), jnp.float32),
        scratch_shapes=[pltpu.VMEM((tm, d), jnp.bfloat16), pltpu.VMEM((tm, d), jnp.float32)],
        compiler_params=_cparams("parallel", "arbitrary", "arbitrary"),
        name="swiglu_ffn",
    )(*args)


def _rope_tables(seq):
    half = DA_DIM // 2
    inv_freq = 1.0 / (ROPE_THETA ** (jnp.arange(0, DA_DIM, 2, dtype=jnp.float32) / DA_DIM))
    ang = jnp.arange(seq, dtype=jnp.float32)[:, None] * inv_freq[None, :]
    cos, sin = jnp.cos(ang), jnp.sin(ang)
    reps = LANES // half
    sign = jnp.tile(jnp.concatenate([-jnp.ones((half,)), jnp.ones((half,))]), reps // 2)
    return jnp.tile(cos, (1, reps)), jnp.tile(sin, (1, reps)) * sign[None, :]


def kernel(x, mem, norm_mix, w_in, gd_conv, lam_q1, lam_k1, lam_q2, lam_k2, da_subln, gd_a_log,
           gd_dt_bias, gd_norm, w_br_a, w_br_b, w_br_c, w_out, norm_x, norm_mem, w_q_x, w_kv_x,
           w_o_x, norm_ffn, w_gate_dense, w_up_dense, w_down_dense, w_router, w_gate_exp,
           w_up_exp, w_down_exp, norm_final):
    batch, seq, d = x.shape
    n_mem = mem.shape[1]
    depth = norm_mix.shape[0]
    bf = jnp.bfloat16
    n = batch * seq
    cos, sin = _rope_tables(seq)
    xf = x.reshape(n, d)
    memf = mem.reshape(batch * n_mem, d)
    ab0 = MIX_W
    ab1 = MIX_W + 2 * GD_HEADS

    for l in range(depth):
        lam_init = 0.8 - 0.6 * math.exp(-0.3 * l)
        w_l = w_in[l]
        w_main = jnp.concatenate([w_l[:, :ab0], w_l[:, ab1:]], axis=1).astype(bf)
        w_ab = jnp.pad(w_l[:, ab0:ab1], ((0, 0), (0, LANES - 2 * GD_HEADS))).astype(bf)
        proj = rms_matmul(xf, norm_mix[l], w_main, bf)
        ab = rms_matmul(xf, norm_mix[l], w_ab, jnp.float32)

        q_rope, k_rope = rope_qk(proj, cos, sin, seq)
        lam_params = jnp.stack([lam_q1[l], lam_k1[l], lam_q2[l], lam_k2[l]])
        o_a = diff_attention(q_rope, k_rope, proj, lam_params, da_subln[l], batch, seq, lam_init)
        o_b = stick_breaking_attention(proj, batch, seq)
        qkv = gd_conv_silu(proj, gd_conv[l], batch, seq)
        gate = gd_gates(ab, gd_a_log[l], gd_dt_bias[l])
        u, w, qg, kd, amat = gd_local(qkv, gate, batch, seq)
        o_c = gd_scan(u, w, qg, kd, amat, gate, proj, gd_norm[l], batch, seq)
        xf = merge_branches(xf, o_a, o_b, o_c, proj, w_br_a[l].astype(bf), w_br_b[l].astype(bf),
                            w_br_c[l].astype(bf), w_out[l].astype(bf))

        kv = rms_matmul(memf, norm_mem[l], w_kv_x[l].astype(bf), bf)
        xf = cross_attention(xf, norm_x[l], w_q_x[l].astype(bf), kv, w_o_x[l].astype(bf),
                             batch, seq, n_mem)

        last = l == depth - 1
        i = l // 2
        if l % 2 == 0:
            xf = swiglu_ffn(xf, norm_ffn[l], w_gate_dense[i][None].astype(bf),
                            w_up_dense[i][None].astype(bf), w_down_dense[i][None].astype(bf),
                            norm_final, final_norm=last)
        else:
            combine = route_top2(xf, norm_ffn[l], w_router[i])
            xf = swiglu_ffn(xf, norm_ffn[l], w_gate_exp[i].astype(bf), w_up_exp[i].astype(bf),
                            w_down_exp[i].astype(bf), norm_final, combine, final_norm=last)
    return xf.reshape(batch, seq, d)
```

```python
import functools
import math

import jax
import jax.numpy as jnp
from jax import lax
from jax.experimental import pallas as pl
from jax.experimental.pallas import tpu as pltpu

EPS = 1e-6
ROPE_THETA = 10000.0
LANES = 128

DA_HEADS = 4
DA_DIM = 64
SB_HEADS = 4
SB_DIM = 128
GD_HEADS = 4
GD_DIM = 128
GD_CONV = 4
GD_CHUNK = 64
X_HEADS = 4
X_DIM = 128
N_EXPERTS = 8
HEAD_W = 512

COL_DA_Q, COL_DA_K, COL_DA_V = 0, 4, 8
COL_SB_Q, COL_SB_K, COL_SB_V = 12, 16, 20
COL_GD_Q, COL_GD_Z = 24, 36
MIX_W = 5120

SB_LOG_FLOOR = -110.0
NEG_BIG = -1e30
VMEM_LIMIT = 48 * 1024 * 1024

_NT = (((1,), (1,)), ((), ()))
_TN = (((0,), (0,)), ((), ()))


def _cparams(*sem):
    return pltpu.CompilerParams(dimension_semantics=sem, vmem_limit_bytes=VMEM_LIMIT)


def _tile(n, pref):
    t = min(n, pref)
    while n % t:
        t //= 2
    return t


def _bdot(a, b, dims=None):
    a = a.astype(jnp.bfloat16)
    b = b.astype(jnp.bfloat16)
    if dims is None:
        return jnp.dot(a, b, preferred_element_type=jnp.float32)
    return lax.dot_general(a, b, dims, preferred_element_type=jnp.float32)


def _dot_split(a, b_exact):
    hi = a.astype(jnp.bfloat16)
    lo = (a - hi.astype(jnp.float32)).astype(jnp.bfloat16)
    return (jnp.dot(hi, b_exact, preferred_element_type=jnp.float32)
            + jnp.dot(lo, b_exact, preferred_element_type=jnp.float32))


def _rms(x, g):
    return x * lax.rsqrt(jnp.mean(x * x, axis=-1, keepdims=True) + EPS) * g


def _softplus(z):
    return jnp.maximum(z, 0.0) + jnp.log1p(jnp.exp(-jnp.abs(z)))


def _silu(x):
    return x * jax.nn.sigmoid(x)


def _lane_tile(x, reps):
    return jnp.concatenate([x] * reps, axis=1)


def _lane_col(x, lane):
    ids = lax.broadcasted_iota(jnp.int32, x.shape, 1)
    return jnp.sum(jnp.where(ids == lane, x, 0.0), axis=1, keepdims=True)


def _rms_matmul_kernel(x_ref, g_ref, w_ref, o_ref, h_ref, *, exact):
    @pl.when(pl.program_id(1) == 0)
    def _():
        h_ref[...] = _rms(x_ref[...], g_ref[...]).astype(h_ref.dtype)

    if exact:
        acc = jnp.dot(h_ref[...], w_ref[...], preferred_element_type=jnp.float32,
                      precision=lax.Precision.HIGHEST)
    else:
        acc = jnp.dot(h_ref[...], w_ref[...], preferred_element_type=jnp.float32)
    o_ref[...] = acc.astype(o_ref.dtype)


def rms_matmul(x, g, w, out_dtype, *, exact=False, tm=1024, tn=1024):
    n, d = x.shape
    nout = w.shape[1]
    tm, tn = _tile(n, tm), _tile(nout, tn)
    return pl.pallas_call(
        functools.partial(_rms_matmul_kernel, exact=exact),
        grid=(n // tm, nout // tn),
        in_specs=[pl.BlockSpec((tm, d), lambda i, j: (i, 0)),
                  pl.BlockSpec((1, d), lambda i, j: (0, 0)),
                  pl.BlockSpec((d, tn), lambda i, j: (0, j))],
        out_specs=pl.BlockSpec((tm, tn), lambda i, j: (i, j)),
        out_shape=jax.ShapeDtypeStruct((n, nout), out_dtype),
        scratch_shapes=[pltpu.VMEM((tm, d), w.dtype)],
        compiler_params=_cparams("parallel", "arbitrary"),
        name="rms_matmul",
    )(x, g.reshape(1, d), w)


def _rope_kernel(q_ref, k_ref, cos_ref, sin_ref, qo_ref, ko_ref):
    cos, sin = cos_ref[...], sin_ref[...]
    lane = lax.broadcasted_iota(jnp.int32, cos.shape, 1)
    first_half = (lane % DA_DIM) < (DA_DIM // 2)

    def rot(t):
        partner = jnp.where(first_half, pltpu.roll(t, LANES - DA_DIM // 2, 1),
                            pltpu.roll(t, DA_DIM // 2, 1))
        return t * cos + partner * sin

    for h in range(DA_HEADS):
        sl = slice(h * LANES, (h + 1) * LANES)
        qo_ref[:, sl] = (rot(q_ref[:, sl].astype(jnp.float32)) * DA_DIM ** -0.5).astype(qo_ref.dtype)
        ko_ref[:, sl] = rot(k_ref[:, sl].astype(jnp.float32)).astype(ko_ref.dtype)


def rope_qk(proj, cos, sin, seq, *, ts=1024):
    n = proj.shape[0]
    ts = _tile(seq, ts)
    per_seq = seq // ts
    out = jax.ShapeDtypeStruct((n, HEAD_W), jnp.bfloat16)
    return pl.pallas_call(
        _rope_kernel,
        grid=(n // ts,),
        in_specs=[pl.BlockSpec((ts, HEAD_W), lambda i: (i, COL_DA_Q // 4)),
                  pl.BlockSpec((ts, HEAD_W), lambda i: (i, COL_DA_K // 4)),
                  pl.BlockSpec((ts, LANES), lambda i: (i % per_seq, 0)),
                  pl.BlockSpec((ts, LANES), lambda i: (i % per_seq, 0))],
        out_specs=[pl.BlockSpec((ts, HEAD_W), lambda i: (i, 0)),
                   pl.BlockSpec((ts, HEAD_W), lambda i: (i, 0))],
        out_shape=[out, out],
        compiler_params=_cparams("parallel"),
        name="rope_qk",
    )(proj, proj, cos, sin)


def _diff_attn_kernel(lam_ref, q_ref, k_ref, v_ref, subln_ref, o_ref,
                      qs_ref, sa_ref, sb_ref, m_ref, l_ref, acc_ref, *, t, lam_init):
    i = pl.program_id(2)
    q = q_ref[...]
    lane = lax.broadcasted_iota(jnp.int32, q.shape, 1)
    zero = jnp.zeros_like(q)
    qs_ref[:t, :] = jnp.where(lane < DA_DIM, q, zero)
    qs_ref[t:, :] = jnp.where(lane >= DA_DIM, q, zero)
    m_ref[...] = jnp.full(m_ref.shape, NEG_BIG, jnp.float32)
    l_ref[...] = jnp.zeros(l_ref.shape, jnp.float32)
    acc_ref[...] = jnp.zeros(acc_ref.shape, jnp.float32)
    reps = t // LANES

    def scores(j, s_ref):
        k = k_ref[pl.ds(pl.multiple_of(jnp.maximum(j, 0) * t, t), t), :]
        s_ref[...] = lax.dot_general(qs_ref[...], k, _NT, preferred_element_type=jnp.float32)

    def consume(j, s_ref, masked):
        s = s_ref[...]
        if masked:
            row = lax.broadcasted_iota(jnp.int32, s.shape, 0) % t
            col = lax.broadcasted_iota(jnp.int32, s.shape, 1)
            s = jnp.where(col <= row, s, NEG_BIG)
        m_old = m_ref[...]
        m_new = jnp.maximum(m_old, jnp.max(s, axis=1, keepdims=True))
        alpha = jnp.exp(m_old - m_new)
        p = jnp.exp(s - _lane_tile(m_new, reps))
        l_ref[...] = alpha * l_ref[...] + jnp.sum(p, axis=1, keepdims=True)
        v = v_ref[pl.ds(pl.multiple_of(j * t, t), t), :]
        acc_ref[...] = alpha * acc_ref[...] + jnp.dot(
            p.astype(jnp.bfloat16), v, preferred_element_type=jnp.float32)
        m_ref[...] = m_new

    scores(i, sa_ref)
    scores(i - 1, sb_ref)
    consume(i, sa_ref, True)

    def earlier_pair(n, carry):
        j = i - 1 - 2 * n
        scores(j - 1, sa_ref)
        consume(j, sb_ref, False)
        scores(j - 2, sb_ref)
        consume(j - 1, sa_ref, False)
        return carry

    lax.fori_loop(0, i // 2, earlier_pair, 0)

    @pl.when(i % 2 == 1)
    def _():
        consume(0, sb_ref, False)

    lq1, lk1, lq2, lk2 = (lam_ref[r:r + 1, :] for r in range(4))
    lam = (jnp.exp(jnp.sum(lq1 * lk1, axis=1, keepdims=True))
           - jnp.exp(jnp.sum(lq2 * lk2, axis=1, keepdims=True)) + lam_init)
    acc = acc_ref[...] / l_ref[...]
    o = acc[:t] - lam * acc[t:]
    o_ref[...] = (_rms(o, subln_ref[...]) * (1.0 - lam_init)).astype(o_ref.dtype)


def diff_attention(q_rope, k_rope, proj, lam_params, subln, batch, seq, lam_init, *, t=512):
    n = q_rope.shape[0]
    t = _tile(seq, t)
    nq = seq // t
    return pl.pallas_call(
        functools.partial(_diff_attn_kernel, t=t, lam_init=lam_init),
        grid=(batch, DA_HEADS, nq),
        in_specs=[pl.BlockSpec((4, DA_DIM), lambda b, h, i: (0, 0)),
                  pl.BlockSpec((t, LANES), lambda b, h, i: (b * nq + i, h)),
                  pl.BlockSpec((seq, LANES), lambda b, h, i: (b, h)),
                  pl.BlockSpec((seq, LANES), lambda b, h, i: (b, COL_DA_V + h)),
                  pl.BlockSpec((1, LANES), lambda b, h, i: (0, 0))],
        out_specs=pl.BlockSpec((t, LANES), lambda b, h, i: (b * nq + i, h)),
        out_shape=jax.ShapeDtypeStruct((n, HEAD_W), jnp.bfloat16),
        scratch_shapes=[pltpu.VMEM((2 * t, LANES), jnp.bfloat16),
                        pltpu.VMEM((2 * t, t), jnp.float32),
                        pltpu.VMEM((2 * t, t), jnp.float32),
                        pltpu.VMEM((2 * t, LANES), jnp.float32),
                        pltpu.VMEM((2 * t, LANES), jnp.float32),
                        pltpu.VMEM((2 * t, LANES), jnp.float32)],
        compiler_params=_cparams("parallel", "parallel", "arbitrary"),
        name="diff_attention",
    )(lam_params, q_rope, k_rope, proj, subln.reshape(1, LANES))


def _sb_attn_kernel(q_ref, k_ref, v_ref, o_ref, za_ref, zb_ref, carry_ref, acc_ref, *, tq):
    i = pl.program_id(2)
    qs_val = (q_ref[...].astype(jnp.float32) * SB_DIM ** -0.5).astype(jnp.bfloat16)
    carry_ref[...] = jnp.zeros(carry_ref.shape, jnp.float32)
    acc_ref[...] = jnp.zeros(acc_ref.shape, jnp.float32)
    r = lax.broadcasted_iota(jnp.int32, (tq, tq), 0)
    c = lax.broadcasted_iota(jnp.int32, (tq, tq), 1)
    later = (r > c).astype(jnp.bfloat16)
    reps = tq // LANES

    def tile_rows(j):
        return pl.ds(pl.multiple_of(jnp.maximum(j, 0) * tq, tq), tq)

    def logits(j, z_ref):
        z_ref[...] = lax.dot_general(qs_val, k_ref[tile_rows(j), :], _NT, preferred_element_type=jnp.float32)

    def consume(j, z_ref, masked):
        z = z_ref[...]
        sp = _softplus(z)
        log_fail = jnp.where(c < r, -sp, 0.0) if masked else -sp
        log_remain = _lane_tile(carry_ref[...], reps) + _dot_split(log_fail, later)
        attn = jnp.exp(z - sp + log_remain)
        if masked:
            attn = jnp.where(c < r, attn, 0.0)
        acc_ref[...] += jnp.dot(attn.astype(jnp.bfloat16), v_ref[tile_rows(j), :],
                                preferred_element_type=jnp.float32)
        carry_ref[...] += jnp.sum(log_fail, axis=1, keepdims=True)

    def consume_if_exists(j, z_ref):
        carry_ref[...] = jnp.where(j < 0, NEG_BIG, carry_ref[...])
        consume(j, z_ref, False)

    logits(i, za_ref)
    logits(i - 1, zb_ref)
    consume(i, za_ref, True)
    logits(i - 2, za_ref)
    consume_if_exists(i - 1, zb_ref)

    def cond(state):
        j, alive = state
        return jnp.logical_and(j >= 0, alive)

    def body(state):
        j, _ = state
        logits(j - 1, zb_ref)
        consume(j, za_ref, False)
        logits(j - 2, za_ref)
        consume_if_exists(j - 1, zb_ref)
        return j - 2, jnp.max(carry_ref[...]) > SB_LOG_FLOOR

    lax.while_loop(cond, body, (i - 2, jnp.max(carry_ref[...]) > SB_LOG_FLOOR))
    o_ref[...] = acc_ref[...].astype(o_ref.dtype)


def stick_breaking_attention(proj, batch, seq, *, tq=256):
    n = proj.shape[0]
    tq = _tile(seq, tq)
    nq = seq // tq
    return pl.pallas_call(
        functools.partial(_sb_attn_kernel, tq=tq),
        grid=(batch, SB_HEADS, nq),
        in_specs=[pl.BlockSpec((tq, LANES), lambda b, h, i: (b * nq + i, COL_SB_Q + h)),
                  pl.BlockSpec((seq, LANES), lambda b, h, i: (b, COL_SB_K + h)),
                  pl.BlockSpec((seq, LANES), lambda b, h, i: (b, COL_SB_V + h))],
        out_specs=pl.BlockSpec((tq, LANES), lambda b, h, i: (b * nq + i, h)),
        out_shape=jax.ShapeDtypeStruct((n, HEAD_W), jnp.bfloat16),
        scratch_shapes=[pltpu.VMEM((tq, tq), jnp.float32),
                        pltpu.VMEM((tq, tq), jnp.float32),
                        pltpu.VMEM((tq, LANES), jnp.float32),
                        pltpu.VMEM((tq, LANES), jnp.float32)],
        compiler_params=_cparams("parallel", "parallel", "arbitrary"),
        name="stick_breaking_attention",
    )(proj, proj, proj)


def _gd_conv_kernel(x_ref, w_ref, o_ref):
    c = pl.program_id(1)
    x = x_ref[...].astype(jnp.float32)
    w = w_ref[...]
    row = lax.broadcasted_iota(jnp.int32, x.shape, 0)
    y = x * w[GD_CONV - 1:GD_CONV, :]
    for back in range(1, GD_CONV):
        shifted = jnp.where(row >= back, pltpu.roll(x, back, 0), 0.0)
        y = y + shifted * w[GD_CONV - 1 - back:GD_CONV - back, :]
    y = _silu(y)
    inv = lax.rsqrt(jnp.sum(y * y, axis=1, keepdims=True) + EPS)
    is_q = c < GD_HEADS
    is_v = c >= 2 * GD_HEADS
    scale = jnp.where(is_v, 1.0, inv * jnp.where(is_q, GD_DIM ** -0.5, 1.0))
    o_ref[...] = (y * scale).astype(o_ref.dtype)


def gd_conv_silu(proj, conv_w, batch, seq):
    n = proj.shape[0]
    ncol = 3 * GD_HEADS
    return pl.pallas_call(
        _gd_conv_kernel,
        grid=(batch, ncol),
        in_specs=[pl.BlockSpec((seq, LANES), lambda b, c: (b, COL_GD_Q + c)),
                  pl.BlockSpec((GD_CONV, LANES), lambda b, c: (0, c))],
        out_specs=pl.BlockSpec((seq, LANES), lambda b, c: (b, c)),
        out_shape=jax.ShapeDtypeStruct((n, ncol * LANES), jnp.float32),
        compiler_params=_cparams("parallel", "parallel"),
        name="gd_conv",
    )(proj, conv_w)


def _gd_gate_kernel(ab_ref, alog_ref, dtb_ref, o_ref, *, rows):
    ab = ab_ref[...]
    is_decay_lane = lax.broadcasted_iota(jnp.int32, (GD_CHUNK, LANES), 1) < GD_HEADS
    g = -jnp.exp(alog_ref[...]) * _softplus(ab + dtb_ref[...])
    beta = jax.nn.sigmoid(ab)
    r = lax.broadcasted_iota(jnp.int32, (GD_CHUNK, GD_CHUNK), 0)
    c = lax.broadcasted_iota(jnp.int32, (GD_CHUNK, GD_CHUNK), 1)
    tril = (c <= r).astype(jnp.bfloat16)
    for ci in range(rows // GD_CHUNK):
        sl = slice(ci * GD_CHUNK, (ci + 1) * GD_CHUNK)
        hi = g[sl].astype(jnp.bfloat16)
        mid_f = g[sl] - hi.astype(jnp.float32)
        mid = mid_f.astype(jnp.bfloat16)
        lo = (mid_f - mid.astype(jnp.float32)).astype(jnp.bfloat16)
        gc = (jnp.dot(tril, hi, preferred_element_type=jnp.float32)
              + jnp.dot(tril, mid, preferred_element_type=jnp.float32)
              + jnp.dot(tril, lo, preferred_element_type=jnp.float32))
        o_ref[sl, :] = jnp.where(is_decay_lane, gc, beta[sl])


def gd_gates(ab, a_log, dt_bias, *, rows=512):
    n = ab.shape[0]
    rows = _tile(n, rows)
    pad = jnp.zeros((LANES - GD_HEADS,), jnp.float32)
    alog = jnp.concatenate([a_log, pad]).reshape(1, LANES)
    dtb = jnp.concatenate([dt_bias, pad]).reshape(1, LANES)
    return pl.pallas_call(
        functools.partial(_gd_gate_kernel, rows=rows),
        grid=(n // rows,),
        in_specs=[pl.BlockSpec((rows, LANES), lambda i: (i, 0)),
                  pl.BlockSpec((1, LANES), lambda i: (0, 0)),
                  pl.BlockSpec((1, LANES), lambda i: (0, 0))],
        out_specs=pl.BlockSpec((rows, LANES), lambda i: (i, 0)),
        out_shape=jax.ShapeDtypeStruct((n, LANES), jnp.float32),
        compiler_params=_cparams("parallel"),
        name="gd_gates",
    )(ab, alog, dtb)


def _unit_lower_inverse(lmats, r, c):
    eye = (r == c).astype(jnp.float32)
    diag16 = (r // 16) == (c // 16)
    ps = [jnp.where(diag16, lm, 0.0) for lm in lmats]
    ts = [eye - p for p in ps]
    for _ in range(3):
        ps = [_bdot(p, p) for p in ps]
        ts = [t + _bdot(t, p) for t, p in zip(ts, ps)]
    for blk in (32, 64):
        band = jnp.logical_and((r // blk) == (c // blk), (r // (blk // 2)) != (c // (blk // 2)))
        tcs = [_bdot(t, jnp.where(band, lm, 0.0)) for t, lm in zip(ts, lmats)]
        ts = [t - _bdot(tc, t) for t, tc in zip(ts, tcs)]
    return ts


def _gd_local_kernel(q_ref, k_ref, v_ref, gate_ref, u_ref, w_ref, qg_ref, kd_ref, a_ref, *, rows):
    h = pl.program_id(1)
    gate = gate_ref[...]
    gc_all = _lane_col(gate, h)
    beta_all = _lane_col(gate, GD_HEADS + h)
    r = lax.broadcasted_iota(jnp.int32, (GD_CHUNK, GD_CHUNK), 0)
    c = lax.broadcasted_iota(jnp.int32, (GD_CHUNK, GD_CHUNK), 1)
    sls = [slice(ci * GD_CHUNK, (ci + 1) * GD_CHUNK) for ci in range(rows // GD_CHUNK)]
    qs = [q_ref[sl, :] for sl in sls]
    ks = [k_ref[sl, :] for sl in sls]
    gcs = [gc_all[sl] for sl in sls]
    kbs = [k * beta_all[sl] for k, sl in zip(ks, sls)]
    kks = [_bdot(kb, k, _NT) for kb, k in zip(kbs, ks)]
    qks = [_bdot(q, k, _NT) for q, k in zip(qs, ks)]
    decays = []
    for gc in gcs:
        gc_rows = jnp.broadcast_to(gc, (GD_CHUNK, LANES))
        gc_cols = jnp.transpose(gc_rows)[0:1, :GD_CHUNK]
        decays.append(jnp.exp(jnp.where(c <= r, gc_rows[:, :GD_CHUNK] - gc_cols, 0.0)))
    tmats = _unit_lower_inverse([jnp.where(c < r, kk * d, 0.0) for kk, d in zip(kks, decays)], r, c)
    egcs = [jnp.exp(gc) for gc in gcs]
    uws = [_bdot(t, jnp.concatenate([v_ref[sl, :] * beta_all[sl], kb * egc], axis=1))
           for t, sl, kb, egc in zip(tmats, sls, kbs, egcs)]
    for sl, q, k, gc, egc, qk, d, uw in zip(sls, qs, ks, gcs, egcs, qks, decays, uws):
        u_ref[sl, :] = uw[:, :LANES]
        w_ref[sl, :] = uw[:, LANES:].astype(w_ref.dtype)
        intra = jnp.where(c <= r, qk * d, 0.0)
        a_ref[sl, :] = jnp.concatenate([intra, jnp.zeros_like(intra)], axis=1).astype(a_ref.dtype)
        qg_ref[sl, :] = (q * egc).astype(qg_ref.dtype)
        g_last = gc[GD_CHUNK - 1:GD_CHUNK, :]
        kd_ref[sl, :] = (k * jnp.exp(g_last - gc)).astype(kd_ref.dtype)


def gd_local(qkv, gate, batch, seq, *, rows=1024):
    n = qkv.shape[0]
    rows = _tile(seq, rows)
    nr = seq // rows
    spec = lambda off: pl.BlockSpec((rows, LANES), lambda b, h, i: (b * nr + i, off + h))
    out_spec = pl.BlockSpec((rows, LANES), lambda b, h, i: (b * nr + i, h))
    bf = jax.ShapeDtypeStruct((n, HEAD_W), jnp.bfloat16)
    return pl.pallas_call(
        functools.partial(_gd_local_kernel, rows=rows),
        grid=(batch, GD_HEADS, nr),
        in_specs=[spec(0), spec(GD_HEADS), spec(2 * GD_HEADS),
                  pl.BlockSpec((rows, LANES), lambda b, h, i: (b * nr + i, 0))],
        out_specs=[out_spec] * 5,
        out_shape=[jax.ShapeDtypeStruct((n, HEAD_W), jnp.float32), bf, bf, bf, bf],
        compiler_params=_cparams("parallel", "parallel", "parallel"),
        name="gd_local",
    )(qkv, qkv, qkv, gate)


def _gd_scan_kernel(u_ref, w_ref, qg_ref, kd_ref, a_ref, gate_ref, z_ref, g_ref, o_ref, state_ref, *, rows):
    @pl.when(pl.program_id(1) == 0)
    def _():
        state_ref[...] = jnp.zeros(state_ref.shape, jnp.float32)

    gain = g_ref[...]
    heads = range(GD_HEADS)
    cols = [slice(h * LANES, (h + 1) * LANES) for h in heads]

    def chunk(ci, carry):
        start = pl.multiple_of(ci * GD_CHUNK, GD_CHUNK)
        sl = pl.ds(start, GD_CHUNK)
        states = [state_ref[h] for h in heads]
        sbs = [s.astype(jnp.bfloat16) for s in states]
        wq = [jnp.dot(jnp.concatenate([w_ref[sl, cs], qg_ref[sl, cs]], axis=0), sb,
                      preferred_element_type=jnp.float32) for cs, sb in zip(cols, sbs)]
        vbs = [(u_ref[sl, cs] - x[:GD_CHUNK]).astype(jnp.bfloat16) for cs, x in zip(cols, wq)]
        decay = jnp.exp(gate_ref[pl.ds(start + GD_CHUNK - 1, 1), :])
        for h, cs in zip(heads, cols):
            state_ref[h] = states[h] * _lane_col(decay, h) + lax.dot_general(
                kd_ref[sl, cs], vbs[h], _TN, preferred_element_type=jnp.float32)
        outs = [x[GD_CHUNK:] + jnp.dot(a_ref[sl, cs][:, :GD_CHUNK], vb, preferred_element_type=jnp.float32)
                for cs, x, vb in zip(cols, wq, vbs)]
        for cs, out in zip(cols, outs):
            o_ref[sl, cs] = (_rms(out, gain) * _silu(z_ref[sl, cs].astype(jnp.float32))).astype(o_ref.dtype)
        return carry

    lax.fori_loop(0, rows // GD_CHUNK, chunk, 0)


def gd_scan(u, w, qg, kd, amat, gate, proj, norm_g, batch, seq, *, rows=1024):
    n = u.shape[0]
    rows = _tile(seq, rows)
    nr = seq // rows
    spec = pl.BlockSpec((rows, HEAD_W), lambda b, i: (b * nr + i, 0))
    return pl.pallas_call(
        functools.partial(_gd_scan_kernel, rows=rows),
        grid=(batch, nr),
        in_specs=[spec, spec, spec, spec, spec,
                  pl.BlockSpec((rows, LANES), lambda b, i: (b * nr + i, 0)),
                  pl.BlockSpec((rows, HEAD_W), lambda b, i: (b * nr + i, COL_GD_Z // GD_HEADS)),
                  pl.BlockSpec((1, LANES), lambda b, i: (0, 0))],
        out_specs=spec,
        out_shape=jax.ShapeDtypeStruct((n, HEAD_W), jnp.bfloat16),
        scratch_shapes=[pltpu.VMEM((GD_HEADS, GD_DIM, GD_DIM), jnp.float32)],
        compiler_params=_cparams("parallel", "arbitrary"),
        name="gd_scan",
    )(u, w, qg, kd, amat, gate, proj, norm_g.reshape(1, LANES))


def _merge_kernel(x_ref, oa_ref, ob_ref, oc_ref, ga_ref, gb_ref, gc_ref,
                  wa_ref, wb_ref, wc_ref, wo_ref, o_ref):
    def branch(o, g, w):
        return jax.nn.sigmoid(g[...].astype(jnp.float32)) * jnp.dot(
            o[...], w[...], preferred_element_type=jnp.float32)

    merged = (branch(oa_ref, ga_ref, wa_ref) + branch(ob_ref, gb_ref, wb_ref)
              + branch(oc_ref, gc_ref, wc_ref))
    o_ref[...] = x_ref[...] + jnp.dot(merged.astype(jnp.bfloat16), wo_ref[...],
                                      preferred_element_type=jnp.float32)


def merge_branches(x, o_a, o_b, o_c, proj, w_a, w_b, w_c, w_out, *, tm=512):
    n, d = x.shape
    tm = _tile(n, tm)
    gate0 = MIX_W // d
    row = lambda width: pl.BlockSpec((tm, width), lambda i: (i, 0))
    full = lambda w: pl.BlockSpec(w.shape, lambda i: (0, 0))
    return pl.pallas_call(
        _merge_kernel,
        grid=(n // tm,),
        in_specs=[row(d), row(HEAD_W), row(HEAD_W), row(HEAD_W),
                  pl.BlockSpec((tm, d), lambda i: (i, gate0)),
                  pl.BlockSpec((tm, d), lambda i: (i, gate0 + 1)),
                  pl.BlockSpec((tm, d), lambda i: (i, gate0 + 2)),
                  full(w_a), full(w_b), full(w_c), full(w_out)],
        out_specs=row(d),
        out_shape=jax.ShapeDtypeStruct((n, d), jnp.float32),
        compiler_params=_cparams("parallel"),
        name="merge_branches",
    )(x, o_a, o_b, o_c, proj, proj, proj, w_a, w_b, w_c, w_out)


def _cross_attn_kernel(x_ref, g_ref, wq_ref, kv_ref, wo_ref, o_ref):
    x = x_ref[...]
    q = jnp.dot(_rms(x, g_ref[...]).astype(jnp.bfloat16), wq_ref[...],
                preferred_element_type=jnp.float32) * X_DIM ** -0.5
    heads = []
    width = X_HEADS * X_DIM
    for h in range(X_HEADS):
        sl = slice(h * X_DIM, (h + 1) * X_DIM)
        k = kv_ref[:, sl]
        v = kv_ref[:, width + h * X_DIM: width + (h + 1) * X_DIM]
        s = _bdot(q[:, sl], k, _NT)
        p = jnp.exp(s - jnp.max(s, axis=1, keepdims=True))
        p = p / jnp.sum(p, axis=1, keepdims=True)
        heads.append(jnp.dot(p.astype(jnp.bfloat16), v, preferred_element_type=jnp.float32))
    o = jnp.concatenate(heads, axis=1).astype(jnp.bfloat16)
    o_ref[...] = x + jnp.dot(o, wo_ref[...], preferred_element_type=jnp.float32)


def cross_attention(x, g, w_q, kv, w_o, batch, seq, n_mem, *, tm=512):
    n, d = x.shape
    tm = _tile(seq, tm)
    per_seq = seq // tm
    full = lambda w: pl.BlockSpec(w.shape, lambda b, i: (0, 0))
    return pl.pallas_call(
        _cross_attn_kernel,
        grid=(batch, per_seq),
        in_specs=[pl.BlockSpec((tm, d), lambda b, i: (b * per_seq + i, 0)),
                  pl.BlockSpec((1, d), lambda b, i: (0, 0)),
                  full(w_q),
                  pl.BlockSpec((n_mem, kv.shape[1]), lambda b, i: (b, 0)),
                  full(w_o)],
        out_specs=pl.BlockSpec((tm, d), lambda b, i: (b * per_seq + i, 0)),
        out_shape=jax.ShapeDtypeStruct((n, d), jnp.float32),
        compiler_params=_cparams("parallel", "parallel"),
        name="cross_attention",
    )(x, g.reshape(1, d), w_q, kv, w_o)


def _swiglu_step(h, wg_ref, wu_ref, wd_ref):
    act = (_silu(jnp.dot(h, wg_ref[...], preferred_element_type=jnp.float32))
           * jnp.dot(h, wu_ref[...], preferred_element_type=jnp.float32))
    return jnp.dot(act.astype(jnp.bfloat16), wd_ref[...], preferred_element_type=jnp.float32)


def _ffn_kernel(x_ref, g_ref, wg_ref, wu_ref, wd_ref, gf_ref, o_ref, h_ref, acc_ref, *, final_norm):
    f = pl.program_id(1)

    @pl.when(f == 0)
    def _():
        h_ref[...] = _rms(x_ref[...], g_ref[...]).astype(h_ref.dtype)
        acc_ref[...] = x_ref[...]

    acc_ref[...] += _swiglu_step(h_ref[...], wg_ref, wu_ref, wd_ref)

    @pl.when(f == pl.num_programs(1) - 1)
    def _():
        out = acc_ref[...]
        if final_norm:
            out = _rms(out, gf_ref[...])
        o_ref[...] = out


def swiglu_ffn(x, g, w_gate, w_up, w_down, final_gain, *, final_norm, tm=1024, tf=512):
    n, d = x.shape
    ff = w_gate.shape[1]
    tm = _tile(n, tm)
    tf = tf if ff % tf == 0 else ff // 2
    return pl.pallas_call(
        functools.partial(_ffn_kernel, final_norm=final_norm),
        grid=(n // tm, ff // tf),
        in_specs=[pl.BlockSpec((tm, d), lambda i, f: (i, 0)),
                  pl.BlockSpec((1, d), lambda i, f: (0, 0)),
                  pl.BlockSpec((d, tf), lambda i, f: (0, f)),
                  pl.BlockSpec((d, tf), lambda i, f: (0, f)),
                  pl.BlockSpec((tf, d), lambda i, f: (f, 0)),
                  pl.BlockSpec((1, d), lambda i, f: (0, 0))],
        out_specs=pl.BlockSpec((tm, d), lambda i, f: (i, 0)),
        out_shape=jax.ShapeDtypeStruct((n, d), jnp.float32),
        scratch_shapes=[pltpu.VMEM((tm, d), jnp.bfloat16), pltpu.VMEM((tm, d), jnp.float32)],
        compiler_params=_cparams("parallel", "arbitrary"),
        name="swiglu_ffn",
    )(x, g.reshape(1, d), w_gate, w_up, w_down, final_gain.reshape(1, d))


MOE_TILE_ROWS = 1024
ROUTE_ROWS = 512


def _pack_bf16_pair(a, b):
    ua = lax.bitcast_convert_type(a.astype(jnp.bfloat16).astype(jnp.float32), jnp.uint32)
    ub = lax.bitcast_convert_type(b.astype(jnp.bfloat16).astype(jnp.float32), jnp.uint32)
    return ua | (ub >> 16)


def _unpack_bf16_pair(p):
    a = lax.bitcast_convert_type(p & jnp.uint32(0xFFFF0000), jnp.float32)
    b = lax.bitcast_convert_type(p << 16, jnp.float32)
    return a, b


def _lane_pick(lane, values):
    out = jnp.zeros(lane.shape, values[0].dtype)
    for k, v in enumerate(values):
        out = jnp.where(lane == k, v, out)
    return out


def _route_kernel(x_ref, g_ref, w_ref, wts_ref, idx_ref, cnt_ref, carry_ref, *, tm):
    @pl.when(pl.program_id(0) == 0)
    def _():
        carry_ref[...] = jnp.zeros(carry_ref.shape, jnp.float32)

    logits = jnp.dot(_rms(x_ref[...], g_ref[...]), w_ref[...],
                     preferred_element_type=jnp.float32, precision=lax.Precision.HIGHEST)
    lane = lax.broadcasted_iota(jnp.int32, logits.shape, 1)
    logits = jnp.where(lane < N_EXPERTS, logits, NEG_BIG)
    m1 = jnp.max(logits, axis=1, keepdims=True)
    i1 = jnp.min(jnp.where(logits == m1, lane, LANES), axis=1, keepdims=True)
    rest = jnp.where(lane == i1, NEG_BIG, logits)
    m2 = jnp.max(rest, axis=1, keepdims=True)
    i2 = jnp.min(jnp.where(rest == m2, lane, LANES), axis=1, keepdims=True)
    e2 = jnp.exp(m2 - m1)
    w1 = 1.0 / (1.0 + e2)
    hot1, hot2 = lane == i1, lane == i2
    both = jnp.logical_or(hot1, hot2).astype(jnp.float32)
    r = lax.broadcasted_iota(jnp.int32, (tm, tm), 0)
    c = lax.broadcasted_iota(jnp.int32, (tm, tm), 1)
    earlier = (c < r).astype(jnp.bfloat16)
    prior = carry_ref[...] + jnp.dot(earlier, both.astype(jnp.bfloat16), preferred_element_type=jnp.float32)
    rank1 = jnp.sum(jnp.where(hot1, prior, 0.0), axis=1, keepdims=True)
    rank2 = jnp.sum(jnp.where(hot2, prior, 0.0), axis=1, keepdims=True)
    carry_ref[...] += jnp.sum(both, axis=0, keepdims=True)
    cnt_ref[...] = jnp.broadcast_to(carry_ref[...], cnt_ref.shape)
    wts_ref[...] = _lane_pick(lane, [w1, e2 * w1])
    idx_ref[...] = _lane_pick(lane, [i1, i2, rank1.astype(jnp.int32), rank2.astype(jnp.int32)])


def moe_route(x, g, w_router):
    n, d = x.shape
    tm = _tile(n, ROUTE_ROWS)
    w = jnp.pad(w_router, ((0, 0), (0, LANES - N_EXPERTS)))
    row = pl.BlockSpec((tm, LANES), lambda i: (i, 0))
    return pl.pallas_call(
        functools.partial(_route_kernel, tm=tm),
        grid=(n // tm,),
        in_specs=[pl.BlockSpec((tm, d), lambda i: (i, 0)),
                  pl.BlockSpec((1, d), lambda i: (0, 0)),
                  pl.BlockSpec((d, LANES), lambda i: (0, 0))],
        out_specs=[row, row, pl.BlockSpec((8, LANES), lambda i: (0, 0))],
        out_shape=[jax.ShapeDtypeStruct((n, LANES), jnp.float32),
                   jax.ShapeDtypeStruct((n, LANES), jnp.int32),
                   jax.ShapeDtypeStruct((8, LANES), jnp.float32)],
        scratch_shapes=[pltpu.VMEM((1, LANES), jnp.float32)],
        compiler_params=_cparams("arbitrary"),
        name="moe_route",
    )(x, g.reshape(1, d), w)


def _dest_kernel(idx_ref, cnt_ref, dest_ref, te_ref, *, tile_rows):
    tiles = jnp.floor((cnt_ref[...] + (tile_rows - 1)) / tile_rows)
    er = lax.broadcasted_iota(jnp.int32, (LANES, LANES), 0)
    ec = lax.broadcasted_iota(jnp.int32, (LANES, LANES), 1)
    first_tile = jnp.dot(tiles.astype(jnp.bfloat16), (er < ec).astype(jnp.bfloat16),
                         preferred_element_type=jnp.float32)
    first_row = first_tile[0:1, :] * tile_rows
    idx = idx_ref[...]
    lane = lax.broadcasted_iota(jnp.int32, idx.shape, 1)
    idx_f = idx.astype(jnp.float32)
    dests = []
    for slot in range(2):
        expert = jnp.sum(jnp.where(lane == slot, idx, 0), axis=1, keepdims=True)
        start = jnp.sum(jnp.where(lane == expert, first_row, 0.0), axis=1, keepdims=True)
        dests.append((start + _lane_col(idx_f, 2 + slot)).astype(jnp.int32))
    dest_ref[...] = _lane_pick(lane, dests)
    last_tile = first_tile + tiles
    tau = lax.broadcasted_iota(jnp.int32, te_ref.shape, 1).astype(jnp.float32)
    owner = jnp.zeros(te_ref.shape, jnp.float32)
    for e in range(N_EXPERTS):
        owner = owner + (_lane_col(last_tile, e) <= tau).astype(jnp.float32)
    te_ref[...] = owner.astype(jnp.int32)


def moe_dest(idx, cnt, n_tiles):
    n = idx.shape[0]
    tm = _tile(n, ROUTE_ROWS)
    te_lanes = -(-n_tiles // LANES) * LANES
    return pl.pallas_call(
        functools.partial(_dest_kernel, tile_rows=MOE_TILE_ROWS),
        grid=(n // tm,),
        in_specs=[pl.BlockSpec((tm, LANES), lambda i: (i, 0)),
                  pl.BlockSpec((8, LANES), lambda i: (0, 0))],
        out_specs=[pl.BlockSpec((tm, LANES), lambda i: (i, 0)),
                   pl.BlockSpec((8, te_lanes), lambda i: (0, 0))],
        out_shape=[jax.ShapeDtypeStruct((n, LANES), jnp.int32),
                   jax.ShapeDtypeStruct((8, te_lanes), jnp.int32)],
        compiler_params=_cparams("arbitrary"),
        name="moe_dest",
    )(idx, cnt)


def _row_copy(src_ref, src_row, dst_ref, dst_row, sem):
    return pltpu.make_async_copy(src_ref.at[pl.ds(src_row, 1)], dst_ref.at[pl.ds(dst_row, 1)], sem)


def _dispatch_kernel(dest_ref, x_ref, g_ref, zeros_ref, xs_ref, hp_ref, sem, *, tm):
    del zeros_ref
    h = _rms(x_ref[...], g_ref[...])
    half = h.shape[1] // 2
    hp_ref[...] = _pack_bf16_pair(h[:, :half], h[:, half:])

    def issue(r, carry):
        for slot in range(2):
            _row_copy(hp_ref, r, xs_ref, dest_ref[0, 2 * r + slot], sem).start()
        return carry

    def drain(r, carry):
        for slot in range(2):
            _row_copy(hp_ref, 0, xs_ref, 0, sem).wait()
        return carry

    lax.fori_loop(0, tm, issue, 0, unroll=8)
    lax.fori_loop(0, tm, drain, 0, unroll=8)


def moe_dispatch(x, g, dest_pairs, n_rows):
    n, d = x.shape
    tm = dest_pairs.shape[2] // 2
    zeros = jnp.zeros((n_rows, d // 2), jnp.uint32)
    return pl.pallas_call(
        functools.partial(_dispatch_kernel, tm=tm),
        grid=(n // tm,),
        in_specs=[pl.BlockSpec((None, 1, 2 * tm), lambda i: (i, 0, 0), memory_space=pltpu.SMEM),
                  pl.BlockSpec((tm, d), lambda i: (i, 0)),
                  pl.BlockSpec((1, d), lambda i: (0, 0)),
                  pl.BlockSpec(memory_space=pl.ANY)],
        out_specs=pl.BlockSpec(memory_space=pl.ANY),
        out_shape=jax.ShapeDtypeStruct((n_rows, d // 2), jnp.uint32),
        scratch_shapes=[pltpu.VMEM((tm, d // 2), jnp.uint32), pltpu.SemaphoreType.DMA(())],
        input_output_aliases={3: 0},
        compiler_params=_cparams("arbitrary"),
        name="moe_dispatch",
    )(dest_pairs, x, g.reshape(1, d), zeros)


def _grouped_ffn_kernel(te_ref, xs_ref, wg_ref, wu_ref, wd_ref, ys_ref, h_ref, acc_ref):
    i, f = pl.program_id(0), pl.program_id(1)
    half = acc_ref.shape[1] // 2

    @pl.when(f == 0)
    def _():
        a, b = _unpack_bf16_pair(xs_ref[...])
        h_ref[:, :half] = a.astype(h_ref.dtype)
        h_ref[:, half:] = b.astype(h_ref.dtype)
        acc_ref[...] = jnp.zeros(acc_ref.shape, jnp.float32)

    @pl.when(te_ref[i] < N_EXPERTS)
    def _():
        acc_ref[...] += _swiglu_step(h_ref[...], wg_ref, wu_ref, wd_ref)

    @pl.when(f == pl.num_programs(1) - 1)
    def _():
        acc = acc_ref[...]
        ys_ref[...] = _pack_bf16_pair(acc[:, :half], acc[:, half:])


def moe_grouped_ffn(xs, tile_expert, w_gate, w_up, w_down, *, tf=512):
    n_rows, half = xs.shape
    d = 2 * half
    ff = w_gate.shape[2]
    tm = MOE_TILE_ROWS
    expert = lambda i, te: jnp.minimum(te[i], N_EXPERTS - 1)
    return pl.pallas_call(
        _grouped_ffn_kernel,
        grid_spec=pltpu.PrefetchScalarGridSpec(
            num_scalar_prefetch=1,
            grid=(n_rows // tm, ff // tf),
            in_specs=[pl.BlockSpec((tm, half), lambda i, f, te: (i, 0)),
                      pl.BlockSpec((None, d, tf), lambda i, f, te: (expert(i, te), 0, f)),
                      pl.BlockSpec((None, d, tf), lambda i, f, te: (expert(i, te), 0, f)),
                      pl.BlockSpec((None, tf, d), lambda i, f, te: (expert(i, te), f, 0))],
            out_specs=pl.BlockSpec((tm, half), lambda i, f, te: (i, 0)),
            scratch_shapes=[pltpu.VMEM((tm, d), jnp.bfloat16), pltpu.VMEM((tm, d), jnp.float32)]),
        out_shape=jax.ShapeDtypeStruct((n_rows, half), jnp.uint32),
        compiler_params=_cparams("parallel", "arbitrary"),
        name="moe_grouped_ffn",
    )(tile_expert, xs, w_gate, w_up, w_down)


def _combine_kernel(dest_ref, x_ref, wts_ref, gf_ref, ys_ref, o_ref, y1_ref, y2_ref, sem, *, tm, final_norm):
    bufs = (y1_ref, y2_ref)

    def issue(r, carry):
        for slot in range(2):
            _row_copy(ys_ref, dest_ref[0, 2 * r + slot], bufs[slot], r, sem).start()
        return carry

    def drain(r, carry):
        for slot in range(2):
            _row_copy(ys_ref, 0, bufs[slot], 0, sem).wait()
        return carry

    lax.fori_loop(0, tm, issue, 0, unroll=8)
    lax.fori_loop(0, tm, drain, 0, unroll=8)
    wts = wts_ref[...]
    out = x_ref[...]
    for slot in range(2):
        y = jnp.concatenate(_unpack_bf16_pair(bufs[slot][...]), axis=1)
        out = out + _lane_col(wts, slot) * y
    if final_norm:
        out = _rms(out, gf_ref[...])
    o_ref[...] = out


def moe_combine(x, wts, dest_pairs, ys, final_gain, *, final_norm):
    n, d = x.shape
    tm = dest_pairs.shape[2] // 2
    return pl.pallas_call(
        functools.partial(_combine_kernel, tm=tm, final_norm=final_norm),
        grid=(n // tm,),
        in_specs=[pl.BlockSpec((None, 1, 2 * tm), lambda i: (i, 0, 0), memory_space=pltpu.SMEM),
                  pl.BlockSpec((tm, d), lambda i: (i, 0)),
                  pl.BlockSpec((tm, LANES), lambda i: (i, 0)),
                  pl.BlockSpec((1, d), lambda i: (0, 0)),
                  pl.BlockSpec(memory_space=pl.ANY)],
        out_specs=pl.BlockSpec((tm, d), lambda i: (i, 0)),
        out_shape=jax.ShapeDtypeStruct((n, d), jnp.float32),
        scratch_shapes=[pltpu.VMEM((tm, d // 2), jnp.uint32), pltpu.VMEM((tm, d // 2), jnp.uint32),
                        pltpu.SemaphoreType.DMA(())],
        compiler_params=_cparams("arbitrary"),
        name="moe_combine",
    )(dest_pairs, x, wts, final_gain.reshape(1, d), ys)


def moe_swiglu(x, g, w_router, w_gate, w_up, w_down, final_gain, *, final_norm):
    n = x.shape[0]
    n_tiles = -(-2 * n // MOE_TILE_ROWS) + N_EXPERTS
    wts, idx, cnt = moe_route(x, g, w_router)
    dest, tile_expert = moe_dest(idx, cnt, n_tiles)
    tm = _tile(n, ROUTE_ROWS)
    dest_pairs = dest[:, :2].reshape(n // tm, 1, 2 * tm)
    xs = moe_dispatch(x, g, dest_pairs, n_tiles * MOE_TILE_ROWS)
    ys = moe_grouped_ffn(xs, tile_expert[0, :n_tiles], w_gate, w_up, w_down)
    return moe_combine(x, wts, dest_pairs, ys, final_gain, final_norm=final_norm)


def _rope_tables(seq):
    half = DA_DIM // 2
    inv_freq = 1.0 / (ROPE_THETA ** (jnp.arange(0, DA_DIM, 2, dtype=jnp.float32) / DA_DIM))
    ang = jnp.arange(seq, dtype=jnp.float32)[:, None] * inv_freq[None, :]
    cos, sin = jnp.cos(ang), jnp.sin(ang)
    reps = LANES // half
    sign = jnp.tile(jnp.concatenate([-jnp.ones((half,)), jnp.ones((half,))]), reps // 2)
    return jnp.tile(cos, (1, reps)), jnp.tile(sin, (1, reps)) * sign[None, :]


def kernel(x, mem, norm_mix, w_in, gd_conv, lam_q1, lam_k1, lam_q2, lam_k2, da_subln, gd_a_log,
           gd_dt_bias, gd_norm, w_br_a, w_br_b, w_br_c, w_out, norm_x, norm_mem, w_q_x, w_kv_x,
           w_o_x, norm_ffn, w_gate_dense, w_up_dense, w_down_dense, w_router, w_gate_exp,
           w_up_exp, w_down_exp, norm_final):
    batch, seq, d = x.shape
    n_mem = mem.shape[1]
    depth = norm_mix.shape[0]
    bf = jnp.bfloat16
    n = batch * seq
    cos, sin = _rope_tables(seq)
    xf = x.reshape(n, d)
    memf = mem.reshape(batch * n_mem, d)
    ab0 = MIX_W
    ab1 = MIX_W + 2 * GD_HEADS

    for l in range(depth):
        lam_init = 0.8 - 0.6 * math.exp(-0.3 * l)
        w_l = w_in[l]
        w_main = jnp.concatenate([w_l[:, :ab0], w_l[:, ab1:]], axis=1).astype(bf)
        w_ab = jnp.pad(w_l[:, ab0:ab1], ((0, 0), (0, LANES - 2 * GD_HEADS))).astype(bf)
        proj = rms_matmul(xf, norm_mix[l], w_main, bf)
        ab = rms_matmul(xf, norm_mix[l], w_ab, jnp.float32)

        q_rope, k_rope = rope_qk(proj, cos, sin, seq)
        lam_params = jnp.stack([lam_q1[l], lam_k1[l], lam_q2[l], lam_k2[l]])
        o_a = diff_attention(q_rope, k_rope, proj, lam_params, da_subln[l], batch, seq, lam_init)
        o_b = stick_breaking_attention(proj, batch, seq)
        qkv = gd_conv_silu(proj, gd_conv[l], batch, seq)
        gate = gd_gates(ab, gd_a_log[l], gd_dt_bias[l])
        u, w, qg, kd, amat = gd_local(qkv, gate, batch, seq)
        o_c = gd_scan(u, w, qg, kd, amat, gate, proj, gd_norm[l], batch, seq)
        xf = merge_branches(xf, o_a, o_b, o_c, proj, w_br_a[l].astype(bf), w_br_b[l].astype(bf),
                            w_br_c[l].astype(bf), w_out[l].astype(bf))

        kv = rms_matmul(memf, norm_mem[l], w_kv_x[l].astype(bf), bf)
        xf = cross_attention(xf, norm_x[l], w_q_x[l].astype(bf), kv, w_o_x[l].astype(bf),
                             batch, seq, n_mem)

        last = l == depth - 1
        i = l // 2
        if l % 2 == 0:
            xf = swiglu_ffn(xf, norm_ffn[l], w_gate_dense[i].astype(bf), w_up_dense[i].astype(bf),
                            w_down_dense[i].astype(bf), norm_final, final_norm=last)
        else:
            xf = moe_swiglu(xf, norm_ffn[l], w_router[i], w_gate_exp[i].astype(bf), w_up_exp[i].astype(bf),
                            w_down_exp[i].astype(bf), norm_final, final_norm=last)
    return xf.reshape(batch, seq, d)
```

```python
import functools
import math

import jax
import jax.numpy as jnp
from jax import lax
from jax.experimental import pallas as pl
from jax.experimental.pallas import tpu as pltpu

EPS = 1e-6
LOG2E = math.log2(math.e)
ROPE_THETA = 10000.0
LANES = 128

DA_HEADS = 4
DA_DIM = 64
SB_HEADS = 4
SB_DIM = 128
GD_HEADS = 4
GD_DIM = 128
GD_CONV = 4
GD_CHUNK = 64
X_HEADS = 4
X_DIM = 128
N_EXPERTS = 8
HEAD_W = 512

COL_DA_Q, COL_DA_K, COL_DA_V = 0, 4, 8
COL_SB_Q, COL_SB_K, COL_SB_V = 12, 16, 20
COL_GD_Q, COL_GD_Z = 24, 36
MIX_W = 5120

SB_LOG_FLOOR = -110.0
NEG_BIG = -1e30
VMEM_LIMIT = 48 * 1024 * 1024

_NT = (((1,), (1,)), ((), ()))
_TN = (((0,), (0,)), ((), ()))


def _cparams(*sem):
    return pltpu.CompilerParams(dimension_semantics=sem, vmem_limit_bytes=VMEM_LIMIT)


def _tile(n, pref):
    t = min(n, pref)
    while n % t:
        t //= 2
    return t


def _bdot(a, b, dims=None):
    a = a.astype(jnp.bfloat16)
    b = b.astype(jnp.bfloat16)
    if dims is None:
        return jnp.dot(a, b, preferred_element_type=jnp.float32)
    return lax.dot_general(a, b, dims, preferred_element_type=jnp.float32)


def _dot_split(a, b_exact):
    hi = a.astype(jnp.bfloat16)
    lo = (a - hi.astype(jnp.float32)).astype(jnp.bfloat16)
    return (jnp.dot(hi, b_exact, preferred_element_type=jnp.float32)
            + jnp.dot(lo, b_exact, preferred_element_type=jnp.float32))


def _rms(x, g):
    return x * lax.rsqrt(jnp.mean(x * x, axis=-1, keepdims=True) + EPS) * g


def _softplus(z):
    return jnp.maximum(z, 0.0) + jnp.log1p(jnp.exp(-jnp.abs(z)))


def _silu(x):
    return x * jax.nn.sigmoid(x)


def _lane_tile(x, reps):
    return jnp.concatenate([x] * reps, axis=1)


def _lane_col(x, lane):
    ids = lax.broadcasted_iota(jnp.int32, x.shape, 1)
    return jnp.sum(jnp.where(ids == lane, x, 0.0), axis=1, keepdims=True)


def _rms_matmul_kernel(x_ref, g_ref, w_ref, o_ref, h_ref):
    first = pl.program_id(1) == 0

    @pl.when(first)
    def _():
        h = _rms(x_ref[...], g_ref[...]).astype(h_ref.dtype)
        h_ref[...] = h
        o_ref[...] = jnp.dot(h, w_ref[...], preferred_element_type=jnp.float32).astype(o_ref.dtype)

    @pl.when(jnp.logical_not(first))
    def _():
        o_ref[...] = jnp.dot(h_ref[...], w_ref[...], preferred_element_type=jnp.float32).astype(o_ref.dtype)


def rms_matmul(x, g, w, out_dtype, *, tm=1024, tn=1024):
    n, d = x.shape
    nout = w.shape[1]
    tm, tn = _tile(n, tm), _tile(nout, tn)
    return pl.pallas_call(
        _rms_matmul_kernel,
        grid=(n // tm, nout // tn),
        in_specs=[pl.BlockSpec((tm, d), lambda i, j: (i, 0)),
                  pl.BlockSpec((1, d), lambda i, j: (0, 0)),
                  pl.BlockSpec((d, tn), lambda i, j: (0, j))],
        out_specs=pl.BlockSpec((tm, tn), lambda i, j: (i, j)),
        out_shape=jax.ShapeDtypeStruct((n, nout), out_dtype),
        scratch_shapes=[pltpu.VMEM((tm, d), w.dtype)],
        compiler_params=_cparams("parallel", "arbitrary"),
        name="rms_matmul",
    )(x, g.reshape(1, d), w)


def _rope_kernel(q_ref, k_ref, cos_ref, sin_ref, qo_ref, ko_ref):
    cos, sin = cos_ref[...], sin_ref[...]
    lane = lax.broadcasted_iota(jnp.int32, cos.shape, 1)
    first_half = (lane % DA_DIM) < (DA_DIM // 2)

    def rot(t):
        partner = jnp.where(first_half, pltpu.roll(t, LANES - DA_DIM // 2, 1),
                            pltpu.roll(t, DA_DIM // 2, 1))
        return t * cos + partner * sin

    for h in range(DA_HEADS):
        sl = slice(h * LANES, (h + 1) * LANES)
        qo_ref[:, sl] = (rot(q_ref[:, sl].astype(jnp.float32)) * (DA_DIM ** -0.5 * LOG2E)).astype(qo_ref.dtype)
        ko_ref[:, sl] = rot(k_ref[:, sl].astype(jnp.float32)).astype(ko_ref.dtype)


def rope_qk(proj, cos, sin, seq, *, ts=1024):
    n = proj.shape[0]
    ts = _tile(seq, ts)
    per_seq = seq // ts
    out = jax.ShapeDtypeStruct((n, HEAD_W), jnp.bfloat16)
    return pl.pallas_call(
        _rope_kernel,
        grid=(n // ts,),
        in_specs=[pl.BlockSpec((ts, HEAD_W), lambda i: (i, COL_DA_Q // 4)),
                  pl.BlockSpec((ts, HEAD_W), lambda i: (i, COL_DA_K // 4)),
                  pl.BlockSpec((ts, LANES), lambda i: (i % per_seq, 0)),
                  pl.BlockSpec((ts, LANES), lambda i: (i % per_seq, 0))],
        out_specs=[pl.BlockSpec((ts, HEAD_W), lambda i: (i, 0)),
                   pl.BlockSpec((ts, HEAD_W), lambda i: (i, 0))],
        out_shape=[out, out],
        compiler_params=_cparams("parallel"),
        name="rope_qk",
    )(proj, proj, cos, sin)


def _diff_attn_kernel(lam_ref, q_ref, k_ref, v_ref, subln_ref, o_ref,
                      qs_ref, sa_ref, sb_ref, ma_ref, mb_ref, m_ref, acc_ref, *, t, lam_init):
    i = pl.program_id(2)
    q = q_ref[...]
    lane = lax.broadcasted_iota(jnp.int32, q.shape, 1)
    zero = jnp.zeros_like(q)
    qs_ref[:t, :] = jnp.where(lane < DA_DIM, q, zero)
    qs_ref[t:, :] = jnp.where(lane >= DA_DIM, q, zero)
    m_ref[...] = jnp.full(m_ref.shape, NEG_BIG, jnp.float32)
    acc_ref[...] = jnp.zeros(acc_ref.shape, jnp.float32)
    reps = t // LANES
    ones = jnp.ones((t, LANES), jnp.bfloat16)

    def scores(j, s_ref, mx_ref, masked=False):
        k = k_ref[pl.ds(pl.multiple_of(jnp.maximum(j, 0) * t, t), t), :]
        s = lax.dot_general(qs_ref[...], k, _NT, preferred_element_type=jnp.float32)
        if masked:
            row = lax.broadcasted_iota(jnp.int32, s.shape, 0) % t
            col = lax.broadcasted_iota(jnp.int32, s.shape, 1)
            s = jnp.where(col <= row, s, NEG_BIG)
        s_ref[...] = s
        mx_ref[...] = jnp.broadcast_to(jnp.max(s, axis=1, keepdims=True), mx_ref.shape)

    def consume(j, s_ref, mx_ref):
        m_old = m_ref[...]
        m_new = jnp.maximum(m_old, mx_ref[...])
        alpha = jnp.exp2(m_old - m_new)
        p = jnp.exp2(s_ref[...] - _lane_tile(m_new, reps))
        v1 = jnp.concatenate([v_ref[pl.ds(pl.multiple_of(j * t, t), t), :], ones], axis=1)
        acc_ref[...] = _lane_tile(alpha, 2) * acc_ref[...] + jnp.dot(
            p.astype(jnp.bfloat16), v1, preferred_element_type=jnp.float32)
        m_ref[...] = m_new

    scores(i, sa_ref, ma_ref, masked=True)
    scores(i - 1, sb_ref, mb_ref)
    consume(i, sa_ref, ma_ref)

    def earlier_pair(n, carry):
        j = i - 1 - 2 * n
        scores(j - 1, sa_ref, ma_ref)
        consume(j, sb_ref, mb_ref)
        scores(j - 2, sb_ref, mb_ref)
        consume(j - 1, sa_ref, ma_ref)
        return carry

    lax.fori_loop(0, i // 2, earlier_pair, 0)

    @pl.when(i % 2 == 1)
    def _():
        consume(0, sb_ref, mb_ref)

    lq1, lk1, lq2, lk2 = (lam_ref[r:r + 1, :] for r in range(4))
    lam = (jnp.exp(jnp.sum(lq1 * lk1, axis=1, keepdims=True))
           - jnp.exp(jnp.sum(lq2 * lk2, axis=1, keepdims=True)) + lam_init)
    acc = acc_ref[:, :LANES] / acc_ref[:, LANES:]
    o = acc[:t] - lam * acc[t:]
    o_ref[...] = (_rms(o, subln_ref[...]) * (1.0 - lam_init)).astype(o_ref.dtype)


def diff_attention(q_rope, k_rope, proj, lam_params, subln, batch, seq, lam_init, *, t=512):
    n = q_rope.shape[0]
    t = _tile(seq, t)
    nq = seq // t
    return pl.pallas_call(
        functools.partial(_diff_attn_kernel, t=t, lam_init=lam_init),
        grid=(batch, DA_HEADS, nq),
        in_specs=[pl.BlockSpec((4, DA_DIM), lambda b, h, i: (0, 0)),
                  pl.BlockSpec((t, LANES), lambda b, h, i: (b * nq + i, h)),
                  pl.BlockSpec((seq, LANES), lambda b, h, i: (b, h)),
                  pl.BlockSpec((seq, LANES), lambda b, h, i: (b, COL_DA_V + h)),
                  pl.BlockSpec((1, LANES), lambda b, h, i: (0, 0))],
        out_specs=pl.BlockSpec((t, LANES), lambda b, h, i: (b * nq + i, h)),
        out_shape=jax.ShapeDtypeStruct((n, HEAD_W), jnp.bfloat16),
        scratch_shapes=[pltpu.VMEM((2 * t, LANES), jnp.bfloat16),
                        pltpu.VMEM((2 * t, t), jnp.float32),
                        pltpu.VMEM((2 * t, t), jnp.float32),
                        pltpu.VMEM((2 * t, LANES), jnp.float32),
                        pltpu.VMEM((2 * t, LANES), jnp.float32),
                        pltpu.VMEM((2 * t, LANES), jnp.float32),
                        pltpu.VMEM((2 * t, 2 * LANES), jnp.float32)],
        compiler_params=_cparams("parallel", "parallel", "arbitrary"),
        name="diff_attention",
    )(lam_params, q_rope, k_rope, proj, subln.reshape(1, LANES))


def _sb_attn_kernel(q_ref, k_ref, v_ref, o_ref, za_ref, zb_ref, carry_ref, acc_ref, *, tq):
    i = pl.program_id(2)
    qs_val = (q_ref[...].astype(jnp.float32) * SB_DIM ** -0.5).astype(jnp.bfloat16)
    carry_ref[...] = jnp.zeros(carry_ref.shape, jnp.float32)
    acc_ref[...] = jnp.zeros(acc_ref.shape, jnp.float32)
    r = lax.broadcasted_iota(jnp.int32, (tq, tq), 0)
    c = lax.broadcasted_iota(jnp.int32, (tq, tq), 1)
    later = (r > c).astype(jnp.bfloat16)
    reps = tq // LANES

    def tile_rows(j):
        return pl.ds(pl.multiple_of(jnp.maximum(j, 0) * tq, tq), tq)

    def logits(j, z_ref):
        z_ref[...] = lax.dot_general(qs_val, k_ref[tile_rows(j), :], _NT, preferred_element_type=jnp.float32)

    def consume(j, z_ref, masked):
        z = z_ref[...]
        sp = _softplus(z)
        log_fail = jnp.where(c < r, -sp, 0.0) if masked else -sp
        log_remain = _lane_tile(carry_ref[...], reps) + _dot_split(log_fail, later)
        attn = jnp.exp(z - sp + log_remain)
        if masked:
            attn = jnp.where(c < r, attn, 0.0)
        acc_ref[...] += jnp.dot(attn.astype(jnp.bfloat16), v_ref[tile_rows(j), :],
                                preferred_element_type=jnp.float32)
        carry_ref[...] += jnp.sum(log_fail, axis=1, keepdims=True)

    def consume_if_exists(j, z_ref):
        carry_ref[...] = jnp.where(j < 0, NEG_BIG, carry_ref[...])
        consume(j, z_ref, False)

    logits(i, za_ref)
    logits(i - 1, zb_ref)
    consume(i, za_ref, True)
    logits(i - 2, za_ref)
    consume_if_exists(i - 1, zb_ref)

    def cond(state):
        j, alive = state
        return jnp.logical_and(j >= 0, alive)

    def body(state):
        j, _ = state
        logits(j - 1, zb_ref)
        consume(j, za_ref, False)
        logits(j - 2, za_ref)
        consume_if_exists(j - 1, zb_ref)
        return j - 2, jnp.max(carry_ref[...]) > SB_LOG_FLOOR

    lax.while_loop(cond, body, (i - 2, jnp.max(carry_ref[...]) > SB_LOG_FLOOR))
    o_ref[...] = acc_ref[...].astype(o_ref.dtype)


def stick_breaking_attention(proj, batch, seq, *, tq=256):
    n = proj.shape[0]
    tq = _tile(seq, tq)
    nq = seq // tq
    return pl.pallas_call(
        functools.partial(_sb_attn_kernel, tq=tq),
        grid=(batch, SB_HEADS, nq),
        in_specs=[pl.BlockSpec((tq, LANES), lambda b, h, i: (b * nq + i, COL_SB_Q + h)),
                  pl.BlockSpec((seq, LANES), lambda b, h, i: (b, COL_SB_K + h)),
                  pl.BlockSpec((seq, LANES), lambda b, h, i: (b, COL_SB_V + h))],
        out_specs=pl.BlockSpec((tq, LANES), lambda b, h, i: (b * nq + i, h)),
        out_shape=jax.ShapeDtypeStruct((n, HEAD_W), jnp.bfloat16),
        scratch_shapes=[pltpu.VMEM((tq, tq), jnp.float32),
                        pltpu.VMEM((tq, tq), jnp.float32),
                        pltpu.VMEM((tq, LANES), jnp.float32),
                        pltpu.VMEM((tq, LANES), jnp.float32)],
        compiler_params=_cparams("parallel", "parallel", "arbitrary"),
        name="stick_breaking_attention",
    )(proj, proj, proj)


def _gd_conv_kernel(x_ref, w_ref, o_ref):
    c = pl.program_id(1)
    w = w_ref[...]
    is_q = c < GD_HEADS
    is_v = c >= 2 * GD_HEADS

    def conv_norm(x, causal_mask):
        y = x * w[GD_CONV - 1:GD_CONV, :]
        for back in range(1, GD_CONV):
            shifted = pltpu.roll(x, back, 0)
            if causal_mask:
                row = lax.broadcasted_iota(jnp.int32, x.shape, 0)
                shifted = jnp.where(row >= back, shifted, 0.0)
            y = y + shifted * w[GD_CONV - 1 - back:GD_CONV - back, :]
        y = _silu(y)
        inv = lax.rsqrt(jnp.sum(y * y, axis=1, keepdims=True) + EPS)
        scale = jnp.where(is_v, 1.0, inv * jnp.where(is_q, GD_DIM ** -0.5, 1.0))
        return (y * scale).astype(o_ref.dtype)

    o_ref[...] = conv_norm(x_ref[...].astype(jnp.float32), False)
    o_ref[:16, :] = conv_norm(x_ref[:16, :].astype(jnp.float32), True)


def gd_conv_silu(proj, conv_w, batch, seq):
    n = proj.shape[0]
    ncol = 3 * GD_HEADS
    return pl.pallas_call(
        _gd_conv_kernel,
        grid=(batch, ncol),
        in_specs=[pl.BlockSpec((seq, LANES), lambda b, c: (b, COL_GD_Q + c)),
                  pl.BlockSpec((GD_CONV, LANES), lambda b, c: (0, c))],
        out_specs=pl.BlockSpec((seq, LANES), lambda b, c: (b, c)),
        out_shape=jax.ShapeDtypeStruct((n, ncol * LANES), jnp.float32),
        compiler_params=_cparams("parallel", "parallel"),
        name="gd_conv",
    )(proj, conv_w)


def _gd_gate_kernel(ab_ref, alog_ref, dtb_ref, o_ref, *, rows):
    ab = ab_ref[...]
    is_decay_lane = lax.broadcasted_iota(jnp.int32, (GD_CHUNK, LANES), 1) < GD_HEADS
    g = -jnp.exp(alog_ref[...]) * _softplus(ab + dtb_ref[...])
    beta = jax.nn.sigmoid(ab)
    r = lax.broadcasted_iota(jnp.int32, (GD_CHUNK, GD_CHUNK), 0)
    c = lax.broadcasted_iota(jnp.int32, (GD_CHUNK, GD_CHUNK), 1)
    tril = (c <= r).astype(jnp.bfloat16)
    for ci in range(rows // GD_CHUNK):
        sl = slice(ci * GD_CHUNK, (ci + 1) * GD_CHUNK)
        hi = g[sl].astype(jnp.bfloat16)
        mid_f = g[sl] - hi.astype(jnp.float32)
        mid = mid_f.astype(jnp.bfloat16)
        lo = (mid_f - mid.astype(jnp.float32)).astype(jnp.bfloat16)
        gc = (jnp.dot(tril, hi, preferred_element_type=jnp.float32)
              + jnp.dot(tril, mid, preferred_element_type=jnp.float32)
              + jnp.dot(tril, lo, preferred_element_type=jnp.float32))
        o_ref[sl, :] = jnp.where(is_decay_lane, gc, beta[sl])


def gd_gates(ab, a_log, dt_bias, *, rows=512):
    n = ab.shape[0]
    rows = _tile(n, rows)
    pad = jnp.zeros((LANES - GD_HEADS,), jnp.float32)
    alog = jnp.concatenate([a_log, pad]).reshape(1, LANES)
    dtb = jnp.concatenate([dt_bias, pad]).reshape(1, LANES)
    return pl.pallas_call(
        functools.partial(_gd_gate_kernel, rows=rows),
        grid=(n // rows,),
        in_specs=[pl.BlockSpec((rows, LANES), lambda i: (i, 0)),
                  pl.BlockSpec((1, LANES), lambda i: (0, 0)),
                  pl.BlockSpec((1, LANES), lambda i: (0, 0))],
        out_specs=pl.BlockSpec((rows, LANES), lambda i: (i, 0)),
        out_shape=jax.ShapeDtypeStruct((n, LANES), jnp.float32),
        compiler_params=_cparams("parallel"),
        name="gd_gates",
    )(ab, alog, dtb)


def _unit_lower_inverse(lmats, r, c):
    eye = (r == c).astype(jnp.float32)
    diag16 = (r // 16) == (c // 16)
    ps = [jnp.where(diag16, lm, 0.0) for lm in lmats]
    ts = [eye - p for p in ps]
    for _ in range(3):
        ps = [_bdot(p, p) for p in ps]
        ts = [t + _bdot(t, p) for t, p in zip(ts, ps)]
    for blk in (32, 64):
        band = jnp.logical_and((r // blk) == (c // blk), (r // (blk // 2)) != (c // (blk // 2)))
        tcs = [_bdot(t, jnp.where(band, lm, 0.0)) for t, lm in zip(ts, lmats)]
        ts = [t - _bdot(tc, t) for t, tc in zip(ts, tcs)]
    return ts


def _gd_local_kernel(q_ref, k_ref, v_ref, gate_ref, u_ref, w_ref, qg_ref, kd_ref, a_ref, *, rows):
    h = pl.program_id(1)
    gate = gate_ref[...]
    gc_all = _lane_col(gate, h)
    beta_all = _lane_col(gate, GD_HEADS + h)
    r = lax.broadcasted_iota(jnp.int32, (GD_CHUNK, GD_CHUNK), 0)
    c = lax.broadcasted_iota(jnp.int32, (GD_CHUNK, GD_CHUNK), 1)
    sls = [slice(ci * GD_CHUNK, (ci + 1) * GD_CHUNK) for ci in range(rows // GD_CHUNK)]
    qs = [q_ref[sl, :] for sl in sls]
    ks = [k_ref[sl, :] for sl in sls]
    gcs = [gc_all[sl] for sl in sls]
    kbs = [k * beta_all[sl] for k, sl in zip(ks, sls)]
    kks = [_bdot(kb, k, _NT) for kb, k in zip(kbs, ks)]
    qks = [_bdot(q, k, _NT) for q, k in zip(qs, ks)]
    decays = []
    for gc in gcs:
        gc_rows = jnp.broadcast_to(gc, (GD_CHUNK, LANES))
        gc_cols = jnp.transpose(gc_rows)[0:1, :GD_CHUNK]
        decays.append(jnp.exp(jnp.where(c <= r, gc_rows[:, :GD_CHUNK] - gc_cols, 0.0)))
    tmats = _unit_lower_inverse([jnp.where(c < r, kk * d, 0.0) for kk, d in zip(kks, decays)], r, c)
    egcs = [jnp.exp(gc) for gc in gcs]
    uws = [_bdot(t, jnp.concatenate([v_ref[sl, :] * beta_all[sl], kb * egc], axis=1))
           for t, sl, kb, egc in zip(tmats, sls, kbs, egcs)]
    for sl, q, k, gc, egc, qk, d, uw in zip(sls, qs, ks, gcs, egcs, qks, decays, uws):
        u_ref[sl, :] = uw[:, :LANES]
        w_ref[sl, :] = uw[:, LANES:].astype(w_ref.dtype)
        intra = jnp.where(c <= r, qk * d, 0.0)
        a_ref[sl, :] = jnp.concatenate([intra, jnp.zeros_like(intra)], axis=1).astype(a_ref.dtype)
        qg_ref[sl, :] = (q * egc).astype(qg_ref.dtype)
        g_last = gc[GD_CHUNK - 1:GD_CHUNK, :]
        kd_ref[sl, :] = (k * jnp.exp(g_last - gc)).astype(kd_ref.dtype)


def gd_local(qkv, gate, batch, seq, *, rows=1024):
    n = qkv.shape[0]
    rows = _tile(seq, rows)
    nr = seq // rows
    spec = lambda off: pl.BlockSpec((rows, LANES), lambda b, h, i: (b * nr + i, off + h))
    out_spec = pl.BlockSpec((rows, LANES), lambda b, h, i: (b * nr + i, h))
    bf = jax.ShapeDtypeStruct((n, HEAD_W), jnp.bfloat16)
    return pl.pallas_call(
        functools.partial(_gd_local_kernel, rows=rows),
        grid=(batch, GD_HEADS, nr),
        in_specs=[spec(0), spec(GD_HEADS), spec(2 * GD_HEADS),
                  pl.BlockSpec((rows, LANES), lambda b, h, i: (b * nr + i, 0))],
        out_specs=[out_spec] * 5,
        out_shape=[jax.ShapeDtypeStruct((n, HEAD_W), jnp.float32), bf, bf, bf, bf],
        compiler_params=_cparams("parallel", "parallel", "parallel"),
        name="gd_local",
    )(qkv, qkv, qkv, gate)


def _gd_scan_kernel(u_ref, w_ref, qg_ref, kd_ref, a_ref, gate_ref, z_ref, g_ref, o_ref, state_ref, *, rows, nb):
    @pl.when(pl.program_id(1) == 0)
    def _():
        state_ref[...] = jnp.zeros(state_ref.shape, jnp.float32)

    gain = g_ref[...]
    chains = [(b, h, slice(h * LANES, (h + 1) * LANES)) for b in range(nb) for h in range(GD_HEADS)]

    def chunk(ci, carry):
        start = pl.multiple_of(ci * GD_CHUNK, GD_CHUNK)
        sl = pl.ds(start, GD_CHUNK)
        states = [state_ref[b * GD_HEADS + h] for b, h, _ in chains]
        sbs = [s.astype(jnp.bfloat16) for s in states]
        wq = [jnp.dot(jnp.concatenate([w_ref[b, sl, cs], qg_ref[b, sl, cs]], axis=0), sb,
                      preferred_element_type=jnp.float32) for (b, _, cs), sb in zip(chains, sbs)]
        vbs = [(u_ref[b, sl, cs] - x[:GD_CHUNK]).astype(jnp.bfloat16) for (b, _, cs), x in zip(chains, wq)]
        decay = [jnp.exp(gate_ref[b, pl.ds(start + GD_CHUNK - 1, 1), :]) for b in range(nb)]
        for (b, h, cs), state, vb in zip(chains, states, vbs):
            state_ref[b * GD_HEADS + h] = state * _lane_col(decay[b], h) + lax.dot_general(
                kd_ref[b, sl, cs], vb, _TN, preferred_element_type=jnp.float32)
        outs = [x[GD_CHUNK:] + jnp.dot(a_ref[b, sl, cs][:, :GD_CHUNK], vb, preferred_element_type=jnp.float32)
                for (b, _, cs), x, vb in zip(chains, wq, vbs)]
        for (b, _, cs), out in zip(chains, outs):
            o_ref[b, sl, cs] = (_rms(out, gain) * _silu(z_ref[b, sl, cs].astype(jnp.float32))).astype(o_ref.dtype)
        return carry

    lax.fori_loop(0, rows // GD_CHUNK, chunk, 0)


def gd_scan(u, w, qg, kd, amat, gate, proj, norm_g, batch, seq, *, rows=1024):
    rows = _tile(seq, rows)
    nb = 2 if batch % 2 == 0 else 1
    per_seq = lambda t: t.reshape(batch, seq, t.shape[-1])
    spec = pl.BlockSpec((nb, rows, HEAD_W), lambda b, i: (b, i, 0))
    out = pl.pallas_call(
        functools.partial(_gd_scan_kernel, rows=rows, nb=nb),
        grid=(batch // nb, seq // rows),
        in_specs=[spec, spec, spec, spec, spec,
                  pl.BlockSpec((nb, rows, LANES), lambda b, i: (b, i, 0)),
                  pl.BlockSpec((nb, rows, HEAD_W), lambda b, i: (b, i, COL_GD_Z // GD_HEADS)),
                  pl.BlockSpec((1, LANES), lambda b, i: (0, 0))],
        out_specs=spec,
        out_shape=jax.ShapeDtypeStruct((batch, seq, HEAD_W), jnp.bfloat16),
        scratch_shapes=[pltpu.VMEM((nb * GD_HEADS, GD_DIM, GD_DIM), jnp.float32)],
        compiler_params=_cparams("parallel", "arbitrary"),
        name="gd_scan",
    )(per_seq(u), per_seq(w), per_seq(qg), per_seq(kd), per_seq(amat), per_seq(gate), per_seq(proj),
      norm_g.reshape(1, LANES))
    return out.reshape(batch * seq, HEAD_W)


def _merge_kernel(x_ref, oa_ref, ob_ref, oc_ref, ga_ref, gb_ref, gc_ref,
                  wa_ref, wb_ref, wc_ref, wo_ref, o_ref):
    def branch(o, g, w):
        return jax.nn.sigmoid(g[...].astype(jnp.float32)) * jnp.dot(
            o[...], w[...], preferred_element_type=jnp.float32)

    merged = (branch(oa_ref, ga_ref, wa_ref) + branch(ob_ref, gb_ref, wb_ref)
              + branch(oc_ref, gc_ref, wc_ref))
    o_ref[...] = x_ref[...] + jnp.dot(merged.astype(jnp.bfloat16), wo_ref[...],
                                      preferred_element_type=jnp.float32)


def merge_branches(x, o_a, o_b, o_c, proj, w_a, w_b, w_c, w_out, *, tm=512):
    n, d = x.shape
    tm = _tile(n, tm)
    gate0 = MIX_W // d
    row = lambda width: pl.BlockSpec((tm, width), lambda i: (i, 0))
    full = lambda w: pl.BlockSpec(w.shape, lambda i: (0, 0))
    return pl.pallas_call(
        _merge_kernel,
        grid=(n // tm,),
        in_specs=[row(d), row(HEAD_W), row(HEAD_W), row(HEAD_W),
                  pl.BlockSpec((tm, d), lambda i: (i, gate0)),
                  pl.BlockSpec((tm, d), lambda i: (i, gate0 + 1)),
                  pl.BlockSpec((tm, d), lambda i: (i, gate0 + 2)),
                  full(w_a), full(w_b), full(w_c), full(w_out)],
        out_specs=row(d),
        out_shape=jax.ShapeDtypeStruct((n, d), jnp.float32),
        compiler_params=_cparams("parallel"),
        name="merge_branches",
    )(x, o_a, o_b, o_c, proj, proj, proj, w_a, w_b, w_c, w_out)


def _cross_attn_kernel(x_ref, g_ref, wq_ref, kv_ref, wo_ref, o_ref):
    x = x_ref[...]
    q = jnp.dot(_rms(x, g_ref[...]).astype(jnp.bfloat16), wq_ref[...],
                preferred_element_type=jnp.float32) * X_DIM ** -0.5
    heads = []
    width = X_HEADS * X_DIM
    for h in range(X_HEADS):
        sl = slice(h * X_DIM, (h + 1) * X_DIM)
        k = kv_ref[:, sl]
        v = kv_ref[:, width + h * X_DIM: width + (h + 1) * X_DIM]
        s = _bdot(q[:, sl], k, _NT)
        p = jnp.exp(s - jnp.max(s, axis=1, keepdims=True))
        p = p / jnp.sum(p, axis=1, keepdims=True)
        heads.append(jnp.dot(p.astype(jnp.bfloat16), v, preferred_element_type=jnp.float32))
    o = jnp.concatenate(heads, axis=1).astype(jnp.bfloat16)
    o_ref[...] = x + jnp.dot(o, wo_ref[...], preferred_element_type=jnp.float32)


def cross_attention(x, g, w_q, kv, w_o, batch, seq, n_mem, *, tm=512):
    n, d = x.shape
    tm = _tile(seq, tm)
    per_seq = seq // tm
    full = lambda w: pl.BlockSpec(w.shape, lambda b, i: (0, 0))
    return pl.pallas_call(
        _cross_attn_kernel,
        grid=(batch, per_seq),
        in_specs=[pl.BlockSpec((tm, d), lambda b, i: (b * per_seq + i, 0)),
                  pl.BlockSpec((1, d), lambda b, i: (0, 0)),
                  full(w_q),
                  pl.BlockSpec((n_mem, kv.shape[1]), lambda b, i: (b, 0)),
                  full(w_o)],
        out_specs=pl.BlockSpec((tm, d), lambda b, i: (b * per_seq + i, 0)),
        out_shape=jax.ShapeDtypeStruct((n, d), jnp.float32),
        compiler_params=_cparams("parallel", "parallel"),
        name="cross_attention",
    )(x, g.reshape(1, d), w_q, kv, w_o)


def _swiglu_step(h, wg_ref, wu_ref, wd_ref):
    act = (_silu(jnp.dot(h, wg_ref[...], preferred_element_type=jnp.float32))
           * jnp.dot(h, wu_ref[...], preferred_element_type=jnp.float32))
    return jnp.dot(act.astype(jnp.bfloat16), wd_ref[...], preferred_element_type=jnp.float32)


def _ffn_kernel(x_ref, g_ref, wg_ref, wu_ref, wd_ref, gf_ref, o_ref, h_ref, acc_ref, *, final_norm):
    f = pl.program_id(1)

    @pl.when(f == 0)
    def _():
        x = x_ref[...]
        h = _rms(x, g_ref[...]).astype(h_ref.dtype)
        h_ref[...] = h
        acc_ref[...] = x + _swiglu_step(h, wg_ref, wu_ref, wd_ref)

    @pl.when(f != 0)
    def _():
        acc_ref[...] += _swiglu_step(h_ref[...], wg_ref, wu_ref, wd_ref)

    @pl.when(f == pl.num_programs(1) - 1)
    def _():
        out = acc_ref[...]
        if final_norm:
            out = _rms(out, gf_ref[...])
        o_ref[...] = out


def swiglu_ffn(x, g, w_gate, w_up, w_down, final_gain, *, final_norm, tm=1024, tf=512):
    n, d = x.shape
    ff = w_gate.shape[1]
    tm = _tile(n, tm)
    tf = tf if ff % tf == 0 else ff // 2
    return pl.pallas_call(
        functools.partial(_ffn_kernel, final_norm=final_norm),
        grid=(n // tm, ff // tf),
        in_specs=[pl.BlockSpec((tm, d), lambda i, f: (i, 0)),
                  pl.BlockSpec((1, d), lambda i, f: (0, 0)),
                  pl.BlockSpec((d, tf), lambda i, f: (0, f)),
                  pl.BlockSpec((d, tf), lambda i, f: (0, f)),
                  pl.BlockSpec((tf, d), lambda i, f: (f, 0)),
                  pl.BlockSpec((1, d), lambda i, f: (0, 0))],
        out_specs=pl.BlockSpec((tm, d), lambda i, f: (i, 0)),
        out_shape=jax.ShapeDtypeStruct((n, d), jnp.float32),
        scratch_shapes=[pltpu.VMEM((tm, d), jnp.bfloat16), pltpu.VMEM((tm, d), jnp.float32)],
        compiler_params=_cparams("parallel", "arbitrary"),
        name="swiglu_ffn",
    )(x, g.reshape(1, d), w_gate, w_up, w_down, final_gain.reshape(1, d))


MOE_TILE_ROWS = 1024
ROUTE_ROWS = 512


def _pack_bf16_pair(a, b):
    ua = lax.bitcast_convert_type(a.astype(jnp.bfloat16).astype(jnp.float32), jnp.uint32)
    ub = lax.bitcast_convert_type(b.astype(jnp.bfloat16).astype(jnp.float32), jnp.uint32)
    return ua | (ub >> 16)


def _unpack_bf16_pair(p):
    a = lax.bitcast_convert_type(p & jnp.uint32(0xFFFF0000), jnp.float32)
    b = lax.bitcast_convert_type(p << 16, jnp.float32)
    return a, b


def _lane_pick(lane, values):
    out = jnp.zeros(lane.shape, values[0].dtype)
    for k, v in enumerate(values):
        out = jnp.where(lane == k, v, out)
    return out


def _route_kernel(x_ref, g_ref, w_ref, wts_ref, idx_ref, cnt_ref, carry_ref, *, tm):
    @pl.when(pl.program_id(0) == 0)
    def _():
        carry_ref[...] = jnp.zeros(carry_ref.shape, jnp.float32)

    logits = jnp.dot(_rms(x_ref[...], g_ref[...]), w_ref[...],
                     preferred_element_type=jnp.float32, precision=lax.Precision.HIGHEST)
    lane = lax.broadcasted_iota(jnp.int32, logits.shape, 1)
    logits = jnp.where(lane < N_EXPERTS, logits, NEG_BIG)
    m1 = jnp.max(logits, axis=1, keepdims=True)
    i1 = jnp.min(jnp.where(logits == m1, lane, LANES), axis=1, keepdims=True)
    rest = jnp.where(lane == i1, NEG_BIG, logits)
    m2 = jnp.max(rest, axis=1, keepdims=True)
    i2 = jnp.min(jnp.where(rest == m2, lane, LANES), axis=1, keepdims=True)
    e2 = jnp.exp(m2 - m1)
    w1 = 1.0 / (1.0 + e2)
    hot1, hot2 = lane == i1, lane == i2
    both = jnp.logical_or(hot1, hot2).astype(jnp.float32)
    r = lax.broadcasted_iota(jnp.int32, (tm, tm), 0)
    c = lax.broadcasted_iota(jnp.int32, (tm, tm), 1)
    earlier = (c < r).astype(jnp.bfloat16)
    prior = carry_ref[...] + jnp.dot(earlier, both.astype(jnp.bfloat16), preferred_element_type=jnp.float32)
    rank1 = jnp.sum(jnp.where(hot1, prior, 0.0), axis=1, keepdims=True)
    rank2 = jnp.sum(jnp.where(hot2, prior, 0.0), axis=1, keepdims=True)
    carry_ref[...] += jnp.sum(both, axis=0, keepdims=True)
    cnt_ref[...] = jnp.broadcast_to(carry_ref[...], cnt_ref.shape)
    wts_ref[...] = _lane_pick(lane, [w1, e2 * w1])
    idx_ref[...] = _lane_pick(lane, [i1, i2, rank1.astype(jnp.int32), rank2.astype(jnp.int32)])


def moe_route(x, g, w_router):
    n, d = x.shape
    tm = _tile(n, ROUTE_ROWS)
    w = jnp.pad(w_router, ((0, 0), (0, LANES - N_EXPERTS)))
    row = pl.BlockSpec((tm, LANES), lambda i: (i, 0))
    return pl.pallas_call(
        functools.partial(_route_kernel, tm=tm),
        grid=(n // tm,),
        in_specs=[pl.BlockSpec((tm, d), lambda i: (i, 0)),
                  pl.BlockSpec((1, d), lambda i: (0, 0)),
                  pl.BlockSpec((d, LANES), lambda i: (0, 0))],
        out_specs=[row, row, pl.BlockSpec((8, LANES), lambda i: (0, 0))],
        out_shape=[jax.ShapeDtypeStruct((n, LANES), jnp.float32),
                   jax.ShapeDtypeStruct((n, LANES), jnp.int32),
                   jax.ShapeDtypeStruct((8, LANES), jnp.float32)],
        scratch_shapes=[pltpu.VMEM((1, LANES), jnp.float32)],
        compiler_params=_cparams("arbitrary"),
        name="moe_route",
    )(x, g.reshape(1, d), w)


def _dest_kernel(idx_ref, cnt_ref, dest_ref, te_ref, *, tile_rows):
    tiles = jnp.floor((cnt_ref[...] + (tile_rows - 1)) / tile_rows)
    er = lax.broadcasted_iota(jnp.int32, (LANES, LANES), 0)
    ec = lax.broadcasted_iota(jnp.int32, (LANES, LANES), 1)
    first_tile = jnp.dot(tiles.astype(jnp.bfloat16), (er < ec).astype(jnp.bfloat16),
                         preferred_element_type=jnp.float32)
    first_row = first_tile[0:1, :] * tile_rows
    idx = idx_ref[...]
    lane = lax.broadcasted_iota(jnp.int32, idx.shape, 1)
    idx_f = idx.astype(jnp.float32)
    dests = []
    for slot in range(2):
        expert = jnp.sum(jnp.where(lane == slot, idx, 0), axis=1, keepdims=True)
        start = jnp.sum(jnp.where(lane == expert, first_row, 0.0), axis=1, keepdims=True)
        dests.append((start + _lane_col(idx_f, 2 + slot)).astype(jnp.int32))
    dest_ref[...] = _lane_pick(lane, dests)
    last_tile = first_tile + tiles
    tau = lax.broadcasted_iota(jnp.int32, te_ref.shape, 1).astype(jnp.float32)
    owner = jnp.zeros(te_ref.shape, jnp.float32)
    for e in range(N_EXPERTS):
        owner = owner + (_lane_col(last_tile, e) <= tau).astype(jnp.float32)
    te_ref[...] = owner.astype(jnp.int32)


def moe_dest(idx, cnt, n_tiles):
    n = idx.shape[0]
    tm = _tile(n, ROUTE_ROWS)
    te_lanes = -(-n_tiles // LANES) * LANES
    return pl.pallas_call(
        functools.partial(_dest_kernel, tile_rows=MOE_TILE_ROWS),
        grid=(n // tm,),
        in_specs=[pl.BlockSpec((tm, LANES), lambda i: (i, 0)),
                  pl.BlockSpec((8, LANES), lambda i: (0, 0))],
        out_specs=[pl.BlockSpec((tm, LANES), lambda i: (i, 0)),
                   pl.BlockSpec((8, te_lanes), lambda i: (0, 0))],
        out_shape=[jax.ShapeDtypeStruct((n, LANES), jnp.int32),
                   jax.ShapeDtypeStruct((8, te_lanes), jnp.int32)],
        compiler_params=_cparams("arbitrary"),
        name="moe_dest",
    )(idx, cnt)


def _row_copy(src_ref, src_row, dst_ref, dst_row, sem):
    return pltpu.make_async_copy(src_ref.at[pl.ds(src_row, 1)], dst_ref.at[pl.ds(dst_row, 1)], sem)


def _dispatch_kernel(dest_ref, x_ref, g_ref, zeros_ref, xs_ref, hp_ref, sem, *, tm):
    del zeros_ref
    h = _rms(x_ref[...], g_ref[...])
    half = h.shape[1] // 2
    hp_ref[...] = _pack_bf16_pair(h[:, :half], h[:, half:])

    def issue(r, carry):
        for slot in range(2):
            _row_copy(hp_ref, r, xs_ref, dest_ref[0, 2 * r + slot], sem).start()
        return carry

    def drain(r, carry):
        for slot in range(2):
            _row_copy(hp_ref, 0, xs_ref, 0, sem).wait()
        return carry

    lax.fori_loop(0, tm, issue, 0, unroll=8)
    lax.fori_loop(0, tm, drain, 0, unroll=8)


def moe_dispatch(x, g, dest_pairs, n_rows):
    n, d = x.shape
    tm = dest_pairs.shape[2] // 2
    zeros = jnp.zeros((n_rows, d // 2), jnp.uint32)
    return pl.pallas_call(
        functools.partial(_dispatch_kernel, tm=tm),
        grid=(n // tm,),
        in_specs=[pl.BlockSpec((None, 1, 2 * tm), lambda i: (i, 0, 0), memory_space=pltpu.SMEM),
                  pl.BlockSpec((tm, d), lambda i: (i, 0)),
                  pl.BlockSpec((1, d), lambda i: (0, 0)),
                  pl.BlockSpec(memory_space=pl.ANY)],
        out_specs=pl.BlockSpec(memory_space=pl.ANY),
        out_shape=jax.ShapeDtypeStruct((n_rows, d // 2), jnp.uint32),
        scratch_shapes=[pltpu.VMEM((tm, d // 2), jnp.uint32), pltpu.SemaphoreType.DMA(())],
        input_output_aliases={3: 0},
        compiler_params=_cparams("arbitrary"),
        name="moe_dispatch",
    )(dest_pairs, x, g.reshape(1, d), zeros)


def _grouped_ffn_kernel(te_ref, xs_ref, wg_ref, wu_ref, wd_ref, ys_ref, h_ref, acc_ref):
    i, f = pl.program_id(0), pl.program_id(1)
    half = acc_ref.shape[1] // 2

    live = te_ref[i] < N_EXPERTS
    first = f == 0

    @pl.when(jnp.logical_and(first, live))
    def _():
        a, b = _unpack_bf16_pair(xs_ref[...])
        h = jnp.concatenate([a, b], axis=1).astype(h_ref.dtype)
        h_ref[...] = h
        acc_ref[...] = _swiglu_step(h, wg_ref, wu_ref, wd_ref)

    @pl.when(jnp.logical_and(first, jnp.logical_not(live)))
    def _():
        acc_ref[...] = jnp.zeros(acc_ref.shape, jnp.float32)

    @pl.when(jnp.logical_and(jnp.logical_not(first), live))
    def _():
        acc_ref[...] += _swiglu_step(h_ref[...], wg_ref, wu_ref, wd_ref)

    @pl.when(f == pl.num_programs(1) - 1)
    def _():
        acc = acc_ref[...]
        ys_ref[...] = _pack_bf16_pair(acc[:, :half], acc[:, half:])


def moe_grouped_ffn(xs, tile_expert, w_gate, w_up, w_down, *, tf=512):
    n_rows, half = xs.shape
    d = 2 * half
    ff = w_gate.shape[2]
    tm = MOE_TILE_ROWS
    expert = lambda i, te: jnp.minimum(te[i], N_EXPERTS - 1)
    return pl.pallas_call(
        _grouped_ffn_kernel,
        grid_spec=pltpu.PrefetchScalarGridSpec(
            num_scalar_prefetch=1,
            grid=(n_rows // tm, ff // tf),
            in_specs=[pl.BlockSpec((tm, half), lambda i, f, te: (i, 0)),
                      pl.BlockSpec((None, d, tf), lambda i, f, te: (expert(i, te), 0, f)),
                      pl.BlockSpec((None, d, tf), lambda i, f, te: (expert(i, te), 0, f)),
                      pl.BlockSpec((None, tf, d), lambda i, f, te: (expert(i, te), f, 0))],
            out_specs=pl.BlockSpec((tm, half), lambda i, f, te: (i, 0)),
            scratch_shapes=[pltpu.VMEM((tm, d), jnp.bfloat16), pltpu.VMEM((tm, d), jnp.float32)]),
        out_shape=jax.ShapeDtypeStruct((n_rows, half), jnp.uint32),
        compiler_params=_cparams("parallel", "arbitrary"),
        name="moe_grouped_ffn",
    )(tile_expert, xs, w_gate, w_up, w_down)


def _combine_kernel(dest_ref, x_ref, wts_ref, gf_ref, ys_ref, o_ref, y1_ref, y2_ref, sem, *, tm, final_norm):
    bufs = (y1_ref, y2_ref)

    def issue(r, carry):
        for slot in range(2):
            _row_copy(ys_ref, dest_ref[0, 2 * r + slot], bufs[slot], r, sem).start()
        return carry

    def drain(r, carry):
        for slot in range(2):
            _row_copy(ys_ref, 0, bufs[slot], 0, sem).wait()
        return carry

    lax.fori_loop(0, tm, issue, 0, unroll=8)
    lax.fori_loop(0, tm, drain, 0, unroll=8)
    wts = wts_ref[...]
    out = x_ref[...]
    for slot in range(2):
        y = jnp.concatenate(_unpack_bf16_pair(bufs[slot][...]), axis=1)
        out = out + _lane_col(wts, slot) * y
    if final_norm:
        out = _rms(out, gf_ref[...])
    o_ref[...] = out


def moe_combine(x, wts, dest_pairs, ys, final_gain, *, final_norm):
    n, d = x.shape
    tm = dest_pairs.shape[2] // 2
    return pl.pallas_call(
        functools.partial(_combine_kernel, tm=tm, final_norm=final_norm),
        grid=(n // tm,),
        in_specs=[pl.BlockSpec((None, 1, 2 * tm), lambda i: (i, 0, 0), memory_space=pltpu.SMEM),
                  pl.BlockSpec((tm, d), lambda i: (i, 0)),
                  pl.BlockSpec((tm, LANES), lambda i: (i, 0)),
                  pl.BlockSpec((1, d), lambda i: (0, 0)),
                  pl.BlockSpec(memory_space=pl.ANY)],
        out_specs=pl.BlockSpec((tm, d), lambda i: (i, 0)),
        out_shape=jax.ShapeDtypeStruct((n, d), jnp.float32),
        scratch_shapes=[pltpu.VMEM((tm, d // 2), jnp.uint32), pltpu.VMEM((tm, d // 2), jnp.uint32),
                        pltpu.SemaphoreType.DMA(())],
        compiler_params=_cparams("arbitrary"),
        name="moe_combine",
    )(dest_pairs, x, wts, final_gain.reshape(1, d), ys)


def moe_swiglu(x, g, w_router, w_gate, w_up, w_down, final_gain, *, final_norm):
    n = x.shape[0]
    n_tiles = -(-2 * n // MOE_TILE_ROWS) + N_EXPERTS
    wts, idx, cnt = moe_route(x, g, w_router)
    dest, tile_expert = moe_dest(idx, cnt, n_tiles)
    tm = _tile(n, ROUTE_ROWS)
    dest_pairs = dest[:, :2].reshape(n // tm, 1, 2 * tm)
    xs = moe_dispatch(x, g, dest_pairs, n_tiles * MOE_TILE_ROWS)
    ys = moe_grouped_ffn(xs, tile_expert[0, :n_tiles], w_gate, w_up, w_down)
    return moe_combine(x, wts, dest_pairs, ys, final_gain, final_norm=final_norm)


def _rope_tables(seq):
    half = DA_DIM // 2
    inv_freq = 1.0 / (ROPE_THETA ** (jnp.arange(0, DA_DIM, 2, dtype=jnp.float32) / DA_DIM))
    ang = jnp.arange(seq, dtype=jnp.float32)[:, None] * inv_freq[None, :]
    cos, sin = jnp.cos(ang), jnp.sin(ang)
    reps = LANES // half
    sign = jnp.tile(jnp.concatenate([-jnp.ones((half,)), jnp.ones((half,))]), reps // 2)
    return jnp.tile(cos, (1, reps)), jnp.tile(sin, (1, reps)) * sign[None, :]


def kernel(x, mem, norm_mix, w_in, gd_conv, lam_q1, lam_k1, lam_q2, lam_k2, da_subln, gd_a_log,
           gd_dt_bias, gd_norm, w_br_a, w_br_b, w_br_c, w_out, norm_x, norm_mem, w_q_x, w_kv_x,
           w_o_x, norm_ffn, w_gate_dense, w_up_dense, w_down_dense, w_router, w_gate_exp,
           w_up_exp, w_down_exp, norm_final):
    batch, seq, d = x.shape
    n_mem = mem.shape[1]
    depth = norm_mix.shape[0]
    bf = jnp.bfloat16
    n = batch * seq
    cos, sin = _rope_tables(seq)
    xf = x.reshape(n, d)
    memf = mem.reshape(batch * n_mem, d)
    ab0 = MIX_W
    ab1 = MIX_W + 2 * GD_HEADS

    for l in range(depth):
        lam_init = 0.8 - 0.6 * math.exp(-0.3 * l)
        w_l = w_in[l]
        w_main = jnp.concatenate([w_l[:, :ab0], w_l[:, ab1:]], axis=1).astype(bf)
        w_ab = jnp.pad(w_l[:, ab0:ab1], ((0, 0), (0, LANES - 2 * GD_HEADS))).astype(bf)
        proj = rms_matmul(xf, norm_mix[l], w_main, bf)
        ab = rms_matmul(xf, norm_mix[l], w_ab, jnp.float32)

        q_rope, k_rope = rope_qk(proj, cos, sin, seq)
        lam_params = jnp.stack([lam_q1[l], lam_k1[l], lam_q2[l], lam_k2[l]])
        o_a = diff_attention(q_rope, k_rope, proj, lam_params, da_subln[l], batch, seq, lam_init)
        o_b = stick_breaking_attention(proj, batch, seq)
        qkv = gd_conv_silu(proj, gd_conv[l], batch, seq)
        gate = gd_gates(ab, gd_a_log[l], gd_dt_bias[l])
        u, w, qg, kd, amat = gd_local(qkv, gate, batch, seq)
        o_c = gd_scan(u, w, qg, kd, amat, gate, proj, gd_norm[l], batch, seq)
        xf = merge_branches(xf, o_a, o_b, o_c, proj, w_br_a[l].astype(bf), w_br_b[l].astype(bf),
                            w_br_c[l].astype(bf), w_out[l].astype(bf))

        kv = rms_matmul(memf, norm_mem[l], w_kv_x[l].astype(bf), bf)
        xf = cross_attention(xf, norm_x[l], w_q_x[l].astype(bf), kv, w_o_x[l].astype(bf),
                             batch, seq, n_mem)

        last = l == depth - 1
        i = l // 2
        if l % 2 == 0:
            xf = swiglu_ffn(xf, norm_ffn[l], w_gate_dense[i].astype(bf), w_up_dense[i].astype(bf),
                            w_down_dense[i].astype(bf), norm_final, final_norm=last)
        else:
            xf = moe_swiglu(xf, norm_ffn[l], w_router[i], w_gate_exp[i].astype(bf), w_up_exp[i].astype(bf),
                            w_down_exp[i].astype(bf), norm_final, final_norm=last)
    return xf.reshape(batch, seq, d)
```

```python
import functools
import math

import jax
import jax.numpy as jnp
from jax import lax
from jax.experimental import pallas as pl
from jax.experimental.pallas import tpu as pltpu

EPS = 1e-6
LOG2E = math.log2(math.e)
ROPE_THETA = 10000.0
LANES = 128

DA_HEADS = 4
DA_DIM = 64
SB_HEADS = 4
SB_DIM = 128
GD_HEADS = 4
GD_DIM = 128
GD_CONV = 4
GD_CHUNK = 64
X_HEADS = 4
X_DIM = 128
N_EXPERTS = 8
HEAD_W = 512

COL_DA_Q, COL_DA_K, COL_DA_V = 0, 4, 8
COL_SB_Q, COL_SB_K, COL_SB_V = 12, 16, 20
COL_GD_Q, COL_GD_Z = 24, 36
MIX_W = 5120

SB_LOG2_FLOOR = -110.0 * LOG2E
NEG_BIG = -1e30
VMEM_LIMIT = 48 * 1024 * 1024

_NT = (((1,), (1,)), ((), ()))
_TN = (((0,), (0,)), ((), ()))


def _cparams(*sem):
    return pltpu.CompilerParams(dimension_semantics=sem, vmem_limit_bytes=VMEM_LIMIT)


def _tile(n, pref):
    t = min(n, pref)
    while n % t:
        t //= 2
    return t


def _bdot(a, b, dims=None):
    a = a.astype(jnp.bfloat16)
    b = b.astype(jnp.bfloat16)
    if dims is None:
        return jnp.dot(a, b, preferred_element_type=jnp.float32)
    return lax.dot_general(a, b, dims, preferred_element_type=jnp.float32)


def _dot_split(a, b_exact):
    hi = a.astype(jnp.bfloat16)
    lo = (a - hi.astype(jnp.float32)).astype(jnp.bfloat16)
    return (jnp.dot(hi, b_exact, preferred_element_type=jnp.float32)
            + jnp.dot(lo, b_exact, preferred_element_type=jnp.float32))


def _rms(x, g):
    return x * lax.rsqrt(jnp.mean(x * x, axis=-1, keepdims=True) + EPS) * g


def _softplus(z):
    return jnp.maximum(z, 0.0) + jnp.log1p(jnp.exp(-jnp.abs(z)))


def _silu(x):
    return x * jax.nn.sigmoid(x)


def _lane_tile(x, reps):
    return jnp.concatenate([x] * reps, axis=1)


def _lane_col(x, lane):
    ids = lax.broadcasted_iota(jnp.int32, x.shape, 1)
    return jnp.sum(jnp.where(ids == lane, x, 0.0), axis=1, keepdims=True)


def _rms_matmul_kernel(x_ref, g_ref, w_ref, o_ref, h_ref):
    first = pl.program_id(1) == 0

    @pl.when(first)
    def _():
        h = _rms(x_ref[...], g_ref[...]).astype(h_ref.dtype)
        h_ref[...] = h
        o_ref[...] = jnp.dot(h, w_ref[...], preferred_element_type=jnp.float32).astype(o_ref.dtype)

    @pl.when(jnp.logical_not(first))
    def _():
        o_ref[...] = jnp.dot(h_ref[...], w_ref[...], preferred_element_type=jnp.float32).astype(o_ref.dtype)


def rms_matmul(x, g, w, out_dtype, *, tm=1024, tn=1024):
    n, d = x.shape
    nout = w.shape[1]
    tm, tn = _tile(n, tm), _tile(nout, tn)
    return pl.pallas_call(
        _rms_matmul_kernel,
        grid=(n // tm, nout // tn),
        in_specs=[pl.BlockSpec((tm, d), lambda i, j: (i, 0)),
                  pl.BlockSpec((1, d), lambda i, j: (0, 0)),
                  pl.BlockSpec((d, tn), lambda i, j: (0, j))],
        out_specs=pl.BlockSpec((tm, tn), lambda i, j: (i, j)),
        out_shape=jax.ShapeDtypeStruct((n, nout), out_dtype),
        scratch_shapes=[pltpu.VMEM((tm, d), w.dtype)],
        compiler_params=_cparams("parallel", "arbitrary"),
        name="rms_matmul",
    )(x, g.reshape(1, d), w)


def _rope_kernel(q_ref, k_ref, cos_ref, sin_ref, qo_ref, ko_ref):
    cos, sin = cos_ref[...], sin_ref[...]
    lane = lax.broadcasted_iota(jnp.int32, cos.shape, 1)
    first_half = (lane % DA_DIM) < (DA_DIM // 2)

    def rot(t):
        partner = jnp.where(first_half, pltpu.roll(t, LANES - DA_DIM // 2, 1),
                            pltpu.roll(t, DA_DIM // 2, 1))
        return t * cos + partner * sin

    for h in range(DA_HEADS):
        sl = slice(h * LANES, (h + 1) * LANES)
        qo_ref[:, sl] = (rot(q_ref[:, sl].astype(jnp.float32)) * (DA_DIM ** -0.5 * LOG2E)).astype(qo_ref.dtype)
        ko_ref[:, sl] = rot(k_ref[:, sl].astype(jnp.float32)).astype(ko_ref.dtype)


def rope_qk(proj, cos, sin, seq, *, ts=1024):
    n = proj.shape[0]
    ts = _tile(seq, ts)
    per_seq = seq // ts
    out = jax.ShapeDtypeStruct((n, HEAD_W), jnp.bfloat16)
    return pl.pallas_call(
        _rope_kernel,
        grid=(n // ts,),
        in_specs=[pl.BlockSpec((ts, HEAD_W), lambda i: (i, COL_DA_Q // 4)),
                  pl.BlockSpec((ts, HEAD_W), lambda i: (i, COL_DA_K // 4)),
                  pl.BlockSpec((ts, LANES), lambda i: (i % per_seq, 0)),
                  pl.BlockSpec((ts, LANES), lambda i: (i % per_seq, 0))],
        out_specs=[pl.BlockSpec((ts, HEAD_W), lambda i: (i, 0)),
                   pl.BlockSpec((ts, HEAD_W), lambda i: (i, 0))],
        out_shape=[out, out],
        compiler_params=_cparams("parallel"),
        name="rope_qk",
    )(proj, proj, cos, sin)


def _diff_attn_kernel(lam_ref, q_ref, k_ref, v_ref, subln_ref, o_ref,
                      qs_ref, sa_ref, sb_ref, ma_ref, mb_ref, m_ref, acc_ref, *, t, lam_init):
    i = pl.program_id(2)
    q = q_ref[...]
    lane = lax.broadcasted_iota(jnp.int32, q.shape, 1)
    zero = jnp.zeros_like(q)
    qs_ref[:t, :] = jnp.where(lane < DA_DIM, q, zero)
    qs_ref[t:, :] = jnp.where(lane >= DA_DIM, q, zero)
    m_ref[...] = jnp.full(m_ref.shape, NEG_BIG, jnp.float32)
    acc_ref[...] = jnp.zeros(acc_ref.shape, jnp.float32)
    reps = t // LANES
    ones = jnp.ones((t, LANES), jnp.bfloat16)

    def scores(j, s_ref, mx_ref, masked=False):
        k = k_ref[pl.ds(pl.multiple_of(jnp.maximum(j, 0) * t, t), t), :]
        s = lax.dot_general(qs_ref[...], k, _NT, preferred_element_type=jnp.float32)
        if masked:
            row = lax.broadcasted_iota(jnp.int32, s.shape, 0) % t
            col = lax.broadcasted_iota(jnp.int32, s.shape, 1)
            s = jnp.where(col <= row, s, NEG_BIG)
        s_ref[...] = s
        mx_ref[...] = jnp.broadcast_to(jnp.max(s, axis=1, keepdims=True), mx_ref.shape)

    def consume(j, s_ref, mx_ref):
        m_old = m_ref[...]
        m_new = jnp.maximum(m_old, mx_ref[...])
        alpha = jnp.exp2(m_old - m_new)
        p = jnp.exp2(s_ref[...] - _lane_tile(m_new, reps))
        v1 = jnp.concatenate([v_ref[pl.ds(pl.multiple_of(j * t, t), t), :], ones], axis=1)
        acc_ref[...] = _lane_tile(alpha, 2) * acc_ref[...] + jnp.dot(
            p.astype(jnp.bfloat16), v1, preferred_element_type=jnp.float32)
        m_ref[...] = m_new

    scores(i, sa_ref, ma_ref, masked=True)
    scores(i - 1, sb_ref, mb_ref)
    consume(i, sa_ref, ma_ref)

    def earlier_pair(n, carry):
        j = i - 1 - 2 * n
        scores(j - 1, sa_ref, ma_ref)
        consume(j, sb_ref, mb_ref)
        scores(j - 2, sb_ref, mb_ref)
        consume(j - 1, sa_ref, ma_ref)
        return carry

    lax.fori_loop(0, i // 2, earlier_pair, 0)

    @pl.when(i % 2 == 1)
    def _():
        consume(0, sb_ref, mb_ref)

    lq1, lk1, lq2, lk2 = (lam_ref[r:r + 1, :] for r in range(4))
    lam = (jnp.exp(jnp.sum(lq1 * lk1, axis=1, keepdims=True))
           - jnp.exp(jnp.sum(lq2 * lk2, axis=1, keepdims=True)) + lam_init)
    acc = acc_ref[:, :LANES] / acc_ref[:, LANES:]
    o = acc[:t] - lam * acc[t:]
    o_ref[...] = (_rms(o, subln_ref[...]) * (1.0 - lam_init)).astype(o_ref.dtype)


def diff_attention(q_rope, k_rope, proj, lam_params, subln, batch, seq, lam_init, *, t=512):
    n = q_rope.shape[0]
    t = _tile(seq, t)
    nq = seq // t
    return pl.pallas_call(
        functools.partial(_diff_attn_kernel, t=t, lam_init=lam_init),
        grid=(batch, DA_HEADS, nq),
        in_specs=[pl.BlockSpec((4, DA_DIM), lambda b, h, i: (0, 0)),
                  pl.BlockSpec((t, LANES), lambda b, h, i: (b * nq + i, h)),
                  pl.BlockSpec((seq, LANES), lambda b, h, i: (b, h)),
                  pl.BlockSpec((seq, LANES), lambda b, h, i: (b, COL_DA_V + h)),
                  pl.BlockSpec((1, LANES), lambda b, h, i: (0, 0))],
        out_specs=pl.BlockSpec((t, LANES), lambda b, h, i: (b * nq + i, h)),
        out_shape=jax.ShapeDtypeStruct((n, HEAD_W), jnp.bfloat16),
        scratch_shapes=[pltpu.VMEM((2 * t, LANES), jnp.bfloat16),
                        pltpu.VMEM((2 * t, t), jnp.float32),
                        pltpu.VMEM((2 * t, t), jnp.float32),
                        pltpu.VMEM((2 * t, LANES), jnp.float32),
                        pltpu.VMEM((2 * t, LANES), jnp.float32),
                        pltpu.VMEM((2 * t, LANES), jnp.float32),
                        pltpu.VMEM((2 * t, 2 * LANES), jnp.float32)],
        compiler_params=_cparams("parallel", "parallel", "arbitrary"),
        name="diff_attention",
    )(lam_params, q_rope, k_rope, proj, subln.reshape(1, LANES))


def _sb_attn_kernel(q_ref, k_ref, v_ref, o_ref, za_ref, zb_ref, carry_ref, acc_ref, *, tq):
    i = pl.program_id(2)
    qs_val = (q_ref[...].astype(jnp.float32) * (SB_DIM ** -0.5 * LOG2E)).astype(jnp.bfloat16)
    carry_ref[...] = jnp.zeros(carry_ref.shape, jnp.float32)
    acc_ref[...] = jnp.zeros(acc_ref.shape, jnp.float32)
    r = lax.broadcasted_iota(jnp.int32, (tq, tq), 0)
    c = lax.broadcasted_iota(jnp.int32, (tq, tq), 1)
    later = (r > c).astype(jnp.bfloat16)
    reps = tq // LANES

    def tile_rows(j):
        return pl.ds(pl.multiple_of(jnp.maximum(j, 0) * tq, tq), tq)

    def logits(j, z_ref):
        z_ref[...] = lax.dot_general(qs_val, k_ref[tile_rows(j), :], _NT, preferred_element_type=jnp.float32)

    def consume(j, z_ref, masked):
        z = z_ref[...]
        sp = jnp.maximum(z, 0.0) + jnp.log2(1.0 + jnp.exp2(-jnp.abs(z)))
        log_fail = jnp.where(c < r, -sp, 0.0) if masked else -sp
        log_remain = _lane_tile(carry_ref[...], reps) + _dot_split(log_fail, later)
        attn = jnp.exp2(z - sp + log_remain)
        if masked:
            attn = jnp.where(c < r, attn, 0.0)
        acc_ref[...] += jnp.dot(attn.astype(jnp.bfloat16), v_ref[tile_rows(j), :],
                                preferred_element_type=jnp.float32)
        carry_ref[...] += jnp.sum(log_fail, axis=1, keepdims=True)

    def consume_if_exists(j, z_ref):
        carry_ref[...] = jnp.where(j < 0, NEG_BIG, carry_ref[...])
        consume(j, z_ref, False)

    logits(i, za_ref)
    logits(i - 1, zb_ref)
    consume(i, za_ref, True)
    logits(i - 2, za_ref)
    consume_if_exists(i - 1, zb_ref)

    def cond(state):
        j, alive = state
        return jnp.logical_and(j >= 0, alive)

    def body(state):
        j, _ = state
        logits(j - 1, zb_ref)
        consume(j, za_ref, False)
        logits(j - 2, za_ref)
        consume_if_exists(j - 1, zb_ref)
        return j - 2, jnp.max(carry_ref[...]) > SB_LOG2_FLOOR

    lax.while_loop(cond, body, (i - 2, jnp.max(carry_ref[...]) > SB_LOG2_FLOOR))
    o_ref[...] = acc_ref[...].astype(o_ref.dtype)


def stick_breaking_attention(proj, batch, seq, *, tq=256):
    n = proj.shape[0]
    tq = _tile(seq, tq)
    nq = seq // tq
    return pl.pallas_call(
        functools.partial(_sb_attn_kernel, tq=tq),
        grid=(batch, SB_HEADS, nq),
        in_specs=[pl.BlockSpec((tq, LANES), lambda b, h, i: (b * nq + i, COL_SB_Q + h)),
                  pl.BlockSpec((seq, LANES), lambda b, h, i: (b, COL_SB_K + h)),
                  pl.BlockSpec((seq, LANES), lambda b, h, i: (b, COL_SB_V + h))],
        out_specs=pl.BlockSpec((tq, LANES), lambda b, h, i: (b * nq + i, h)),
        out_shape=jax.ShapeDtypeStruct((n, HEAD_W), jnp.bfloat16),
        scratch_shapes=[pltpu.VMEM((tq, tq), jnp.float32),
                        pltpu.VMEM((tq, tq), jnp.float32),
                        pltpu.VMEM((tq, LANES), jnp.float32),
                        pltpu.VMEM((tq, LANES), jnp.float32)],
        compiler_params=_cparams("parallel", "parallel", "arbitrary"),
        name="stick_breaking_attention",
    )(proj, proj, proj)


def _gd_conv_kernel(x_ref, w_ref, o_ref):
    c = pl.program_id(1)
    w = w_ref[...]
    is_q = c < GD_HEADS
    is_v = c >= 2 * GD_HEADS

    def conv_norm(x, causal_mask):
        y = x * w[GD_CONV - 1:GD_CONV, :]
        for back in range(1, GD_CONV):
            shifted = pltpu.roll(x, back, 0)
            if causal_mask:
                row = lax.broadcasted_iota(jnp.int32, x.shape, 0)
                shifted = jnp.where(row >= back, shifted, 0.0)
            y = y + shifted * w[GD_CONV - 1 - back:GD_CONV - back, :]
        y = _silu(y)
        inv = lax.rsqrt(jnp.sum(y * y, axis=1, keepdims=True) + EPS)
        scale = jnp.where(is_v, 1.0, inv * jnp.where(is_q, GD_DIM ** -0.5, 1.0))
        return (y * scale).astype(o_ref.dtype)

    o_ref[...] = conv_norm(x_ref[...].astype(jnp.float32), False)
    o_ref[:16, :] = conv_norm(x_ref[:16, :].astype(jnp.float32), True)


def gd_conv_silu(proj, conv_w, batch, seq):
    n = proj.shape[0]
    ncol = 3 * GD_HEADS
    return pl.pallas_call(
        _gd_conv_kernel,
        grid=(batch, ncol),
        in_specs=[pl.BlockSpec((seq, LANES), lambda b, c: (b, COL_GD_Q + c)),
                  pl.BlockSpec((GD_CONV, LANES), lambda b, c: (0, c))],
        out_specs=pl.BlockSpec((seq, LANES), lambda b, c: (b, c)),
        out_shape=jax.ShapeDtypeStruct((n, ncol * LANES), jnp.float32),
        compiler_params=_cparams("parallel", "parallel"),
        name="gd_conv",
    )(proj, conv_w)


def _gd_gate_kernel(ab_ref, alog_ref, dtb_ref, o_ref, *, rows):
    ab = ab_ref[...]
    is_decay_lane = lax.broadcasted_iota(jnp.int32, (GD_CHUNK, LANES), 1) < GD_HEADS
    g = -jnp.exp(alog_ref[...]) * _softplus(ab + dtb_ref[...])
    beta = jax.nn.sigmoid(ab)
    r = lax.broadcasted_iota(jnp.int32, (GD_CHUNK, GD_CHUNK), 0)
    c = lax.broadcasted_iota(jnp.int32, (GD_CHUNK, GD_CHUNK), 1)
    tril = (c <= r).astype(jnp.bfloat16)
    for ci in range(rows // GD_CHUNK):
        sl = slice(ci * GD_CHUNK, (ci + 1) * GD_CHUNK)
        hi = g[sl].astype(jnp.bfloat16)
        mid_f = g[sl] - hi.astype(jnp.float32)
        mid = mid_f.astype(jnp.bfloat16)
        lo = (mid_f - mid.astype(jnp.float32)).astype(jnp.bfloat16)
        gc = (jnp.dot(tril, hi, preferred_element_type=jnp.float32)
              + jnp.dot(tril, mid, preferred_element_type=jnp.float32)
              + jnp.dot(tril, lo, preferred_element_type=jnp.float32))
        o_ref[sl, :] = jnp.where(is_decay_lane, gc, beta[sl])


def gd_gates(ab, a_log, dt_bias, *, rows=512):
    n = ab.shape[0]
    rows = _tile(n, rows)
    pad = jnp.zeros((LANES - GD_HEADS,), jnp.float32)
    alog = jnp.concatenate([a_log, pad]).reshape(1, LANES)
    dtb = jnp.concatenate([dt_bias, pad]).reshape(1, LANES)
    return pl.pallas_call(
        functools.partial(_gd_gate_kernel, rows=rows),
        grid=(n // rows,),
        in_specs=[pl.BlockSpec((rows, LANES), lambda i: (i, 0)),
                  pl.BlockSpec((1, LANES), lambda i: (0, 0)),
                  pl.BlockSpec((1, LANES), lambda i: (0, 0))],
        out_specs=pl.BlockSpec((rows, LANES), lambda i: (i, 0)),
        out_shape=jax.ShapeDtypeStruct((n, LANES), jnp.float32),
        compiler_params=_cparams("parallel"),
        name="gd_gates",
    )(ab, alog, dtb)


def _unit_lower_inverse(lmats, r, c):
    eye = (r == c).astype(jnp.float32)
    diag16 = (r // 16) == (c // 16)
    ps = [jnp.where(diag16, lm, 0.0) for lm in lmats]
    ts = [eye - p for p in ps]
    for _ in range(3):
        ps = [_bdot(p, p) for p in ps]
        ts = [t + _bdot(t, p) for t, p in zip(ts, ps)]
    for blk in (32, 64):
        band = jnp.logical_and((r // blk) == (c // blk), (r // (blk // 2)) != (c // (blk // 2)))
        tcs = [_bdot(t, jnp.where(band, lm, 0.0)) for t, lm in zip(ts, lmats)]
        ts = [t - _bdot(tc, t) for t, tc in zip(ts, tcs)]
    return ts


def _gd_local_kernel(q_ref, k_ref, v_ref, gate_ref, u_ref, w_ref, qg_ref, kd_ref, a_ref, *, rows):
    h = pl.program_id(1)
    gate = gate_ref[...]
    gc_all = _lane_col(gate, h)
    beta_all = _lane_col(gate, GD_HEADS + h)
    r = lax.broadcasted_iota(jnp.int32, (GD_CHUNK, GD_CHUNK), 0)
    c = lax.broadcasted_iota(jnp.int32, (GD_CHUNK, GD_CHUNK), 1)
    sls = [slice(ci * GD_CHUNK, (ci + 1) * GD_CHUNK) for ci in range(rows // GD_CHUNK)]
    qs = [q_ref[sl, :] for sl in sls]
    ks = [k_ref[sl, :] for sl in sls]
    gcs = [gc_all[sl] for sl in sls]
    kbs = [k * beta_all[sl] for k, sl in zip(ks, sls)]
    kks = [_bdot(kb, k, _NT) for kb, k in zip(kbs, ks)]
    qks = [_bdot(q, k, _NT) for q, k in zip(qs, ks)]
    decays = []
    for gc in gcs:
        gc_rows = jnp.broadcast_to(gc, (GD_CHUNK, LANES))
        gc_cols = jnp.transpose(gc_rows)[0:1, :GD_CHUNK]
        decays.append(jnp.exp(jnp.where(c <= r, gc_rows[:, :GD_CHUNK] - gc_cols, 0.0)))
    tmats = _unit_lower_inverse([jnp.where(c < r, kk * d, 0.0) for kk, d in zip(kks, decays)], r, c)
    egcs = [jnp.exp(gc) for gc in gcs]
    uws = [_bdot(t, jnp.concatenate([v_ref[sl, :] * beta_all[sl], kb * egc], axis=1))
           for t, sl, kb, egc in zip(tmats, sls, kbs, egcs)]
    for sl, q, k, gc, egc, qk, d, uw in zip(sls, qs, ks, gcs, egcs, qks, decays, uws):
        u_ref[sl, :] = uw[:, :LANES]
        w_ref[sl, :] = uw[:, LANES:].astype(w_ref.dtype)
        intra = jnp.where(c <= r, qk * d, 0.0)
        a_ref[sl, :] = jnp.concatenate([intra, jnp.zeros_like(intra)], axis=1).astype(a_ref.dtype)
        qg_ref[sl, :] = (q * egc).astype(qg_ref.dtype)
        g_last = gc[GD_CHUNK - 1:GD_CHUNK, :]
        kd_ref[sl, :] = (k * jnp.exp(g_last - gc)).astype(kd_ref.dtype)


def gd_local(qkv, gate, batch, seq, *, rows=2048):
    n = qkv.shape[0]
    rows = _tile(seq, rows)
    nr = seq // rows
    spec = lambda off: pl.BlockSpec((rows, LANES), lambda b, h, i: (b * nr + i, off + h))
    out_spec = pl.BlockSpec((rows, LANES), lambda b, h, i: (b * nr + i, h))
    bf = jax.ShapeDtypeStruct((n, HEAD_W), jnp.bfloat16)
    return pl.pallas_call(
        functools.partial(_gd_local_kernel, rows=rows),
        grid=(batch, GD_HEADS, nr),
        in_specs=[spec(0), spec(GD_HEADS), spec(2 * GD_HEADS),
                  pl.BlockSpec((rows, LANES), lambda b, h, i: (b * nr + i, 0))],
        out_specs=[out_spec] * 5,
        out_shape=[jax.ShapeDtypeStruct((n, HEAD_W), jnp.float32), bf, bf, bf, bf],
        compiler_params=_cparams("parallel", "parallel", "parallel"),
        name="gd_local",
    )(qkv, qkv, qkv, gate)


def _gd_scan_kernel(u_ref, w_ref, qg_ref, kd_ref, a_ref, gate_ref, z_ref, g_ref, o_ref, state_ref, *, rows, nb):
    @pl.when(pl.program_id(1) == 0)
    def _():
        state_ref[...] = jnp.zeros(state_ref.shape, jnp.float32)

    gain = g_ref[...]
    chains = [(b, h, slice(h * LANES, (h + 1) * LANES)) for b in range(nb) for h in range(GD_HEADS)]

    def chunk(ci, carry):
        start = pl.multiple_of(ci * GD_CHUNK, GD_CHUNK)
        sl = pl.ds(start, GD_CHUNK)
        states = [state_ref[b * GD_HEADS + h] for b, h, _ in chains]
        sbs = [s.astype(jnp.bfloat16) for s in states]
        wq = [jnp.dot(jnp.concatenate([w_ref[b, sl, cs], qg_ref[b, sl, cs]], axis=0), sb,
                      preferred_element_type=jnp.float32) for (b, _, cs), sb in zip(chains, sbs)]
        vbs = [(u_ref[b, sl, cs] - x[:GD_CHUNK]).astype(jnp.bfloat16) for (b, _, cs), x in zip(chains, wq)]
        decay = [jnp.exp(gate_ref[b, pl.ds(start + GD_CHUNK - 1, 1), :]) for b in range(nb)]
        for (b, h, cs), state, vb in zip(chains, states, vbs):
            state_ref[b * GD_HEADS + h] = state * _lane_col(decay[b], h) + lax.dot_general(
                kd_ref[b, sl, cs], vb, _TN, preferred_element_type=jnp.float32)
        outs = [x[GD_CHUNK:] + jnp.dot(a_ref[b, sl, cs][:, :GD_CHUNK], vb, preferred_element_type=jnp.float32)
                for (b, _, cs), x, vb in zip(chains, wq, vbs)]
        for (b, _, cs), out in zip(chains, outs):
            o_ref[b, sl, cs] = (_rms(out, gain) * _silu(z_ref[b, sl, cs].astype(jnp.float32))).astype(o_ref.dtype)
        return carry

    lax.fori_loop(0, rows // GD_CHUNK, chunk, 0)


def gd_scan(u, w, qg, kd, amat, gate, proj, norm_g, batch, seq, *, rows=1024):
    rows = _tile(seq, rows)
    nb = 2 if batch % 2 == 0 else 1
    per_seq = lambda t: t.reshape(batch, seq, t.shape[-1])
    spec = pl.BlockSpec((nb, rows, HEAD_W), lambda b, i: (b, i, 0))
    out = pl.pallas_call(
        functools.partial(_gd_scan_kernel, rows=rows, nb=nb),
        grid=(batch // nb, seq // rows),
        in_specs=[spec, spec, spec, spec, spec,
                  pl.BlockSpec((nb, rows, LANES), lambda b, i: (b, i, 0)),
                  pl.BlockSpec((nb, rows, HEAD_W), lambda b, i: (b, i, COL_GD_Z // GD_HEADS)),
                  pl.BlockSpec((1, LANES), lambda b, i: (0, 0))],
        out_specs=spec,
        out_shape=jax.ShapeDtypeStruct((batch, seq, HEAD_W), jnp.bfloat16),
        scratch_shapes=[pltpu.VMEM((nb * GD_HEADS, GD_DIM, GD_DIM), jnp.float32)],
        compiler_params=_cparams("parallel", "arbitrary"),
        name="gd_scan",
    )(per_seq(u), per_seq(w), per_seq(qg), per_seq(kd), per_seq(amat), per_seq(gate), per_seq(proj),
      norm_g.reshape(1, LANES))
    return out.reshape(batch * seq, HEAD_W)


def _merge_kernel(x_ref, oa_ref, ob_ref, oc_ref, ga_ref, gb_ref, gc_ref,
                  wa_ref, wb_ref, wc_ref, wo_ref, o_ref):
    def branch(o, g, w):
        return jax.nn.sigmoid(g[...].astype(jnp.float32)) * jnp.dot(
            o[...], w[...], preferred_element_type=jnp.float32)

    merged = (branch(oa_ref, ga_ref, wa_ref) + branch(ob_ref, gb_ref, wb_ref)
              + branch(oc_ref, gc_ref, wc_ref))
    o_ref[...] = x_ref[...] + jnp.dot(merged.astype(jnp.bfloat16), wo_ref[...],
                                      preferred_element_type=jnp.float32)


def merge_branches(x, o_a, o_b, o_c, proj, w_a, w_b, w_c, w_out, *, tm=512):
    n, d = x.shape
    tm = _tile(n, tm)
    gate0 = MIX_W // d
    row = lambda width: pl.BlockSpec((tm, width), lambda i: (i, 0))
    full = lambda w: pl.BlockSpec(w.shape, lambda i: (0, 0))
    return pl.pallas_call(
        _merge_kernel,
        grid=(n // tm,),
        in_specs=[row(d), row(HEAD_W), row(HEAD_W), row(HEAD_W),
                  pl.BlockSpec((tm, d), lambda i: (i, gate0)),
                  pl.BlockSpec((tm, d), lambda i: (i, gate0 + 1)),
                  pl.BlockSpec((tm, d), lambda i: (i, gate0 + 2)),
                  full(w_a), full(w_b), full(w_c), full(w_out)],
        out_specs=row(d),
        out_shape=jax.ShapeDtypeStruct((n, d), jnp.float32),
        compiler_params=_cparams("parallel"),
        name="merge_branches",
    )(x, o_a, o_b, o_c, proj, proj, proj, w_a, w_b, w_c, w_out)


def _cross_attn_kernel(x_ref, g_ref, wq_ref, kv_ref, wo_ref, o_ref):
    x = x_ref[...]
    q = jnp.dot(_rms(x, g_ref[...]).astype(jnp.bfloat16), wq_ref[...],
                preferred_element_type=jnp.float32) * X_DIM ** -0.5
    heads = []
    width = X_HEADS * X_DIM
    for h in range(X_HEADS):
        sl = slice(h * X_DIM, (h + 1) * X_DIM)
        k = kv_ref[:, sl]
        v = kv_ref[:, width + h * X_DIM: width + (h + 1) * X_DIM]
        s = _bdot(q[:, sl], k, _NT)
        p = jnp.exp(s - jnp.max(s, axis=1, keepdims=True))
        p = p / jnp.sum(p, axis=1, keepdims=True)
        heads.append(jnp.dot(p.astype(jnp.bfloat16), v, preferred_element_type=jnp.float32))
    o = jnp.concatenate(heads, axis=1).astype(jnp.bfloat16)
    o_ref[...] = x + jnp.dot(o, wo_ref[...], preferred_element_type=jnp.float32)


def cross_attention(x, g, w_q, kv, w_o, batch, seq, n_mem, *, tm=512):
    n, d = x.shape
    tm = _tile(seq, tm)
    per_seq = seq // tm
    full = lambda w: pl.BlockSpec(w.shape, lambda b, i: (0, 0))
    return pl.pallas_call(
        _cross_attn_kernel,
        grid=(batch, per_seq),
        in_specs=[pl.BlockSpec((tm, d), lambda b, i: (b * per_seq + i, 0)),
                  pl.BlockSpec((1, d), lambda b, i: (0, 0)),
                  full(w_q),
                  pl.BlockSpec((n_mem, kv.shape[1]), lambda b, i: (b, 0)),
                  full(w_o)],
        out_specs=pl.BlockSpec((tm, d), lambda b, i: (b * per_seq + i, 0)),
        out_shape=jax.ShapeDtypeStruct((n, d), jnp.float32),
        compiler_params=_cparams("parallel", "parallel"),
        name="cross_attention",
    )(x, g.reshape(1, d), w_q, kv, w_o)


def _swiglu_step(h, wg_ref, wu_ref, wd_ref):
    act = (_silu(jnp.dot(h, wg_ref[...], preferred_element_type=jnp.float32))
           * jnp.dot(h, wu_ref[...], preferred_element_type=jnp.float32))
    return jnp.dot(act.astype(jnp.bfloat16), wd_ref[...], preferred_element_type=jnp.float32)


def _ffn_kernel(x_ref, g_ref, wg_ref, wu_ref, wd_ref, gf_ref, o_ref, h_ref, acc_ref, *, final_norm):
    f = pl.program_id(1)

    @pl.when(f == 0)
    def _():
        x = x_ref[...]
        h = _rms(x, g_ref[...]).astype(h_ref.dtype)
        h_ref[...] = h
        acc_ref[...] = x + _swiglu_step(h, wg_ref, wu_ref, wd_ref)

    @pl.when(f != 0)
    def _():
        acc_ref[...] += _swiglu_step(h_ref[...], wg_ref, wu_ref, wd_ref)

    @pl.when(f == pl.num_programs(1) - 1)
    def _():
        out = acc_ref[...]
        if final_norm:
            out = _rms(out, gf_ref[...])
        o_ref[...] = out


def swiglu_ffn(x, g, w_gate, w_up, w_down, final_gain, *, final_norm, tm=1024, tf=512):
    n, d = x.shape
    ff = w_gate.shape[1]
    tm = _tile(n, tm)
    tf = tf if ff % tf == 0 else ff // 2
    return pl.pallas_call(
        functools.partial(_ffn_kernel, final_norm=final_norm),
        grid=(n // tm, ff // tf),
        in_specs=[pl.BlockSpec((tm, d), lambda i, f: (i, 0)),
                  pl.BlockSpec((1, d), lambda i, f: (0, 0)),
                  pl.BlockSpec((d, tf), lambda i, f: (0, f)),
                  pl.BlockSpec((d, tf), lambda i, f: (0, f)),
                  pl.BlockSpec((tf, d), lambda i, f: (f, 0)),
                  pl.BlockSpec((1, d), lambda i, f: (0, 0))],
        out_specs=pl.BlockSpec((tm, d), lambda i, f: (i, 0)),
        out_shape=jax.ShapeDtypeStruct((n, d), jnp.float32),
        scratch_shapes=[pltpu.VMEM((tm, d), jnp.bfloat16), pltpu.VMEM((tm, d), jnp.float32)],
        compiler_params=_cparams("parallel", "arbitrary"),
        name="swiglu_ffn",
    )(x, g.reshape(1, d), w_gate, w_up, w_down, final_gain.reshape(1, d))


MOE_TILE_ROWS = 1024
ROUTE_ROWS = 512


def _pack_bf16_pair(a, b):
    ua = lax.bitcast_convert_type(a.astype(jnp.bfloat16).astype(jnp.float32), jnp.uint32)
    ub = lax.bitcast_convert_type(b.astype(jnp.bfloat16).astype(jnp.float32), jnp.uint32)
    return ua | (ub >> 16)


def _unpack_bf16_pair(p):
    a = lax.bitcast_convert_type(p & jnp.uint32(0xFFFF0000), jnp.float32)
    b = lax.bitcast_convert_type(p << 16, jnp.float32)
    return a, b


def _lane_pick(lane, values):
    out = jnp.zeros(lane.shape, values[0].dtype)
    for k, v in enumerate(values):
        out = jnp.where(lane == k, v, out)
    return out


def _route_kernel(x_ref, g_ref, w_ref, wts_ref, idx_ref, cnt_ref, carry_ref, *, tm):
    @pl.when(pl.program_id(0) == 0)
    def _():
        carry_ref[...] = jnp.zeros(carry_ref.shape, jnp.float32)

    logits = jnp.dot(_rms(x_ref[...], g_ref[...]), w_ref[...],
                     preferred_element_type=jnp.float32, precision=lax.Precision.HIGHEST)
    lane = lax.broadcasted_iota(jnp.int32, logits.shape, 1)
    logits = jnp.where(lane < N_EXPERTS, logits, NEG_BIG)
    m1 = jnp.max(logits, axis=1, keepdims=True)
    i1 = jnp.min(jnp.where(logits == m1, lane, LANES), axis=1, keepdims=True)
    rest = jnp.where(lane == i1, NEG_BIG, logits)
    m2 = jnp.max(rest, axis=1, keepdims=True)
    i2 = jnp.min(jnp.where(rest == m2, lane, LANES), axis=1, keepdims=True)
    e2 = jnp.exp(m2 - m1)
    w1 = 1.0 / (1.0 + e2)
    hot1, hot2 = lane == i1, lane == i2
    both = jnp.logical_or(hot1, hot2).astype(jnp.float32)
    r = lax.broadcasted_iota(jnp.int32, (tm, tm), 0)
    c = lax.broadcasted_iota(jnp.int32, (tm, tm), 1)
    earlier = (c < r).astype(jnp.bfloat16)
    prior = carry_ref[...] + jnp.dot(earlier, both.astype(jnp.bfloat16), preferred_element_type=jnp.float32)
    rank1 = jnp.sum(jnp.where(hot1, prior, 0.0), axis=1, keepdims=True)
    rank2 = jnp.sum(jnp.where(hot2, prior, 0.0), axis=1, keepdims=True)
    carry_ref[...] += jnp.sum(both, axis=0, keepdims=True)
    cnt_ref[...] = jnp.broadcast_to(carry_ref[...], cnt_ref.shape)
    wts_ref[...] = _lane_pick(lane, [w1, e2 * w1])
    idx_ref[...] = _lane_pick(lane, [i1, i2, rank1.astype(jnp.int32), rank2.astype(jnp.int32)])


def moe_route(x, g, w_router):
    n, d = x.shape
    tm = _tile(n, ROUTE_ROWS)
    w = jnp.pad(w_router, ((0, 0), (0, LANES - N_EXPERTS)))
    row = pl.BlockSpec((tm, LANES), lambda i: (i, 0))
    return pl.pallas_call(
        functools.partial(_route_kernel, tm=tm),
        grid=(n // tm,),
        in_specs=[pl.BlockSpec((tm, d), lambda i: (i, 0)),
                  pl.BlockSpec((1, d), lambda i: (0, 0)),
                  pl.BlockSpec((d, LANES), lambda i: (0, 0))],
        out_specs=[row, row, pl.BlockSpec((8, LANES), lambda i: (0, 0))],
        out_shape=[jax.ShapeDtypeStruct((n, LANES), jnp.float32),
                   jax.ShapeDtypeStruct((n, LANES), jnp.int32),
                   jax.ShapeDtypeStruct((8, LANES), jnp.float32)],
        scratch_shapes=[pltpu.VMEM((1, LANES), jnp.float32)],
        compiler_params=_cparams("arbitrary"),
        name="moe_route",
    )(x, g.reshape(1, d), w)


def _dest_kernel(idx_ref, cnt_ref, dest_ref, te_ref, *, tile_rows):
    tiles = jnp.floor((cnt_ref[...] + (tile_rows - 1)) / tile_rows)
    er = lax.broadcasted_iota(jnp.int32, (LANES, LANES), 0)
    ec = lax.broadcasted_iota(jnp.int32, (LANES, LANES), 1)
    first_tile = jnp.dot(tiles.astype(jnp.bfloat16), (er < ec).astype(jnp.bfloat16),
                         preferred_element_type=jnp.float32)
    first_row = first_tile[0:1, :] * tile_rows
    idx = idx_ref[...]
    lane = lax.broadcasted_iota(jnp.int32, idx.shape, 1)
    idx_f = idx.astype(jnp.float32)
    dests = []
    for slot in range(2):
        expert = jnp.sum(jnp.where(lane == slot, idx, 0), axis=1, keepdims=True)
        start = jnp.sum(jnp.where(lane == expert, first_row, 0.0), axis=1, keepdims=True)
        dests.append((start + _lane_col(idx_f, 2 + slot)).astype(jnp.int32))
    dest_ref[...] = _lane_pick(lane, dests)
    last_tile = first_tile + tiles
    tau = lax.broadcasted_iota(jnp.int32, te_ref.shape, 1).astype(jnp.float32)
    owner = jnp.zeros(te_ref.shape, jnp.float32)
    for e in range(N_EXPERTS):
        owner = owner + (_lane_col(last_tile, e) <= tau).astype(jnp.float32)
    te_ref[...] = owner.astype(jnp.int32)


def moe_dest(idx, cnt, n_tiles):
    n = idx.shape[0]
    tm = _tile(n, ROUTE_ROWS)
    te_lanes = -(-n_tiles // LANES) * LANES
    return pl.pallas_call(
        functools.partial(_dest_kernel, tile_rows=MOE_TILE_ROWS),
        grid=(n // tm,),
        in_specs=[pl.BlockSpec((tm, LANES), lambda i: (i, 0)),
                  pl.BlockSpec((8, LANES), lambda i: (0, 0))],
        out_specs=[pl.BlockSpec((tm, LANES), lambda i: (i, 0)),
                   pl.BlockSpec((8, te_lanes), lambda i: (0, 0))],
        out_shape=[jax.ShapeDtypeStruct((n, LANES), jnp.int32),
                   jax.ShapeDtypeStruct((8, te_lanes), jnp.int32)],
        compiler_params=_cparams("arbitrary"),
        name="moe_dest",
    )(idx, cnt)


def _row_copy(src_ref, src_row, dst_ref, dst_row, sem):
    return pltpu.make_async_copy(src_ref.at[pl.ds(src_row, 1)], dst_ref.at[pl.ds(dst_row, 1)], sem)


def _dispatch_kernel(dest_ref, x_ref, g_ref, zeros_ref, xs_ref, hp_ref, sem, *, tm):
    del zeros_ref
    h = _rms(x_ref[...], g_ref[...])
    half = h.shape[1] // 2
    hp_ref[...] = _pack_bf16_pair(h[:, :half], h[:, half:])

    def issue(r, carry):
        for slot in range(2):
            _row_copy(hp_ref, r, xs_ref, dest_ref[0, 2 * r + slot], sem).start()
        return carry

    def drain(r, carry):
        for slot in range(2):
            _row_copy(hp_ref, 0, xs_ref, 0, sem).wait()
        return carry

    lax.fori_loop(0, tm, issue, 0, unroll=8)
    lax.fori_loop(0, tm, drain, 0, unroll=8)


def moe_dispatch(x, g, dest_pairs, n_rows):
    n, d = x.shape
    tm = dest_pairs.shape[2] // 2
    zeros = jnp.zeros((n_rows, d // 2), jnp.uint32)
    return pl.pallas_call(
        functools.partial(_dispatch_kernel, tm=tm),
        grid=(n // tm,),
        in_specs=[pl.BlockSpec((None, 1, 2 * tm), lambda i: (i, 0, 0), memory_space=pltpu.SMEM),
                  pl.BlockSpec((tm, d), lambda i: (i, 0)),
                  pl.BlockSpec((1, d), lambda i: (0, 0)),
                  pl.BlockSpec(memory_space=pl.ANY)],
        out_specs=pl.BlockSpec(memory_space=pl.ANY),
        out_shape=jax.ShapeDtypeStruct((n_rows, d // 2), jnp.uint32),
        scratch_shapes=[pltpu.VMEM((tm, d // 2), jnp.uint32), pltpu.SemaphoreType.DMA(())],
        input_output_aliases={3: 0},
        compiler_params=_cparams("arbitrary"),
        name="moe_dispatch",
    )(dest_pairs, x, g.reshape(1, d), zeros)


def _grouped_ffn_kernel(te_ref, xs_ref, wg_ref, wu_ref, wd_ref, ys_ref, h_ref, acc_ref):
    i, f = pl.program_id(0), pl.program_id(1)
    half = acc_ref.shape[1] // 2

    live = te_ref[i] < N_EXPERTS
    first = f == 0

    @pl.when(jnp.logical_and(first, live))
    def _():
        a, b = _unpack_bf16_pair(xs_ref[...])
        h = jnp.concatenate([a, b], axis=1).astype(h_ref.dtype)
        h_ref[...] = h
        acc_ref[...] = _swiglu_step(h, wg_ref, wu_ref, wd_ref)

    @pl.when(jnp.logical_and(first, jnp.logical_not(live)))
    def _():
        acc_ref[...] = jnp.zeros(acc_ref.shape, jnp.float32)

    @pl.when(jnp.logical_and(jnp.logical_not(first), live))
    def _():
        acc_ref[...] += _swiglu_step(h_ref[...], wg_ref, wu_ref, wd_ref)

    @pl.when(f == pl.num_programs(1) - 1)
    def _():
        acc = acc_ref[...]
        ys_ref[...] = _pack_bf16_pair(acc[:, :half], acc[:, half:])


def moe_grouped_ffn(xs, tile_expert, w_gate, w_up, w_down, *, tf=512):
    n_rows, half = xs.shape
    d = 2 * half
    ff = w_gate.shape[2]
    tm = MOE_TILE_ROWS
    expert = lambda i, te: jnp.minimum(te[i], N_EXPERTS - 1)
    return pl.pallas_call(
        _grouped_ffn_kernel,
        grid_spec=pltpu.PrefetchScalarGridSpec(
            num_scalar_prefetch=1,
            grid=(n_rows // tm, ff // tf),
            in_specs=[pl.BlockSpec((tm, half), lambda i, f, te: (i, 0)),
                      pl.BlockSpec((None, d, tf), lambda i, f, te: (expert(i, te), 0, f)),
                      pl.BlockSpec((None, d, tf), lambda i, f, te: (expert(i, te), 0, f)),
                      pl.BlockSpec((None, tf, d), lambda i, f, te: (expert(i, te), f, 0))],
            out_specs=pl.BlockSpec((tm, half), lambda i, f, te: (i, 0)),
            scratch_shapes=[pltpu.VMEM((tm, d), jnp.bfloat16), pltpu.VMEM((tm, d), jnp.float32)]),
        out_shape=jax.ShapeDtypeStruct((n_rows, half), jnp.uint32),
        compiler_params=_cparams("parallel", "arbitrary"),
        name="moe_grouped_ffn",
    )(tile_expert, xs, w_gate, w_up, w_down)


def _combine_kernel(dest_ref, x_ref, wts_ref, gf_ref, ys_ref, o_ref, y1_ref, y2_ref, sem, *, tm, final_norm):
    bufs = (y1_ref, y2_ref)

    def issue(r, carry):
        for slot in range(2):
            _row_copy(ys_ref, dest_ref[0, 2 * r + slot], bufs[slot], r, sem).start()
        return carry

    def drain(r, carry):
        for slot in range(2):
            _row_copy(ys_ref, 0, bufs[slot], 0, sem).wait()
        return carry

    lax.fori_loop(0, tm, issue, 0, unroll=8)
    lax.fori_loop(0, tm, drain, 0, unroll=8)
    wts = wts_ref[...]
    out = x_ref[...]
    for slot in range(2):
        y = jnp.concatenate(_unpack_bf16_pair(bufs[slot][...]), axis=1)
        out = out + _lane_col(wts, slot) * y
    if final_norm:
        out = _rms(out, gf_ref[...])
    o_ref[...] = out


def moe_combine(x, wts, dest_pairs, ys, final_gain, *, final_norm):
    n, d = x.shape
    tm = dest_pairs.shape[2] // 2
    return pl.pallas_call(
        functools.partial(_combine_kernel, tm=tm, final_norm=final_norm),
        grid=(n // tm,),
        in_specs=[pl.BlockSpec((None, 1, 2 * tm), lambda i: (i, 0, 0), memory_space=pltpu.SMEM),
                  pl.BlockSpec((tm, d), lambda i: (i, 0)),
                  pl.BlockSpec((tm, LANES), lambda i: (i, 0)),
                  pl.BlockSpec((1, d), lambda i: (0, 0)),
                  pl.BlockSpec(memory_space=pl.ANY)],
        out_specs=pl.BlockSpec((tm, d), lambda i: (i, 0)),
        out_shape=jax.ShapeDtypeStruct((n, d), jnp.float32),
        scratch_shapes=[pltpu.VMEM((tm, d // 2), jnp.uint32), pltpu.VMEM((tm, d // 2), jnp.uint32),
                        pltpu.SemaphoreType.DMA(())],
        compiler_params=_cparams("arbitrary"),
        name="moe_combine",
    )(dest_pairs, x, wts, final_gain.reshape(1, d), ys)


def moe_swiglu(x, g, w_router, w_gate, w_up, w_down, final_gain, *, final_norm):
    n = x.shape[0]
    n_tiles = -(-2 * n // MOE_TILE_ROWS) + N_EXPERTS
    wts, idx, cnt = moe_route(x, g, w_router)
    dest, tile_expert = moe_dest(idx, cnt, n_tiles)
    tm = _tile(n, ROUTE_ROWS)
    dest_pairs = dest[:, :2].reshape(n // tm, 1, 2 * tm)
    xs = moe_dispatch(x, g, dest_pairs, n_tiles * MOE_TILE_ROWS)
    ys = moe_grouped_ffn(xs, tile_expert[0, :n_tiles], w_gate, w_up, w_down)
    return moe_combine(x, wts, dest_pairs, ys, final_gain, final_norm=final_norm)


def _rope_tables(seq):
    half = DA_DIM // 2
    inv_freq = 1.0 / (ROPE_THETA ** (jnp.arange(0, DA_DIM, 2, dtype=jnp.float32) / DA_DIM))
    ang = jnp.arange(seq, dtype=jnp.float32)[:, None] * inv_freq[None, :]
    cos, sin = jnp.cos(ang), jnp.sin(ang)
    reps = LANES // half
    sign = jnp.tile(jnp.concatenate([-jnp.ones((half,)), jnp.ones((half,))]), reps // 2)
    return jnp.tile(cos, (1, reps)), jnp.tile(sin, (1, reps)) * sign[None, :]


def kernel(x, mem, norm_mix, w_in, gd_conv, lam_q1, lam_k1, lam_q2, lam_k2, da_subln, gd_a_log,
           gd_dt_bias, gd_norm, w_br_a, w_br_b, w_br_c, w_out, norm_x, norm_mem, w_q_x, w_kv_x,
           w_o_x, norm_ffn, w_gate_dense, w_up_dense, w_down_dense, w_router, w_gate_exp,
           w_up_exp, w_down_exp, norm_final):
    batch, seq, d = x.shape
    n_mem = mem.shape[1]
    depth = norm_mix.shape[0]
    bf = jnp.bfloat16
    n = batch * seq
    cos, sin = _rope_tables(seq)
    xf = x.reshape(n, d)
    memf = mem.reshape(batch * n_mem, d)
    ab0 = MIX_W
    ab1 = MIX_W + 2 * GD_HEADS

    for l in range(depth):
        lam_init = 0.8 - 0.6 * math.exp(-0.3 * l)
        w_l = w_in[l]
        w_main = jnp.concatenate([w_l[:, :ab0], w_l[:, ab1:]], axis=1).astype(bf)
        w_ab = jnp.pad(w_l[:, ab0:ab1], ((0, 0), (0, LANES - 2 * GD_HEADS))).astype(bf)
        proj = rms_matmul(xf, norm_mix[l], w_main, bf)
        ab = rms_matmul(xf, norm_mix[l], w_ab, jnp.float32)

        q_rope, k_rope = rope_qk(proj, cos, sin, seq)
        lam_params = jnp.stack([lam_q1[l], lam_k1[l], lam_q2[l], lam_k2[l]])
        o_a = diff_attention(q_rope, k_rope, proj, lam_params, da_subln[l], batch, seq, lam_init)
        o_b = stick_breaking_attention(proj, batch, seq)
        qkv = gd_conv_silu(proj, gd_conv[l], batch, seq)
        gate = gd_gates(ab, gd_a_log[l], gd_dt_bias[l])
        u, w, qg, kd, amat = gd_local(qkv, gate, batch, seq)
        o_c = gd_scan(u, w, qg, kd, amat, gate, proj, gd_norm[l], batch, seq)
        xf = merge_branches(xf, o_a, o_b, o_c, proj, w_br_a[l].astype(bf), w_br_b[l].astype(bf),
                            w_br_c[l].astype(bf), w_out[l].astype(bf))

        kv = rms_matmul(memf, norm_mem[l], w_kv_x[l].astype(bf), bf)
        xf = cross_attention(xf, norm_x[l], w_q_x[l].astype(bf), kv, w_o_x[l].astype(bf),
                             batch, seq, n_mem)

        last = l == depth - 1
        i = l // 2
        if l % 2 == 0:
            xf = swiglu_ffn(xf, norm_ffn[l], w_gate_dense[i].astype(bf), w_up_dense[i].astype(bf),
                            w_down_dense[i].astype(bf), norm_final, final_norm=last)
        else:
            xf = moe_swiglu(xf, norm_ffn[l], w_router[i], w_gate_exp[i].astype(bf), w_up_exp[i].astype(bf),
                            w_down_exp[i].astype(bf), norm_final, final_norm=last)
    return xf.reshape(batch, seq, d)
```

```python
import functools
import math

import jax
import jax.numpy as jnp
from jax import lax
from jax.experimental import pallas as pl
from jax.experimental.pallas import tpu as pltpu

EPS = 1e-6
LOG2E = math.log2(math.e)
ROPE_THETA = 10000.0
LANES = 128

DA_HEADS = 4
DA_DIM = 64
SB_HEADS = 4
SB_DIM = 128
GD_HEADS = 4
GD_DIM = 128
GD_CONV = 4
GD_CHUNK = 64
X_HEADS = 4
X_DIM = 128
N_EXPERTS = 8
HEAD_W = 512

COL_DA_Q, COL_DA_K, COL_DA_V = 0, 4, 8
COL_SB_Q, COL_SB_K, COL_SB_V = 12, 16, 20
COL_GD_Q, COL_GD_Z = 24, 36
MIX_W = 5120

SB_LOG2_FLOOR = -110.0 * LOG2E
NEG_BIG = -1e30
VMEM_LIMIT = 48 * 1024 * 1024

_NT = (((1,), (1,)), ((), ()))
_TN = (((0,), (0,)), ((), ()))


def _cparams(*sem):
    return pltpu.CompilerParams(dimension_semantics=sem, vmem_limit_bytes=VMEM_LIMIT)


def _tile(n, pref):
    t = min(n, pref)
    while n % t:
        t //= 2
    return t


def _bdot(a, b, dims=None):
    a = a.astype(jnp.bfloat16)
    b = b.astype(jnp.bfloat16)
    if dims is None:
        return jnp.dot(a, b, preferred_element_type=jnp.float32)
    return lax.dot_general(a, b, dims, preferred_element_type=jnp.float32)


def _dot_split(a, b_exact):
    hi = a.astype(jnp.bfloat16)
    lo = (a - hi.astype(jnp.float32)).astype(jnp.bfloat16)
    return (jnp.dot(hi, b_exact, preferred_element_type=jnp.float32)
            + jnp.dot(lo, b_exact, preferred_element_type=jnp.float32))


def _rms(x, g):
    return x * lax.rsqrt(jnp.mean(x * x, axis=-1, keepdims=True) + EPS) * g


def _softplus(z):
    return jnp.maximum(z, 0.0) + jnp.log1p(jnp.exp(-jnp.abs(z)))


def _silu(x):
    return x * jax.nn.sigmoid(x)


def _lane_tile(x, reps):
    return jnp.concatenate([x] * reps, axis=1)


def _lane_col(x, lane):
    ids = lax.broadcasted_iota(jnp.int32, x.shape, 1)
    return jnp.sum(jnp.where(ids == lane, x, 0.0), axis=1, keepdims=True)


def _rms_matmul_kernel(x_ref, g_ref, w_ref, o_ref, h_ref):
    first = pl.program_id(1) == 0

    @pl.when(first)
    def _():
        h = _rms(x_ref[...], g_ref[...]).astype(h_ref.dtype)
        h_ref[...] = h
        o_ref[...] = jnp.dot(h, w_ref[...], preferred_element_type=jnp.float32).astype(o_ref.dtype)

    @pl.when(jnp.logical_not(first))
    def _():
        o_ref[...] = jnp.dot(h_ref[...], w_ref[...], preferred_element_type=jnp.float32).astype(o_ref.dtype)


def rms_matmul(x, g, w, out_dtype, *, tm=1024, tn=1024):
    n, d = x.shape
    nout = w.shape[1]
    tm, tn = _tile(n, tm), _tile(nout, tn)
    return pl.pallas_call(
        _rms_matmul_kernel,
        grid=(n // tm, nout // tn),
        in_specs=[pl.BlockSpec((tm, d), lambda i, j: (i, 0)),
                  pl.BlockSpec((1, d), lambda i, j: (0, 0)),
                  pl.BlockSpec((d, tn), lambda i, j: (0, j))],
        out_specs=pl.BlockSpec((tm, tn), lambda i, j: (i, j)),
        out_shape=jax.ShapeDtypeStruct((n, nout), out_dtype),
        scratch_shapes=[pltpu.VMEM((tm, d), w.dtype)],
        compiler_params=_cparams("parallel", "arbitrary"),
        name="rms_matmul",
    )(x, g.reshape(1, d), w)


def _gd_gates(ab, a_log, dt_bias):
    is_decay_lane = lax.broadcasted_iota(jnp.int32, (GD_CHUNK, LANES), 1) < GD_HEADS
    g = -jnp.exp(a_log) * _softplus(ab + dt_bias)
    beta = jax.nn.sigmoid(ab)
    r = lax.broadcasted_iota(jnp.int32, (GD_CHUNK, GD_CHUNK), 0)
    c = lax.broadcasted_iota(jnp.int32, (GD_CHUNK, GD_CHUNK), 1)
    tril = (c <= r).astype(jnp.bfloat16)
    chunks = []
    for ci in range(ab.shape[0] // GD_CHUNK):
        sl = slice(ci * GD_CHUNK, (ci + 1) * GD_CHUNK)
        hi = g[sl].astype(jnp.bfloat16)
        mid_f = g[sl] - hi.astype(jnp.float32)
        mid = mid_f.astype(jnp.bfloat16)
        lo = (mid_f - mid.astype(jnp.float32)).astype(jnp.bfloat16)
        gc = (jnp.dot(tril, hi, preferred_element_type=jnp.float32)
              + jnp.dot(tril, mid, preferred_element_type=jnp.float32)
              + jnp.dot(tril, lo, preferred_element_type=jnp.float32))
        chunks.append(jnp.where(is_decay_lane, gc, beta[sl]))
    return jnp.concatenate(chunks, axis=0)


def _mixer_proj_kernel(x_ref, g_ref, w_ref, wab_ref, cos_ref, sin_ref, alog_ref, dtb_ref,
                       o_ref, gate_ref, h_ref):
    first = pl.program_id(1) == 0

    @pl.when(first)
    def _():
        h = _rms(x_ref[...], g_ref[...]).astype(h_ref.dtype)
        h_ref[...] = h
        acc = jnp.dot(h, w_ref[...], preferred_element_type=jnp.float32)
        cos, sin = cos_ref[...], sin_ref[...]
        lane = lax.broadcasted_iota(jnp.int32, cos.shape, 1)
        first_half = (lane % DA_DIM) < (DA_DIM // 2)
        for blk in range(2 * DA_HEADS):
            sl = slice(blk * LANES, (blk + 1) * LANES)
            t = acc[:, sl]
            partner = jnp.where(first_half, pltpu.roll(t, LANES - DA_DIM // 2, 1),
                                pltpu.roll(t, DA_DIM // 2, 1))
            rot = t * cos + partner * sin
            if blk < DA_HEADS:
                rot = rot * (DA_DIM ** -0.5 * LOG2E)
            o_ref[:, sl] = rot.astype(o_ref.dtype)
        ab = jnp.dot(h, wab_ref[...], preferred_element_type=jnp.float32)
        gate_ref[...] = _gd_gates(ab, alog_ref[...], dtb_ref[...])

    @pl.when(jnp.logical_not(first))
    def _():
        o_ref[...] = jnp.dot(h_ref[...], w_ref[...], preferred_element_type=jnp.float32).astype(o_ref.dtype)


def mixer_in_proj(x, g, w_main, w_ab, cos, sin, a_log, dt_bias, seq):
    n, d = x.shape
    nout = w_main.shape[1]
    tm = _tile(seq, 1024)
    tn = 2 * DA_HEADS * LANES
    per_seq = seq // tm
    pad = jnp.zeros((LANES - GD_HEADS,), jnp.float32)
    alog = jnp.concatenate([a_log, pad]).reshape(1, LANES)
    dtb = jnp.concatenate([dt_bias, pad]).reshape(1, LANES)
    vec = lambda width: pl.BlockSpec((1, width), lambda i, j: (0, 0))
    table = pl.BlockSpec((tm, LANES), lambda i, j: (i % per_seq, 0))
    return pl.pallas_call(
        _mixer_proj_kernel,
        grid=(n // tm, nout // tn),
        in_specs=[pl.BlockSpec((tm, d), lambda i, j: (i, 0)), vec(d),
                  pl.BlockSpec((d, tn), lambda i, j: (0, j)),
                  pl.BlockSpec((d, LANES), lambda i, j: (0, 0)),
                  table, table, vec(LANES), vec(LANES)],
        out_specs=[pl.BlockSpec((tm, tn), lambda i, j: (i, j)),
                   pl.BlockSpec((tm, LANES), lambda i, j: (i, 0))],
        out_shape=[jax.ShapeDtypeStruct((n, nout), jnp.bfloat16),
                   jax.ShapeDtypeStruct((n, LANES), jnp.float32)],
        scratch_shapes=[pltpu.VMEM((tm, d), jnp.bfloat16)],
        compiler_params=_cparams("parallel", "arbitrary"),
        name="mixer_in_proj",
    )(x, g.reshape(1, d), w_main, w_ab, cos, sin, alog, dtb)


def _diff_attn_kernel(lam_ref, q_ref, k_ref, v_ref, subln_ref, o_ref,
                      qs_ref, sa_ref, sb_ref, ma_ref, mb_ref, m_ref, acc_ref, *, t, lam_init):
    i = pl.program_id(2)
    q = q_ref[...]
    lane = lax.broadcasted_iota(jnp.int32, q.shape, 1)
    zero = jnp.zeros_like(q)
    qs_ref[:t, :] = jnp.where(lane < DA_DIM, q, zero)
    qs_ref[t:, :] = jnp.where(lane >= DA_DIM, q, zero)
    m_ref[...] = jnp.full(m_ref.shape, NEG_BIG, jnp.float32)
    acc_ref[...] = jnp.zeros(acc_ref.shape, jnp.float32)
    reps = t // LANES
    ones = jnp.ones((t, LANES), jnp.bfloat16)

    def scores(j, s_ref, mx_ref, masked=False):
        k = k_ref[pl.ds(pl.multiple_of(jnp.maximum(j, 0) * t, t), t), :]
        s = lax.dot_general(qs_ref[...], k, _NT, preferred_element_type=jnp.float32)
        if masked:
            row = lax.broadcasted_iota(jnp.int32, s.shape, 0) % t
            col = lax.broadcasted_iota(jnp.int32, s.shape, 1)
            s = jnp.where(col <= row, s, NEG_BIG)
        s_ref[...] = s
        mx_ref[...] = jnp.broadcast_to(jnp.max(s, axis=1, keepdims=True), mx_ref.shape)

    def consume(j, s_ref, mx_ref):
        m_old = m_ref[...]
        m_new = jnp.maximum(m_old, mx_ref[...])
        alpha = jnp.exp2(m_old - m_new)
        p = jnp.exp2(s_ref[...] - _lane_tile(m_new, reps))
        v1 = jnp.concatenate([v_ref[pl.ds(pl.multiple_of(j * t, t), t), :], ones], axis=1)
        acc_ref[...] = _lane_tile(alpha, 2) * acc_ref[...] + jnp.dot(
            p.astype(jnp.bfloat16), v1, preferred_element_type=jnp.float32)
        m_ref[...] = m_new

    scores(i, sa_ref, ma_ref, masked=True)
    scores(i - 1, sb_ref, mb_ref)
    consume(i, sa_ref, ma_ref)

    def earlier_pair(n, carry):
        j = i - 1 - 2 * n
        scores(j - 1, sa_ref, ma_ref)
        consume(j, sb_ref, mb_ref)
        scores(j - 2, sb_ref, mb_ref)
        consume(j - 1, sa_ref, ma_ref)
        return carry

    lax.fori_loop(0, i // 2, earlier_pair, 0)

    @pl.when(i % 2 == 1)
    def _():
        consume(0, sb_ref, mb_ref)

    lq1, lk1, lq2, lk2 = (lam_ref[r:r + 1, :] for r in range(4))
    lam = (jnp.exp(jnp.sum(lq1 * lk1, axis=1, keepdims=True))
           - jnp.exp(jnp.sum(lq2 * lk2, axis=1, keepdims=True)) + lam_init)
    acc = acc_ref[:, :LANES] / acc_ref[:, LANES:]
    o = acc[:t] - lam * acc[t:]
    o_ref[...] = (_rms(o, subln_ref[...]) * (1.0 - lam_init)).astype(o_ref.dtype)


def diff_attention(proj, lam_params, subln, batch, seq, lam_init, *, t=512):
    n = proj.shape[0]
    t = _tile(seq, t)
    nq = seq // t
    return pl.pallas_call(
        functools.partial(_diff_attn_kernel, t=t, lam_init=lam_init),
        grid=(batch, DA_HEADS, nq),
        in_specs=[pl.BlockSpec((4, DA_DIM), lambda b, h, i: (0, 0)),
                  pl.BlockSpec((t, LANES), lambda b, h, i: (b * nq + i, COL_DA_Q + h)),
                  pl.BlockSpec((seq, LANES), lambda b, h, i: (b, COL_DA_K + h)),
                  pl.BlockSpec((seq, LANES), lambda b, h, i: (b, COL_DA_V + h)),
                  pl.BlockSpec((1, LANES), lambda b, h, i: (0, 0))],
        out_specs=pl.BlockSpec((t, LANES), lambda b, h, i: (b * nq + i, h)),
        out_shape=jax.ShapeDtypeStruct((n, HEAD_W), jnp.bfloat16),
        scratch_shapes=[pltpu.VMEM((2 * t, LANES), jnp.bfloat16),
                        pltpu.VMEM((2 * t, t), jnp.float32),
                        pltpu.VMEM((2 * t, t), jnp.float32),
                        pltpu.VMEM((2 * t, LANES), jnp.float32),
                        pltpu.VMEM((2 * t, LANES), jnp.float32),
                        pltpu.VMEM((2 * t, LANES), jnp.float32),
                        pltpu.VMEM((2 * t, 2 * LANES), jnp.float32)],
        compiler_params=_cparams("parallel", "parallel", "arbitrary"),
        name="diff_attention",
    )(lam_params, proj, proj, proj, subln.reshape(1, LANES))


def _sb_attn_kernel(q_ref, k_ref, v_ref, o_ref, za_ref, zb_ref, carry_ref, acc_ref, *, tq):
    i = pl.program_id(2)
    qs_val = (q_ref[...].astype(jnp.float32) * (SB_DIM ** -0.5 * LOG2E)).astype(jnp.bfloat16)
    carry_ref[...] = jnp.zeros(carry_ref.shape, jnp.float32)
    acc_ref[...] = jnp.zeros(acc_ref.shape, jnp.float32)
    r = lax.broadcasted_iota(jnp.int32, (tq, tq), 0)
    c = lax.broadcasted_iota(jnp.int32, (tq, tq), 1)
    later = (r > c).astype(jnp.bfloat16)
    reps = tq // LANES

    def tile_rows(j):
        return pl.ds(pl.multiple_of(jnp.maximum(j, 0) * tq, tq), tq)

    def logits(j, z_ref):
        z_ref[...] = lax.dot_general(qs_val, k_ref[tile_rows(j), :], _NT, preferred_element_type=jnp.float32)

    def consume(j, z_ref, masked):
        z = z_ref[...]
        sp = jnp.maximum(z, 0.0) + jnp.log2(1.0 + jnp.exp2(-jnp.abs(z)))
        log_fail = jnp.where(c < r, -sp, 0.0) if masked else -sp
        log_remain = _lane_tile(carry_ref[...], reps) + _dot_split(log_fail, later)
        attn = jnp.exp2(z - sp + log_remain)
        if masked:
            attn = jnp.where(c < r, attn, 0.0)
        acc_ref[...] += jnp.dot(attn.astype(jnp.bfloat16), v_ref[tile_rows(j), :],
                                preferred_element_type=jnp.float32)
        carry_ref[...] += jnp.sum(log_fail, axis=1, keepdims=True)

    def consume_if_exists(j, z_ref):
        carry_ref[...] = jnp.where(j < 0, NEG_BIG, carry_ref[...])
        consume(j, z_ref, False)

    logits(i, za_ref)
    logits(i - 1, zb_ref)
    consume(i, za_ref, True)
    logits(i - 2, za_ref)
    consume_if_exists(i - 1, zb_ref)

    def cond(state):
        j, alive = state
        return jnp.logical_and(j >= 0, alive)

    def body(state):
        j, _ = state
        logits(j - 1, zb_ref)
        consume(j, za_ref, False)
        logits(j - 2, za_ref)
        consume_if_exists(j - 1, zb_ref)
        return j - 2, jnp.max(carry_ref[...]) > SB_LOG2_FLOOR

    lax.while_loop(cond, body, (i - 2, jnp.max(carry_ref[...]) > SB_LOG2_FLOOR))
    o_ref[...] = acc_ref[...].astype(o_ref.dtype)


def stick_breaking_attention(proj, batch, seq, *, tq=256):
    n = proj.shape[0]
    tq = _tile(seq, tq)
    nq = seq // tq
    return pl.pallas_call(
        functools.partial(_sb_attn_kernel, tq=tq),
        grid=(batch, SB_HEADS, nq),
        in_specs=[pl.BlockSpec((tq, LANES), lambda b, h, i: (b * nq + i, COL_SB_Q + h)),
                  pl.BlockSpec((seq, LANES), lambda b, h, i: (b, COL_SB_K + h)),
                  pl.BlockSpec((seq, LANES), lambda b, h, i: (b, COL_SB_V + h))],
        out_specs=pl.BlockSpec((tq, LANES), lambda b, h, i: (b * nq + i, h)),
        out_shape=jax.ShapeDtypeStruct((n, HEAD_W), jnp.bfloat16),
        scratch_shapes=[pltpu.VMEM((tq, tq), jnp.float32),
                        pltpu.VMEM((tq, tq), jnp.float32),
                        pltpu.VMEM((tq, LANES), jnp.float32),
                        pltpu.VMEM((tq, LANES), jnp.float32)],
        compiler_params=_cparams("parallel", "parallel", "arbitrary"),
        name="stick_breaking_attention",
    )(proj, proj, proj)


def _gd_conv_kernel(x_ref, w_ref, o_ref):
    c = pl.program_id(1)
    w = w_ref[...]
    is_q = c < GD_HEADS
    is_v = c >= 2 * GD_HEADS

    def conv_norm(x, causal_mask):
        y = x * w[GD_CONV - 1:GD_CONV, :]
        for back in range(1, GD_CONV):
            shifted = pltpu.roll(x, back, 0)
            if causal_mask:
                row = lax.broadcasted_iota(jnp.int32, x.shape, 0)
                shifted = jnp.where(row >= back, shifted, 0.0)
            y = y + shifted * w[GD_CONV - 1 - back:GD_CONV - back, :]
        y = _silu(y)
        inv = lax.rsqrt(jnp.sum(y * y, axis=1, keepdims=True) + EPS)
        scale = jnp.where(is_v, 1.0, inv * jnp.where(is_q, GD_DIM ** -0.5, 1.0))
        return (y * scale).astype(o_ref.dtype)

    o_ref[...] = conv_norm(x_ref[...].astype(jnp.float32), False)
    o_ref[:16, :] = conv_norm(x_ref[:16, :].astype(jnp.float32), True)


def gd_conv_silu(proj, conv_w, batch, seq):
    n = proj.shape[0]
    ncol = 3 * GD_HEADS
    return pl.pallas_call(
        _gd_conv_kernel,
        grid=(batch, ncol),
        in_specs=[pl.BlockSpec((seq, LANES), lambda b, c: (b, COL_GD_Q + c)),
                  pl.BlockSpec((GD_CONV, LANES), lambda b, c: (0, c))],
        out_specs=pl.BlockSpec((seq, LANES), lambda b, c: (b, c)),
        out_shape=jax.ShapeDtypeStruct((n, ncol * LANES), jnp.float32),
        compiler_params=_cparams("parallel", "parallel"),
        name="gd_conv",
    )(proj, conv_w)


def _unit_lower_inverse(lmats, r, c):
    eye = (r == c).astype(jnp.float32)
    diag16 = (r // 16) == (c // 16)
    ps = [jnp.where(diag16, lm, 0.0) for lm in lmats]
    ts = [eye - p for p in ps]
    for _ in range(3):
        ps = [_bdot(p, p) for p in ps]
        ts = [t + _bdot(t, p) for t, p in zip(ts, ps)]
    for blk in (32, 64):
        band = jnp.logical_and((r // blk) == (c // blk), (r // (blk // 2)) != (c // (blk // 2)))
        tcs = [_bdot(t, jnp.where(band, lm, 0.0)) for t, lm in zip(ts, lmats)]
        ts = [t - _bdot(tc, t) for t, tc in zip(ts, tcs)]
    return ts


def _gd_local_kernel(q_ref, k_ref, v_ref, gate_ref, u_ref, w_ref, qg_ref, kd_ref, a_ref, *, rows):
    h = pl.program_id(1)
    gate = gate_ref[...]
    gc_all = _lane_col(gate, h)
    beta_all = _lane_col(gate, GD_HEADS + h)
    r = lax.broadcasted_iota(jnp.int32, (GD_CHUNK, GD_CHUNK), 0)
    c = lax.broadcasted_iota(jnp.int32, (GD_CHUNK, GD_CHUNK), 1)
    sls = [slice(ci * GD_CHUNK, (ci + 1) * GD_CHUNK) for ci in range(rows // GD_CHUNK)]
    qs = [q_ref[sl, :] for sl in sls]
    ks = [k_ref[sl, :] for sl in sls]
    gcs = [gc_all[sl] for sl in sls]
    kbs = [k * beta_all[sl] for k, sl in zip(ks, sls)]
    kks = [_bdot(kb, k, _NT) for kb, k in zip(kbs, ks)]
    qks = [_bdot(q, k, _NT) for q, k in zip(qs, ks)]
    decays = []
    for gc in gcs:
        gc_rows = jnp.broadcast_to(gc, (GD_CHUNK, LANES))
        gc_cols = jnp.transpose(gc_rows)[0:1, :GD_CHUNK]
        decays.append(jnp.exp(jnp.where(c <= r, gc_rows[:, :GD_CHUNK] - gc_cols, 0.0)))
    tmats = _unit_lower_inverse([jnp.where(c < r, kk * d, 0.0) for kk, d in zip(kks, decays)], r, c)
    egcs = [jnp.exp(gc) for gc in gcs]
    uws = [_bdot(t, jnp.concatenate([v_ref[sl, :] * beta_all[sl], kb * egc], axis=1))
           for t, sl, kb, egc in zip(tmats, sls, kbs, egcs)]
    for sl, q, k, gc, egc, qk, d, uw in zip(sls, qs, ks, gcs, egcs, qks, decays, uws):
        u_ref[sl, :] = uw[:, :LANES]
        w_ref[sl, :] = uw[:, LANES:].astype(w_ref.dtype)
        intra = jnp.where(c <= r, qk * d, 0.0)
        a_ref[sl, :] = jnp.concatenate([intra, jnp.zeros_like(intra)], axis=1).astype(a_ref.dtype)
        qg_ref[sl, :] = (q * egc).astype(qg_ref.dtype)
        g_last = gc[GD_CHUNK - 1:GD_CHUNK, :]
        kd_ref[sl, :] = (k * jnp.exp(g_last - gc)).astype(kd_ref.dtype)


def gd_local(qkv, gate, batch, seq, *, rows=2048):
    n = qkv.shape[0]
    rows = _tile(seq, rows)
    nr = seq // rows
    spec = lambda off: pl.BlockSpec((rows, LANES), lambda b, h, i: (b * nr + i, off + h))
    out_spec = pl.BlockSpec((rows, LANES), lambda b, h, i: (b * nr + i, h))
    bf = jax.ShapeDtypeStruct((n, HEAD_W), jnp.bfloat16)
    return pl.pallas_call(
        functools.partial(_gd_local_kernel, rows=rows),
        grid=(batch, GD_HEADS, nr),
        in_specs=[spec(0), spec(GD_HEADS), spec(2 * GD_HEADS),
                  pl.BlockSpec((rows, LANES), lambda b, h, i: (b * nr + i, 0))],
        out_specs=[out_spec] * 5,
        out_shape=[jax.ShapeDtypeStruct((n, HEAD_W), jnp.float32), bf, bf, bf, bf],
        compiler_params=_cparams("parallel", "parallel", "parallel"),
        name="gd_local",
    )(qkv, qkv, qkv, gate)


def _gd_scan_kernel(u_ref, w_ref, qg_ref, kd_ref, a_ref, gate_ref, z_ref, g_ref, o_ref, state_ref, *, rows, nb):
    @pl.when(pl.program_id(1) == 0)
    def _():
        state_ref[...] = jnp.zeros(state_ref.shape, jnp.float32)

    gain = g_ref[...]
    chains = [(b, h, slice(h * LANES, (h + 1) * LANES)) for b in range(nb) for h in range(GD_HEADS)]

    def chunk(ci, carry):
        start = pl.multiple_of(ci * GD_CHUNK, GD_CHUNK)
        sl = pl.ds(start, GD_CHUNK)
        states = [state_ref[b * GD_HEADS + h] for b, h, _ in chains]
        sbs = [s.astype(jnp.bfloat16) for s in states]
        wq = [jnp.dot(jnp.concatenate([w_ref[b, sl, cs], qg_ref[b, sl, cs]], axis=0), sb,
                      preferred_element_type=jnp.float32) for (b, _, cs), sb in zip(chains, sbs)]
        vbs = [(u_ref[b, sl, cs] - x[:GD_CHUNK]).astype(jnp.bfloat16) for (b, _, cs), x in zip(chains, wq)]
        decay = [jnp.exp(gate_ref[b, pl.ds(start + GD_CHUNK - 1, 1), :]) for b in range(nb)]
        for (b, h, cs), state, vb in zip(chains, states, vbs):
            state_ref[b * GD_HEADS + h] = state * _lane_col(decay[b], h) + lax.dot_general(
                kd_ref[b, sl, cs], vb, _TN, preferred_element_type=jnp.float32)
        outs = [x[GD_CHUNK:] + jnp.dot(a_ref[b, sl, cs][:, :GD_CHUNK], vb, preferred_element_type=jnp.float32)
                for (b, _, cs), x, vb in zip(chains, wq, vbs)]
        for (b, _, cs), out in zip(chains, outs):
            o_ref[b, sl, cs] = (_rms(out, gain) * _silu(z_ref[b, sl, cs].astype(jnp.float32))).astype(o_ref.dtype)
        return carry

    lax.fori_loop(0, rows // GD_CHUNK, chunk, 0)


def gd_scan(u, w, qg, kd, amat, gate, proj, norm_g, batch, seq, *, rows=512, nb=4):
    rows = _tile(seq, rows)
    nb = _tile(batch, nb)
    per_seq = lambda t: t.reshape(batch, seq, t.shape[-1])
    spec = pl.BlockSpec((nb, rows, HEAD_W), lambda b, i: (b, i, 0))
    out = pl.pallas_call(
        functools.partial(_gd_scan_kernel, rows=rows, nb=nb),
        grid=(batch // nb, seq // rows),
        in_specs=[spec, spec, spec, spec, spec,
                  pl.BlockSpec((nb, rows, LANES), lambda b, i: (b, i, 0)),
                  pl.BlockSpec((nb, rows, HEAD_W), lambda b, i: (b, i, COL_GD_Z // GD_HEADS)),
                  pl.BlockSpec((1, LANES), lambda b, i: (0, 0))],
        out_specs=spec,
        out_shape=jax.ShapeDtypeStruct((batch, seq, HEAD_W), jnp.bfloat16),
        scratch_shapes=[pltpu.VMEM((nb * GD_HEADS, GD_DIM, GD_DIM), jnp.float32)],
        compiler_params=_cparams("parallel", "arbitrary"),
        name="gd_scan",
    )(per_seq(u), per_seq(w), per_seq(qg), per_seq(kd), per_seq(amat), per_seq(gate), per_seq(proj),
      norm_g.reshape(1, LANES))
    return out.reshape(batch * seq, HEAD_W)


def _merge_kernel(x_ref, oa_ref, ob_ref, oc_ref, ga_ref, gb_ref, gc_ref,
                  wa_ref, wb_ref, wc_ref, wo_ref, o_ref):
    def branch(o, g, w):
        return jax.nn.sigmoid(g[...].astype(jnp.float32)) * jnp.dot(
            o[...], w[...], preferred_element_type=jnp.float32)

    merged = (branch(oa_ref, ga_ref, wa_ref) + branch(ob_ref, gb_ref, wb_ref)
              + branch(oc_ref, gc_ref, wc_ref))
    o_ref[...] = x_ref[...] + jnp.dot(merged.astype(jnp.bfloat16), wo_ref[...],
                                      preferred_element_type=jnp.float32)


def merge_branches(x, o_a, o_b, o_c, proj, w_a, w_b, w_c, w_out, *, tm=512):
    n, d = x.shape
    tm = _tile(n, tm)
    gate0 = MIX_W // d
    row = lambda width: pl.BlockSpec((tm, width), lambda i: (i, 0))
    full = lambda w: pl.BlockSpec(w.shape, lambda i: (0, 0))
    return pl.pallas_call(
        _merge_kernel,
        grid=(n // tm,),
        in_specs=[row(d), row(HEAD_W), row(HEAD_W), row(HEAD_W),
                  pl.BlockSpec((tm, d), lambda i: (i, gate0)),
                  pl.BlockSpec((tm, d), lambda i: (i, gate0 + 1)),
                  pl.BlockSpec((tm, d), lambda i: (i, gate0 + 2)),
                  full(w_a), full(w_b), full(w_c), full(w_out)],
        out_specs=row(d),
        out_shape=jax.ShapeDtypeStruct((n, d), jnp.float32),
        compiler_params=_cparams("parallel"),
        name="merge_branches",
    )(x, o_a, o_b, o_c, proj, proj, proj, w_a, w_b, w_c, w_out)


def _cross_attn_kernel(x_ref, g_ref, wq_ref, kv_ref, wo_ref, o_ref):
    x = x_ref[...]
    q = jnp.dot(_rms(x, g_ref[...]).astype(jnp.bfloat16), wq_ref[...],
                preferred_element_type=jnp.float32) * (X_DIM ** -0.5 * LOG2E)
    heads = []
    width = X_HEADS * X_DIM
    ones = jnp.ones((kv_ref.shape[0], LANES), jnp.bfloat16)
    for h in range(X_HEADS):
        sl = slice(h * X_DIM, (h + 1) * X_DIM)
        k = kv_ref[:, sl]
        v = kv_ref[:, width + h * X_DIM: width + (h + 1) * X_DIM]
        s = _bdot(q[:, sl], k, _NT)
        p = jnp.exp2(s - jnp.max(s, axis=1, keepdims=True))
        pv = jnp.dot(p.astype(jnp.bfloat16), jnp.concatenate([v, ones], axis=1),
                     preferred_element_type=jnp.float32)
        heads.append(pv[:, :LANES] / pv[:, LANES:])
    o = jnp.concatenate(heads, axis=1).astype(jnp.bfloat16)
    o_ref[...] = x + jnp.dot(o, wo_ref[...], preferred_element_type=jnp.float32)


def cross_attention(x, g, w_q, kv, w_o, batch, seq, n_mem, *, tm=512):
    n, d = x.shape
    tm = _tile(seq, tm)
    per_seq = seq // tm
    full = lambda w: pl.BlockSpec(w.shape, lambda b, i: (0, 0))
    return pl.pallas_call(
        _cross_attn_kernel,
        grid=(batch, per_seq),
        in_specs=[pl.BlockSpec((tm, d), lambda b, i: (b * per_seq + i, 0)),
                  pl.BlockSpec((1, d), lambda b, i: (0, 0)),
                  full(w_q),
                  pl.BlockSpec((n_mem, kv.shape[1]), lambda b, i: (b, 0)),
                  full(w_o)],
        out_specs=pl.BlockSpec((tm, d), lambda b, i: (b * per_seq + i, 0)),
        out_shape=jax.ShapeDtypeStruct((n, d), jnp.float32),
        compiler_params=_cparams("parallel", "parallel"),
        name="cross_attention",
    )(x, g.reshape(1, d), w_q, kv, w_o)


def _swiglu_step(h, wg_ref, wu_ref, wd_ref):
    act = (_silu(jnp.dot(h, wg_ref[...], preferred_element_type=jnp.float32))
           * jnp.dot(h, wu_ref[...], preferred_element_type=jnp.float32))
    return jnp.dot(act.astype(jnp.bfloat16), wd_ref[...], preferred_element_type=jnp.float32)


def _ffn_kernel(x_ref, g_ref, wg_ref, wu_ref, wd_ref, gf_ref, o_ref, h_ref, acc_ref, *, final_norm):
    f = pl.program_id(1)

    @pl.when(f == 0)
    def _():
        x = x_ref[...]
        h = _rms(x, g_ref[...]).astype(h_ref.dtype)
        h_ref[...] = h
        acc_ref[...] = x + _swiglu_step(h, wg_ref, wu_ref, wd_ref)

    @pl.when(f != 0)
    def _():
        acc_ref[...] += _swiglu_step(h_ref[...], wg_ref, wu_ref, wd_ref)

    @pl.when(f == pl.num_programs(1) - 1)
    def _():
        out = acc_ref[...]
        if final_norm:
            out = _rms(out, gf_ref[...])
        o_ref[...] = out


def swiglu_ffn(x, g, w_gate, w_up, w_down, final_gain, *, final_norm, tm=1024, tf=512):
    n, d = x.shape
    ff = w_gate.shape[1]
    tm = _tile(n, tm)
    tf = tf if ff % tf == 0 else ff // 2
    return pl.pallas_call(
        functools.partial(_ffn_kernel, final_norm=final_norm),
        grid=(n // tm, ff // tf),
        in_specs=[pl.BlockSpec((tm, d), lambda i, f: (i, 0)),
                  pl.BlockSpec((1, d), lambda i, f: (0, 0)),
                  pl.BlockSpec((d, tf), lambda i, f: (0, f)),
                  pl.BlockSpec((d, tf), lambda i, f: (0, f)),
                  pl.BlockSpec((tf, d), lambda i, f: (f, 0)),
                  pl.BlockSpec((1, d), lambda i, f: (0, 0))],
        out_specs=pl.BlockSpec((tm, d), lambda i, f: (i, 0)),
        out_shape=jax.ShapeDtypeStruct((n, d), jnp.float32),
        scratch_shapes=[pltpu.VMEM((tm, d), jnp.bfloat16), pltpu.VMEM((tm, d), jnp.float32)],
        compiler_params=_cparams("parallel", "arbitrary"),
        name="swiglu_ffn",
    )(x, g.reshape(1, d), w_gate, w_up, w_down, final_gain.reshape(1, d))


MOE_TILE_ROWS = 1024
ROUTE_ROWS = 512


def _pack_bf16_pair(a, b):
    ua = lax.bitcast_convert_type(a.astype(jnp.bfloat16).astype(jnp.float32), jnp.uint32)
    ub = lax.bitcast_convert_type(b.astype(jnp.bfloat16).astype(jnp.float32), jnp.uint32)
    return ua | (ub >> 16)


def _unpack_bf16_pair(p):
    a = lax.bitcast_convert_type(p & jnp.uint32(0xFFFF0000), jnp.float32)
    b = lax.bitcast_convert_type(p << 16, jnp.float32)
    return a, b


def _lane_pick(lane, values):
    out = jnp.zeros(lane.shape, values[0].dtype)
    for k, v in enumerate(values):
        out = jnp.where(lane == k, v, out)
    return out


def _route_kernel(x_ref, g_ref, w_ref, wts_ref, idx_ref, cnt_ref, carry_ref, *, tm):
    @pl.when(pl.program_id(0) == 0)
    def _():
        carry_ref[...] = jnp.zeros(carry_ref.shape, jnp.float32)

    logits = jnp.dot(_rms(x_ref[...], g_ref[...]), w_ref[...],
                     preferred_element_type=jnp.float32, precision=lax.Precision.HIGHEST)
    lane = lax.broadcasted_iota(jnp.int32, logits.shape, 1)
    logits = jnp.where(lane < N_EXPERTS, logits, NEG_BIG)
    m1 = jnp.max(logits, axis=1, keepdims=True)
    i1 = jnp.min(jnp.where(logits == m1, lane, LANES), axis=1, keepdims=True)
    rest = jnp.where(lane == i1, NEG_BIG, logits)
    m2 = jnp.max(rest, axis=1, keepdims=True)
    i2 = jnp.min(jnp.where(rest == m2, lane, LANES), axis=1, keepdims=True)
    e2 = jnp.exp(m2 - m1)
    w1 = 1.0 / (1.0 + e2)
    hot1, hot2 = lane == i1, lane == i2
    both = jnp.logical_or(hot1, hot2).astype(jnp.float32)
    r = lax.broadcasted_iota(jnp.int32, (tm, tm), 0)
    c = lax.broadcasted_iota(jnp.int32, (tm, tm), 1)
    earlier = (c < r).astype(jnp.bfloat16)
    prior = carry_ref[...] + jnp.dot(earlier, both.astype(jnp.bfloat16), preferred_element_type=jnp.float32)
    rank1 = jnp.sum(jnp.where(hot1, prior, 0.0), axis=1, keepdims=True)
    rank2 = jnp.sum(jnp.where(hot2, prior, 0.0), axis=1, keepdims=True)
    carry_ref[...] += jnp.sum(both, axis=0, keepdims=True)
    cnt_ref[...] = jnp.broadcast_to(carry_ref[...], cnt_ref.shape)
    wts_ref[...] = _lane_pick(lane, [w1, e2 * w1])
    idx_ref[...] = _lane_pick(lane, [i1, i2, rank1.astype(jnp.int32), rank2.astype(jnp.int32)])


def moe_route(x, g, w_router):
    n, d = x.shape
    tm = _tile(n, ROUTE_ROWS)
    w = jnp.pad(w_router, ((0, 0), (0, LANES - N_EXPERTS)))
    row = pl.BlockSpec((tm, LANES), lambda i: (i, 0))
    return pl.pallas_call(
        functools.partial(_route_kernel, tm=tm),
        grid=(n // tm,),
        in_specs=[pl.BlockSpec((tm, d), lambda i: (i, 0)),
                  pl.BlockSpec((1, d), lambda i: (0, 0)),
                  pl.BlockSpec((d, LANES), lambda i: (0, 0))],
        out_specs=[row, row, pl.BlockSpec((8, LANES), lambda i: (0, 0))],
        out_shape=[jax.ShapeDtypeStruct((n, LANES), jnp.float32),
                   jax.ShapeDtypeStruct((n, LANES), jnp.int32),
                   jax.ShapeDtypeStruct((8, LANES), jnp.float32)],
        scratch_shapes=[pltpu.VMEM((1, LANES), jnp.float32)],
        compiler_params=_cparams("arbitrary"),
        name="moe_route",
    )(x, g.reshape(1, d), w)


def _dest_kernel(idx_ref, cnt_ref, dest_ref, te_ref, *, tile_rows):
    tiles = jnp.floor((cnt_ref[...] + (tile_rows - 1)) / tile_rows)
    er = lax.broadcasted_iota(jnp.int32, (LANES, LANES), 0)
    ec = lax.broadcasted_iota(jnp.int32, (LANES, LANES), 1)
    first_tile = jnp.dot(tiles.astype(jnp.bfloat16), (er < ec).astype(jnp.bfloat16),
                         preferred_element_type=jnp.float32)
    first_row = first_tile[0:1, :] * tile_rows
    idx = idx_ref[...]
    lane = lax.broadcasted_iota(jnp.int32, idx.shape, 1)
    idx_f = idx.astype(jnp.float32)
    dests = []
    for slot in range(2):
        expert = jnp.sum(jnp.where(lane == slot, idx, 0), axis=1, keepdims=True)
        start = jnp.sum(jnp.where(lane == expert, first_row, 0.0), axis=1, keepdims=True)
        dests.append((start + _lane_col(idx_f, 2 + slot)).astype(jnp.int32))
    dest_ref[...] = _lane_pick(lane, dests)
    last_tile = first_tile + tiles
    tau = lax.broadcasted_iota(jnp.int32, te_ref.shape, 1).astype(jnp.float32)
    owner = jnp.zeros(te_ref.shape, jnp.float32)
    for e in range(N_EXPERTS):
        owner = owner + (_lane_col(last_tile, e) <= tau).astype(jnp.float32)
    te_ref[...] = owner.astype(jnp.int32)


def moe_dest(idx, cnt, n_tiles):
    n = idx.shape[0]
    tm = _tile(n, ROUTE_ROWS)
    te_lanes = -(-n_tiles // LANES) * LANES
    return pl.pallas_call(
        functools.partial(_dest_kernel, tile_rows=MOE_TILE_ROWS),
        grid=(n // tm,),
        in_specs=[pl.BlockSpec((tm, LANES), lambda i: (i, 0)),
                  pl.BlockSpec((8, LANES), lambda i: (0, 0))],
        out_specs=[pl.BlockSpec((tm, LANES), lambda i: (i, 0)),
                   pl.BlockSpec((8, te_lanes), lambda i: (0, 0))],
        out_shape=[jax.ShapeDtypeStruct((n, LANES), jnp.int32),
                   jax.ShapeDtypeStruct((8, te_lanes), jnp.int32)],
        compiler_params=_cparams("arbitrary"),
        name="moe_dest",
    )(idx, cnt)


def _row_copy(src_ref, src_row, dst_ref, dst_row, sem):
    return pltpu.make_async_copy(src_ref.at[pl.ds(src_row, 1)], dst_ref.at[pl.ds(dst_row, 1)], sem)


def _dispatch_kernel(dest_ref, x_ref, g_ref, zeros_ref, xs_ref, hp_ref, sem, *, tm):
    del zeros_ref
    h = _rms(x_ref[...], g_ref[...])
    half = h.shape[1] // 2
    hp_ref[...] = _pack_bf16_pair(h[:, :half], h[:, half:])

    def issue(r, carry):
        for slot in range(2):
            _row_copy(hp_ref, r, xs_ref, dest_ref[0, 2 * r + slot], sem).start()
        return carry

    def drain(r, carry):
        for slot in range(2):
            _row_copy(hp_ref, 0, xs_ref, 0, sem).wait()
        return carry

    lax.fori_loop(0, tm, issue, 0, unroll=8)
    lax.fori_loop(0, tm, drain, 0, unroll=8)


def moe_dispatch(x, g, dest_pairs, n_rows):
    n, d = x.shape
    tm = dest_pairs.shape[2] // 2
    zeros = jnp.zeros((n_rows, d // 2), jnp.uint32)
    return pl.pallas_call(
        functools.partial(_dispatch_kernel, tm=tm),
        grid=(n // tm,),
        in_specs=[pl.BlockSpec((None, 1, 2 * tm), lambda i: (i, 0, 0), memory_space=pltpu.SMEM),
                  pl.BlockSpec((tm, d), lambda i: (i, 0)),
                  pl.BlockSpec((1, d), lambda i: (0, 0)),
                  pl.BlockSpec(memory_space=pl.ANY)],
        out_specs=pl.BlockSpec(memory_space=pl.ANY),
        out_shape=jax.ShapeDtypeStruct((n_rows, d // 2), jnp.uint32),
        scratch_shapes=[pltpu.VMEM((tm, d // 2), jnp.uint32), pltpu.SemaphoreType.DMA(())],
        input_output_aliases={3: 0},
        compiler_params=_cparams("arbitrary"),
        name="moe_dispatch",
    )(dest_pairs, x, g.reshape(1, d), zeros)


def _grouped_ffn_kernel(te_ref, xs_ref, wg_ref, wu_ref, wd_ref, ys_ref, h_ref, acc_ref):
    i, f = pl.program_id(0), pl.program_id(1)
    half = acc_ref.shape[1] // 2

    live = te_ref[i] < N_EXPERTS
    first = f == 0

    @pl.when(jnp.logical_and(first, live))
    def _():
        a, b = _unpack_bf16_pair(xs_ref[...])
        h = jnp.concatenate([a, b], axis=1).astype(h_ref.dtype)
        h_ref[...] = h
        acc_ref[...] = _swiglu_step(h, wg_ref, wu_ref, wd_ref)

    @pl.when(jnp.logical_and(first, jnp.logical_not(live)))
    def _():
        acc_ref[...] = jnp.zeros(acc_ref.shape, jnp.float32)

    @pl.when(jnp.logical_and(jnp.logical_not(first), live))
    def _():
        acc_ref[...] += _swiglu_step(h_ref[...], wg_ref, wu_ref, wd_ref)

    @pl.when(f == pl.num_programs(1) - 1)
    def _():
        acc = acc_ref[...]
        ys_ref[...] = _pack_bf16_pair(acc[:, :half], acc[:, half:])


def moe_grouped_ffn(xs, tile_expert, w_gate, w_up, w_down, *, tf=512):
    n_rows, half = xs.shape
    d = 2 * half
    ff = w_gate.shape[2]
    tm = MOE_TILE_ROWS
    expert = lambda i, te: jnp.minimum(te[i], N_EXPERTS - 1)
    return pl.pallas_call(
        _grouped_ffn_kernel,
        grid_spec=pltpu.PrefetchScalarGridSpec(
            num_scalar_prefetch=1,
            grid=(n_rows // tm, ff // tf),
            in_specs=[pl.BlockSpec((tm, half), lambda i, f, te: (i, 0)),
                      pl.BlockSpec((None, d, tf), lambda i, f, te: (expert(i, te), 0, f)),
                      pl.BlockSpec((None, d, tf), lambda i, f, te: (expert(i, te), 0, f)),
                      pl.BlockSpec((None, tf, d), lambda i, f, te: (expert(i, te), f, 0))],
            out_specs=pl.BlockSpec((tm, half), lambda i, f, te: (i, 0)),
            scratch_shapes=[pltpu.VMEM((tm, d), jnp.bfloat16), pltpu.VMEM((tm, d), jnp.float32)]),
        out_shape=jax.ShapeDtypeStruct((n_rows, half), jnp.uint32),
        compiler_params=_cparams("parallel", "arbitrary"),
        name="moe_grouped_ffn",
    )(tile_expert, xs, w_gate, w_up, w_down)


def _combine_kernel(dest_ref, x_ref, wts_ref, gf_ref, ys_ref, o_ref, y1_ref, y2_ref, sem, *, tm, final_norm):
    bufs = (y1_ref, y2_ref)

    def issue(r, carry):
        for slot in range(2):
            _row_copy(ys_ref, dest_ref[0, 2 * r + slot], bufs[slot], r, sem).start()
        return carry

    def drain(r, carry):
        for slot in range(2):
            _row_copy(ys_ref, 0, bufs[slot], 0, sem).wait()
        return carry

    lax.fori_loop(0, tm, issue, 0, unroll=8)
    lax.fori_loop(0, tm, drain, 0, unroll=8)
    wts = wts_ref[...]
    out = x_ref[...]
    for slot in range(2):
        y = jnp.concatenate(_unpack_bf16_pair(bufs[slot][...]), axis=1)
        out = out + _lane_col(wts, slot) * y
    if final_norm:
        out = _rms(out, gf_ref[...])
    o_ref[...] = out


def moe_combine(x, wts, dest_pairs, ys, final_gain, *, final_norm):
    n, d = x.shape
    tm = dest_pairs.shape[2] // 2
    return pl.pallas_call(
        functools.partial(_combine_kernel, tm=tm, final_norm=final_norm),
        grid=(n // tm,),
        in_specs=[pl.BlockSpec((None, 1, 2 * tm), lambda i: (i, 0, 0), memory_space=pltpu.SMEM),
                  pl.BlockSpec((tm, d), lambda i: (i, 0)),
                  pl.BlockSpec((tm, LANES), lambda i: (i, 0)),
                  pl.BlockSpec((1, d), lambda i: (0, 0)),
                  pl.BlockSpec(memory_space=pl.ANY)],
        out_specs=pl.BlockSpec((tm, d), lambda i: (i, 0)),
        out_shape=jax.ShapeDtypeStruct((n, d), jnp.float32),
        scratch_shapes=[pltpu.VMEM((tm, d // 2), jnp.uint32), pltpu.VMEM((tm, d // 2), jnp.uint32),
                        pltpu.SemaphoreType.DMA(())],
        compiler_params=_cparams("arbitrary"),
        name="moe_combine",
    )(dest_pairs, x, wts, final_gain.reshape(1, d), ys)


def moe_swiglu(x, g, w_router, w_gate, w_up, w_down, final_gain, *, final_norm):
    n = x.shape[0]
    n_tiles = -(-2 * n // MOE_TILE_ROWS) + N_EXPERTS
    wts, idx, cnt = moe_route(x, g, w_router)
    dest, tile_expert = moe_dest(idx, cnt, n_tiles)
    tm = _tile(n, ROUTE_ROWS)
    dest_pairs = dest[:, :2].reshape(n // tm, 1, 2 * tm)
    xs = moe_dispatch(x, g, dest_pairs, n_tiles * MOE_TILE_ROWS)
    ys = moe_grouped_ffn(xs, tile_expert[0, :n_tiles], w_gate, w_up, w_down)
    return moe_combine(x, wts, dest_pairs, ys, final_gain, final_norm=final_norm)


def _rope_tables(seq):
    half = DA_DIM // 2
    inv_freq = 1.0 / (ROPE_THETA ** (jnp.arange(0, DA_DIM, 2, dtype=jnp.float32) / DA_DIM))
    ang = jnp.arange(seq, dtype=jnp.float32)[:, None] * inv_freq[None, :]
    cos, sin = jnp.cos(ang), jnp.sin(ang)
    reps = LANES // half
    sign = jnp.tile(jnp.concatenate([-jnp.ones((half,)), jnp.ones((half,))]), reps // 2)
    return jnp.tile(cos, (1, reps)), jnp.tile(sin, (1, reps)) * sign[None, :]


def kernel(x, mem, norm_mix, w_in, gd_conv, lam_q1, lam_k1, lam_q2, lam_k2, da_subln, gd_a_log,
           gd_dt_bias, gd_norm, w_br_a, w_br_b, w_br_c, w_out, norm_x, norm_mem, w_q_x, w_kv_x,
           w_o_x, norm_ffn, w_gate_dense, w_up_dense, w_down_dense, w_router, w_gate_exp,
           w_up_exp, w_down_exp, norm_final):
    batch, seq, d = x.shape
    n_mem = mem.shape[1]
    depth = norm_mix.shape[0]
    bf = jnp.bfloat16
    n = batch * seq
    cos, sin = _rope_tables(seq)
    xf = x.reshape(n, d)
    memf = mem.reshape(batch * n_mem, d)
    ab0 = MIX_W
    ab1 = MIX_W + 2 * GD_HEADS

    for l in range(depth):
        lam_init = 0.8 - 0.6 * math.exp(-0.3 * l)
        w_l = w_in[l]
        w_main = jnp.concatenate([w_l[:, :ab0], w_l[:, ab1:]], axis=1).astype(bf)
        w_ab = jnp.pad(w_l[:, ab0:ab1], ((0, 0), (0, LANES - 2 * GD_HEADS))).astype(bf)
        proj, gate = mixer_in_proj(xf, norm_mix[l], w_main, w_ab, cos, sin, gd_a_log[l], gd_dt_bias[l], seq)

        lam_params = jnp.stack([lam_q1[l], lam_k1[l], lam_q2[l], lam_k2[l]])
        o_a = diff_attention(proj, lam_params, da_subln[l], batch, seq, lam_init)
        o_b = stick_breaking_attention(proj, batch, seq)
        qkv = gd_conv_silu(proj, gd_conv[l], batch, seq)
        u, w, qg, kd, amat = gd_local(qkv, gate, batch, seq)
        o_c = gd_scan(u, w, qg, kd, amat, gate, proj, gd_norm[l], batch, seq)
        xf = merge_branches(xf, o_a, o_b, o_c, proj, w_br_a[l].astype(bf), w_br_b[l].astype(bf),
                            w_br_c[l].astype(bf), w_out[l].astype(bf))

        kv = rms_matmul(memf, norm_mem[l], w_kv_x[l].astype(bf), bf)
        xf = cross_attention(xf, norm_x[l], w_q_x[l].astype(bf), kv, w_o_x[l].astype(bf),
                             batch, seq, n_mem)

        last = l == depth - 1
        i = l // 2
        if l % 2 == 0:
            xf = swiglu_ffn(xf, norm_ffn[l], w_gate_dense[i].astype(bf), w_up_dense[i].astype(bf),
                            w_down_dense[i].astype(bf), norm_final, final_norm=last)
        else:
            xf = moe_swiglu(xf, norm_ffn[l], w_router[i], w_gate_exp[i].astype(bf), w_up_exp[i].astype(bf),
                            w_down_exp[i].astype(bf), norm_final, final_norm=last)
    return xf.reshape(batch, seq, d)
```

```python
import functools
import math

import jax
import jax.numpy as jnp
from jax import lax
from jax.experimental import pallas as pl
from jax.experimental.pallas import tpu as pltpu

EPS = 1e-6
LOG2E = math.log2(math.e)
ROPE_THETA = 10000.0
LANES = 128

DA_HEADS = 4
DA_DIM = 64
SB_HEADS = 4
SB_DIM = 128
GD_HEADS = 4
GD_DIM = 128
GD_CONV = 4
GD_CHUNK = 64
X_HEADS = 4
X_DIM = 128
N_EXPERTS = 8
HEAD_W = 512

COL_DA_Q, COL_DA_K, COL_DA_V = 0, 4, 8
COL_SB_Q, COL_SB_K, COL_SB_V = 12, 16, 20
COL_GD_Q, COL_GD_Z = 24, 36
MIX_W = 5120

SB_LOG2_FLOOR = -110.0 * LOG2E
NEG_BIG = -1e30
VMEM_LIMIT = 48 * 1024 * 1024

_NT = (((1,), (1,)), ((), ()))
_TN = (((0,), (0,)), ((), ()))


def _cparams(*sem):
    return pltpu.CompilerParams(dimension_semantics=sem, vmem_limit_bytes=VMEM_LIMIT)


def _tile(n, pref):
    t = min(n, pref)
    while n % t:
        t //= 2
    return t


def _bdot(a, b, dims=None):
    a = a.astype(jnp.bfloat16)
    b = b.astype(jnp.bfloat16)
    if dims is None:
        return jnp.dot(a, b, preferred_element_type=jnp.float32)
    return lax.dot_general(a, b, dims, preferred_element_type=jnp.float32)


def _dot_split(a, b_exact):
    hi = a.astype(jnp.bfloat16)
    lo = (a - hi.astype(jnp.float32)).astype(jnp.bfloat16)
    return (jnp.dot(hi, b_exact, preferred_element_type=jnp.float32)
            + jnp.dot(lo, b_exact, preferred_element_type=jnp.float32))


def _rms(x, g):
    return x * lax.rsqrt(jnp.mean(x * x, axis=-1, keepdims=True) + EPS) * g


def _softplus(z):
    return jnp.maximum(z, 0.0) + jnp.log1p(jnp.exp(-jnp.abs(z)))


def _silu(x):
    return x * jax.nn.sigmoid(x)


def _lane_tile(x, reps):
    return jnp.concatenate([x] * reps, axis=1)


def _lane_col(x, lane):
    ids = lax.broadcasted_iota(jnp.int32, x.shape, 1)
    return jnp.sum(jnp.where(ids == lane, x, 0.0), axis=1, keepdims=True)


def _rms_matmul_kernel(x_ref, g_ref, w_ref, o_ref, h_ref):
    first = pl.program_id(1) == 0

    @pl.when(first)
    def _():
        h = _rms(x_ref[...], g_ref[...]).astype(h_ref.dtype)
        h_ref[...] = h
        o_ref[...] = jnp.dot(h, w_ref[...], preferred_element_type=jnp.float32).astype(o_ref.dtype)

    @pl.when(jnp.logical_not(first))
    def _():
        o_ref[...] = jnp.dot(h_ref[...], w_ref[...], preferred_element_type=jnp.float32).astype(o_ref.dtype)


def rms_matmul(x, g, w, out_dtype, *, tm=1024, tn=1024):
    n, d = x.shape
    nout = w.shape[1]
    tm, tn = _tile(n, tm), _tile(nout, tn)
    return pl.pallas_call(
        _rms_matmul_kernel,
        grid=(n // tm, nout // tn),
        in_specs=[pl.BlockSpec((tm, d), lambda i, j: (i, 0)),
                  pl.BlockSpec((1, d), lambda i, j: (0, 0)),
                  pl.BlockSpec((d, tn), lambda i, j: (0, j))],
        out_specs=pl.BlockSpec((tm, tn), lambda i, j: (i, j)),
        out_shape=jax.ShapeDtypeStruct((n, nout), out_dtype),
        scratch_shapes=[pltpu.VMEM((tm, d), w.dtype)],
        compiler_params=_cparams("parallel", "arbitrary"),
        name="rms_matmul",
    )(x, g.reshape(1, d), w)


def _gd_gates(ab, a_log, dt_bias):
    is_decay_lane = lax.broadcasted_iota(jnp.int32, (GD_CHUNK, LANES), 1) < GD_HEADS
    g = -jnp.exp(a_log) * _softplus(ab + dt_bias)
    beta = jax.nn.sigmoid(ab)
    r = lax.broadcasted_iota(jnp.int32, (GD_CHUNK, GD_CHUNK), 0)
    c = lax.broadcasted_iota(jnp.int32, (GD_CHUNK, GD_CHUNK), 1)
    tril = (c <= r).astype(jnp.bfloat16)
    chunks = []
    for ci in range(ab.shape[0] // GD_CHUNK):
        sl = slice(ci * GD_CHUNK, (ci + 1) * GD_CHUNK)
        hi = g[sl].astype(jnp.bfloat16)
        mid_f = g[sl] - hi.astype(jnp.float32)
        mid = mid_f.astype(jnp.bfloat16)
        lo = (mid_f - mid.astype(jnp.float32)).astype(jnp.bfloat16)
        gc = (jnp.dot(tril, hi, preferred_element_type=jnp.float32)
              + jnp.dot(tril, mid, preferred_element_type=jnp.float32)
              + jnp.dot(tril, lo, preferred_element_type=jnp.float32))
        chunks.append(jnp.where(is_decay_lane, gc, beta[sl]))
    return jnp.concatenate(chunks, axis=0)


def _mixer_proj_kernel(x_ref, g_ref, w_ref, wab_ref, cos_ref, sin_ref, alog_ref, dtb_ref,
                       o_ref, gate_ref, h_ref):
    first = pl.program_id(1) == 0

    @pl.when(first)
    def _():
        h = _rms(x_ref[...], g_ref[...]).astype(h_ref.dtype)
        h_ref[...] = h
        acc = jnp.dot(h, w_ref[...], preferred_element_type=jnp.float32)
        cos, sin = cos_ref[...], sin_ref[...]
        lane = lax.broadcasted_iota(jnp.int32, cos.shape, 1)
        first_half = (lane % DA_DIM) < (DA_DIM // 2)
        for blk in range(2 * DA_HEADS):
            sl = slice(blk * LANES, (blk + 1) * LANES)
            t = acc[:, sl]
            partner = jnp.where(first_half, pltpu.roll(t, LANES - DA_DIM // 2, 1),
                                pltpu.roll(t, DA_DIM // 2, 1))
            rot = t * cos + partner * sin
            if blk < DA_HEADS:
                rot = rot * (DA_DIM ** -0.5 * LOG2E)
            o_ref[:, sl] = rot.astype(o_ref.dtype)
        ab = jnp.dot(h, wab_ref[...], preferred_element_type=jnp.float32)
        gate_ref[...] = _gd_gates(ab, alog_ref[...], dtb_ref[...])

    @pl.when(jnp.logical_not(first))
    def _():
        o_ref[...] = jnp.dot(h_ref[...], w_ref[...], preferred_element_type=jnp.float32).astype(o_ref.dtype)


def mixer_in_proj(x, g, w_main, w_ab, cos, sin, a_log, dt_bias, seq):
    n, d = x.shape
    nout = w_main.shape[1]
    tm = _tile(seq, 1024)
    tn = 2 * DA_HEADS * LANES
    per_seq = seq // tm
    pad = jnp.zeros((LANES - GD_HEADS,), jnp.float32)
    alog = jnp.concatenate([a_log, pad]).reshape(1, LANES)
    dtb = jnp.concatenate([dt_bias, pad]).reshape(1, LANES)
    vec = lambda width: pl.BlockSpec((1, width), lambda i, j: (0, 0))
    table = pl.BlockSpec((tm, LANES), lambda i, j: (i % per_seq, 0))
    return pl.pallas_call(
        _mixer_proj_kernel,
        grid=(n // tm, nout // tn),
        in_specs=[pl.BlockSpec((tm, d), lambda i, j: (i, 0)), vec(d),
                  pl.BlockSpec((d, tn), lambda i, j: (0, j)),
                  pl.BlockSpec((d, LANES), lambda i, j: (0, 0)),
                  table, table, vec(LANES), vec(LANES)],
        out_specs=[pl.BlockSpec((tm, tn), lambda i, j: (i, j)),
                   pl.BlockSpec((tm, LANES), lambda i, j: (i, 0))],
        out_shape=[jax.ShapeDtypeStruct((n, nout), jnp.bfloat16),
                   jax.ShapeDtypeStruct((n, LANES), jnp.float32)],
        scratch_shapes=[pltpu.VMEM((tm, d), jnp.bfloat16)],
        compiler_params=_cparams("parallel", "arbitrary"),
        name="mixer_in_proj",
    )(x, g.reshape(1, d), w_main, w_ab, cos, sin, alog, dtb)


def _diff_attn_kernel(lam_ref, q_ref, k_ref, v_ref, subln_ref, o_ref,
                      qs_ref, sa_ref, sb_ref, ma_ref, mb_ref, m_ref, acc_ref, *, t, lam_init):
    i = pl.program_id(2)
    q = q_ref[...]
    lane = lax.broadcasted_iota(jnp.int32, q.shape, 1)
    zero = jnp.zeros_like(q)
    qs_ref[:t, :] = jnp.where(lane < DA_DIM, q, zero)
    qs_ref[t:, :] = jnp.where(lane >= DA_DIM, q, zero)
    m_ref[...] = jnp.full(m_ref.shape, NEG_BIG, jnp.float32)
    acc_ref[...] = jnp.zeros(acc_ref.shape, jnp.float32)
    reps = t // LANES
    ones = jnp.ones((t, LANES), jnp.bfloat16)

    def scores(j, s_ref, mx_ref, masked=False):
        k = k_ref[pl.ds(pl.multiple_of(jnp.maximum(j, 0) * t, t), t), :]
        s = lax.dot_general(qs_ref[...], k, _NT, preferred_element_type=jnp.float32)
        if masked:
            row = lax.broadcasted_iota(jnp.int32, s.shape, 0) % t
            col = lax.broadcasted_iota(jnp.int32, s.shape, 1)
            s = jnp.where(col <= row, s, NEG_BIG)
        s_ref[...] = s
        mx_ref[...] = jnp.broadcast_to(jnp.max(s, axis=1, keepdims=True), mx_ref.shape)

    def consume(j, s_ref, mx_ref):
        m_old = m_ref[...]
        m_new = jnp.maximum(m_old, mx_ref[...])
        alpha = jnp.exp2(m_old - m_new)
        p = jnp.exp2(s_ref[...] - _lane_tile(m_new, reps))
        v1 = jnp.concatenate([v_ref[pl.ds(pl.multiple_of(j * t, t), t), :], ones], axis=1)
        acc_ref[...] = _lane_tile(alpha, 2) * acc_ref[...] + jnp.dot(
            p.astype(jnp.bfloat16), v1, preferred_element_type=jnp.float32)
        m_ref[...] = m_new

    scores(i, sa_ref, ma_ref, masked=True)
    scores(i - 1, sb_ref, mb_ref)
    consume(i, sa_ref, ma_ref)

    def earlier_pair(n, carry):
        j = i - 1 - 2 * n
        scores(j - 1, sa_ref, ma_ref)
        consume(j, sb_ref, mb_ref)
        scores(j - 2, sb_ref, mb_ref)
        consume(j - 1, sa_ref, ma_ref)
        return carry

    lax.fori_loop(0, i // 2, earlier_pair, 0)

    @pl.when(i % 2 == 1)
    def _():
        consume(0, sb_ref, mb_ref)

    lq1, lk1, lq2, lk2 = (lam_ref[r:r + 1, :] for r in range(4))
    lam = (jnp.exp(jnp.sum(lq1 * lk1, axis=1, keepdims=True))
           - jnp.exp(jnp.sum(lq2 * lk2, axis=1, keepdims=True)) + lam_init)
    acc = acc_ref[:, :LANES] / acc_ref[:, LANES:]
    o = acc[:t] - lam * acc[t:]
    o_ref[...] = (_rms(o, subln_ref[...]) * (1.0 - lam_init)).astype(o_ref.dtype)


def diff_attention(proj, lam_params, subln, batch, seq, lam_init, *, t=512):
    n = proj.shape[0]
    t = _tile(seq, t)
    nq = seq // t
    return pl.pallas_call(
        functools.partial(_diff_attn_kernel, t=t, lam_init=lam_init),
        grid=(batch, DA_HEADS, nq),
        in_specs=[pl.BlockSpec((4, DA_DIM), lambda b, h, i: (0, 0)),
                  pl.BlockSpec((t, LANES), lambda b, h, i: (b * nq + i, COL_DA_Q + h)),
                  pl.BlockSpec((seq, LANES), lambda b, h, i: (b, COL_DA_K + h)),
                  pl.BlockSpec((seq, LANES), lambda b, h, i: (b, COL_DA_V + h)),
                  pl.BlockSpec((1, LANES), lambda b, h, i: (0, 0))],
        out_specs=pl.BlockSpec((t, LANES), lambda b, h, i: (b * nq + i, h)),
        out_shape=jax.ShapeDtypeStruct((n, HEAD_W), jnp.bfloat16),
        scratch_shapes=[pltpu.VMEM((2 * t, LANES), jnp.bfloat16),
                        pltpu.VMEM((2 * t, t), jnp.float32),
                        pltpu.VMEM((2 * t, t), jnp.float32),
                        pltpu.VMEM((2 * t, LANES), jnp.float32),
                        pltpu.VMEM((2 * t, LANES), jnp.float32),
                        pltpu.VMEM((2 * t, LANES), jnp.float32),
                        pltpu.VMEM((2 * t, 2 * LANES), jnp.float32)],
        compiler_params=_cparams("parallel", "parallel", "arbitrary"),
        name="diff_attention",
    )(lam_params, proj, proj, proj, subln.reshape(1, LANES))


def _sb_attn_kernel(q_ref, k_ref, v_ref, o_ref, za_ref, zb_ref, carry_ref, acc_ref, *, tq):
    i = pl.program_id(2)
    qs_val = (q_ref[...].astype(jnp.float32) * (SB_DIM ** -0.5 * LOG2E)).astype(jnp.bfloat16)
    carry_ref[...] = jnp.zeros(carry_ref.shape, jnp.float32)
    acc_ref[...] = jnp.zeros(acc_ref.shape, jnp.float32)
    r = lax.broadcasted_iota(jnp.int32, (tq, tq), 0)
    c = lax.broadcasted_iota(jnp.int32, (tq, tq), 1)
    later = (r > c).astype(jnp.bfloat16)
    reps = tq // LANES

    def tile_rows(j):
        return pl.ds(pl.multiple_of(jnp.maximum(j, 0) * tq, tq), tq)

    def logits(j, z_ref):
        z_ref[...] = lax.dot_general(qs_val, k_ref[tile_rows(j), :], _NT, preferred_element_type=jnp.float32)

    def consume(j, z_ref, masked):
        z = z_ref[...]
        sp = jnp.maximum(z, 0.0) + jnp.log2(1.0 + jnp.exp2(-jnp.abs(z)))
        log_fail = jnp.where(c < r, -sp, 0.0) if masked else -sp
        log_remain = _lane_tile(carry_ref[...], reps) + _dot_split(log_fail, later)
        attn = jnp.exp2(z - sp + log_remain)
        if masked:
            attn = jnp.where(c < r, attn, 0.0)
        acc_ref[...] += jnp.dot(attn.astype(jnp.bfloat16), v_ref[tile_rows(j), :],
                                preferred_element_type=jnp.float32)
        carry_ref[...] += jnp.sum(log_fail, axis=1, keepdims=True)

    def consume_if_exists(j, z_ref):
        carry_ref[...] = jnp.where(j < 0, NEG_BIG, carry_ref[...])
        consume(j, z_ref, False)

    logits(i, za_ref)
    logits(i - 1, zb_ref)
    consume(i, za_ref, True)
    logits(i - 2, za_ref)
    consume_if_exists(i - 1, zb_ref)

    def cond(state):
        j, alive = state
        return jnp.logical_and(j >= 0, alive)

    def body(state):
        j, _ = state
        logits(j - 1, zb_ref)
        consume(j, za_ref, False)
        logits(j - 2, za_ref)
        consume_if_exists(j - 1, zb_ref)
        return j - 2, jnp.max(carry_ref[...]) > SB_LOG2_FLOOR

    lax.while_loop(cond, body, (i - 2, jnp.max(carry_ref[...]) > SB_LOG2_FLOOR))
    o_ref[...] = acc_ref[...].astype(o_ref.dtype)


def stick_breaking_attention(proj, batch, seq, *, tq=256):
    n = proj.shape[0]
    tq = _tile(seq, tq)
    nq = seq // tq
    return pl.pallas_call(
        functools.partial(_sb_attn_kernel, tq=tq),
        grid=(batch, SB_HEADS, nq),
        in_specs=[pl.BlockSpec((tq, LANES), lambda b, h, i: (b * nq + i, COL_SB_Q + h)),
                  pl.BlockSpec((seq, LANES), lambda b, h, i: (b, COL_SB_K + h)),
                  pl.BlockSpec((seq, LANES), lambda b, h, i: (b, COL_SB_V + h))],
        out_specs=pl.BlockSpec((tq, LANES), lambda b, h, i: (b * nq + i, h)),
        out_shape=jax.ShapeDtypeStruct((n, HEAD_W), jnp.bfloat16),
        scratch_shapes=[pltpu.VMEM((tq, tq), jnp.float32),
                        pltpu.VMEM((tq, tq), jnp.float32),
                        pltpu.VMEM((tq, LANES), jnp.float32),
                        pltpu.VMEM((tq, LANES), jnp.float32)],
        compiler_params=_cparams("parallel", "parallel", "arbitrary"),
        name="stick_breaking_attention",
    )(proj, proj, proj)


def _gd_conv_kernel(x_ref, w_ref, o_ref):
    c = pl.program_id(1)
    w = w_ref[...]
    is_q = c < GD_HEADS
    is_v = c >= 2 * GD_HEADS

    def conv_norm(x, causal_mask):
        y = x * w[GD_CONV - 1:GD_CONV, :]
        for back in range(1, GD_CONV):
            shifted = pltpu.roll(x, back, 0)
            if causal_mask:
                row = lax.broadcasted_iota(jnp.int32, x.shape, 0)
                shifted = jnp.where(row >= back, shifted, 0.0)
            y = y + shifted * w[GD_CONV - 1 - back:GD_CONV - back, :]
        y = _silu(y)
        inv = lax.rsqrt(jnp.sum(y * y, axis=1, keepdims=True) + EPS)
        scale = jnp.where(is_v, 1.0, inv * jnp.where(is_q, GD_DIM ** -0.5, 1.0))
        return (y * scale).astype(o_ref.dtype)

    o_ref[...] = conv_norm(x_ref[...].astype(jnp.float32), False)
    o_ref[:16, :] = conv_norm(x_ref[:16, :].astype(jnp.float32), True)


def gd_conv_silu(proj, conv_w, batch, seq):
    n = proj.shape[0]
    ncol = 3 * GD_HEADS
    return pl.pallas_call(
        _gd_conv_kernel,
        grid=(batch, ncol),
        in_specs=[pl.BlockSpec((seq, LANES), lambda b, c: (b, COL_GD_Q + c)),
                  pl.BlockSpec((GD_CONV, LANES), lambda b, c: (0, c))],
        out_specs=pl.BlockSpec((seq, LANES), lambda b, c: (b, c)),
        out_shape=jax.ShapeDtypeStruct((n, ncol * LANES), jnp.float32),
        compiler_params=_cparams("parallel", "parallel"),
        name="gd_conv",
    )(proj, conv_w)


def _unit_lower_inverse(lmats, r, c):
    eye = (r == c).astype(jnp.float32)
    diag16 = (r // 16) == (c // 16)
    ps = [jnp.where(diag16, lm, 0.0) for lm in lmats]
    ts = [eye - p for p in ps]
    for _ in range(3):
        ps = [_bdot(p, p) for p in ps]
        ts = [t + _bdot(t, p) for t, p in zip(ts, ps)]
    for blk in (32, 64):
        band = jnp.logical_and((r // blk) == (c // blk), (r // (blk // 2)) != (c // (blk // 2)))
        tcs = [_bdot(t, jnp.where(band, lm, 0.0)) for t, lm in zip(ts, lmats)]
        ts = [t - _bdot(tc, t) for t, tc in zip(ts, tcs)]
    return ts


def _gd_local_kernel(q_ref, k_ref, v_ref, gate_ref, u_ref, w_ref, qg_ref, kd_ref, a_ref, *, rows):
    h = pl.program_id(1)
    gate = gate_ref[...]
    gc_all = _lane_col(gate, h)
    beta_all = _lane_col(gate, GD_HEADS + h)
    r = lax.broadcasted_iota(jnp.int32, (GD_CHUNK, GD_CHUNK), 0)
    c = lax.broadcasted_iota(jnp.int32, (GD_CHUNK, GD_CHUNK), 1)
    sls = [slice(ci * GD_CHUNK, (ci + 1) * GD_CHUNK) for ci in range(rows // GD_CHUNK)]
    qs = [q_ref[sl, :] for sl in sls]
    ks = [k_ref[sl, :] for sl in sls]
    gcs = [gc_all[sl] for sl in sls]
    kbs = [k * beta_all[sl] for k, sl in zip(ks, sls)]
    kks = [_bdot(kb, k, _NT) for kb, k in zip(kbs, ks)]
    qks = [_bdot(q, k, _NT) for q, k in zip(qs, ks)]
    decays = []
    for gc in gcs:
        gc_rows = jnp.broadcast_to(gc, (GD_CHUNK, LANES))
        gc_cols = jnp.transpose(gc_rows)[0:1, :GD_CHUNK]
        decays.append(jnp.exp(jnp.where(c <= r, gc_rows[:, :GD_CHUNK] - gc_cols, 0.0)))
    tmats = _unit_lower_inverse([jnp.where(c < r, kk * d, 0.0) for kk, d in zip(kks, decays)], r, c)
    egcs = [jnp.exp(gc) for gc in gcs]
    uws = [_bdot(t, jnp.concatenate([v_ref[sl, :] * beta_all[sl], kb * egc], axis=1))
           for t, sl, kb, egc in zip(tmats, sls, kbs, egcs)]
    for sl, q, k, gc, egc, qk, d, uw in zip(sls, qs, ks, gcs, egcs, qks, decays, uws):
        u_ref[sl, :] = uw[:, :LANES]
        w_ref[sl, :] = uw[:, LANES:].astype(w_ref.dtype)
        intra = jnp.where(c <= r, qk * d, 0.0)
        a_ref[sl, :] = jnp.concatenate([intra, jnp.zeros_like(intra)], axis=1).astype(a_ref.dtype)
        qg_ref[sl, :] = (q * egc).astype(qg_ref.dtype)
        g_last = gc[GD_CHUNK - 1:GD_CHUNK, :]
        kd_ref[sl, :] = (k * jnp.exp(g_last - gc)).astype(kd_ref.dtype)


def gd_local(qkv, gate, batch, seq, *, rows=2048):
    n = qkv.shape[0]
    rows = _tile(seq, rows)
    nr = seq // rows
    spec = lambda off: pl.BlockSpec((rows, LANES), lambda b, h, i: (b * nr + i, off + h))
    out_spec = pl.BlockSpec((rows, LANES), lambda b, h, i: (b * nr + i, h))
    bf = jax.ShapeDtypeStruct((n, HEAD_W), jnp.bfloat16)
    return pl.pallas_call(
        functools.partial(_gd_local_kernel, rows=rows),
        grid=(batch, GD_HEADS, nr),
        in_specs=[spec(0), spec(GD_HEADS), spec(2 * GD_HEADS),
                  pl.BlockSpec((rows, LANES), lambda b, h, i: (b * nr + i, 0))],
        out_specs=[out_spec] * 5,
        out_shape=[jax.ShapeDtypeStruct((n, HEAD_W), jnp.float32), bf, bf, bf, bf],
        compiler_params=_cparams("parallel", "parallel", "parallel"),
        name="gd_local",
    )(qkv, qkv, qkv, gate)


def _gd_scan_kernel(u_ref, w_ref, qg_ref, kd_ref, a_ref, gate_ref, z_ref, g_ref, o_ref, state_ref, *, rows, nb):
    @pl.when(pl.program_id(1) == 0)
    def _():
        state_ref[...] = jnp.zeros(state_ref.shape, jnp.float32)

    gain = g_ref[...]
    chains = [(b, h, slice(h * LANES, (h + 1) * LANES)) for b in range(nb) for h in range(GD_HEADS)]

    def chunk(ci, carry):
        start = pl.multiple_of(ci * GD_CHUNK, GD_CHUNK)
        sl = pl.ds(start, GD_CHUNK)
        states = [state_ref[b * GD_HEADS + h] for b, h, _ in chains]
        sbs = [s.astype(jnp.bfloat16) for s in states]
        wq = [jnp.dot(jnp.concatenate([w_ref[b, sl, cs], qg_ref[b, sl, cs]], axis=0), sb,
                      preferred_element_type=jnp.float32) for (b, _, cs), sb in zip(chains, sbs)]
        vbs = [(u_ref[b, sl, cs] - x[:GD_CHUNK]).astype(jnp.bfloat16) for (b, _, cs), x in zip(chains, wq)]
        decay = [jnp.exp(gate_ref[b, pl.ds(start + GD_CHUNK - 1, 1), :]) for b in range(nb)]
        for (b, h, cs), state, vb in zip(chains, states, vbs):
            state_ref[b * GD_HEADS + h] = state * _lane_col(decay[b], h) + lax.dot_general(
                kd_ref[b, sl, cs], vb, _TN, preferred_element_type=jnp.float32)
        outs = [x[GD_CHUNK:] + jnp.dot(a_ref[b, sl, cs][:, :GD_CHUNK], vb, preferred_element_type=jnp.float32)
                for (b, _, cs), x, vb in zip(chains, wq, vbs)]
        for (b, _, cs), out in zip(chains, outs):
            o_ref[b, sl, cs] = (_rms(out, gain) * _silu(z_ref[b, sl, cs].astype(jnp.float32))).astype(o_ref.dtype)
        return carry

    lax.fori_loop(0, rows // GD_CHUNK, chunk, 0)


def gd_scan(u, w, qg, kd, amat, gate, proj, norm_g, batch, seq, *, rows=512, nb=4):
    rows = _tile(seq, rows)
    nb = _tile(batch, nb)
    per_seq = lambda t: t.reshape(batch, seq, t.shape[-1])
    spec = pl.BlockSpec((nb, rows, HEAD_W), lambda b, i: (b, i, 0))
    out = pl.pallas_call(
        functools.partial(_gd_scan_kernel, rows=rows, nb=nb),
        grid=(batch // nb, seq // rows),
        in_specs=[spec, spec, spec, spec, spec,
                  pl.BlockSpec((nb, rows, LANES), lambda b, i: (b, i, 0)),
                  pl.BlockSpec((nb, rows, HEAD_W), lambda b, i: (b, i, COL_GD_Z // GD_HEADS)),
                  pl.BlockSpec((1, LANES), lambda b, i: (0, 0))],
        out_specs=spec,
        out_shape=jax.ShapeDtypeStruct((batch, seq, HEAD_W), jnp.bfloat16),
        scratch_shapes=[pltpu.VMEM((nb * GD_HEADS, GD_DIM, GD_DIM), jnp.float32)],
        compiler_params=_cparams("parallel", "arbitrary"),
        name="gd_scan",
    )(per_seq(u), per_seq(w), per_seq(qg), per_seq(kd), per_seq(amat), per_seq(gate), per_seq(proj),
      norm_g.reshape(1, LANES))
    return out.reshape(batch * seq, HEAD_W)


def _merge_kernel(x_ref, oa_ref, ob_ref, oc_ref, ga_ref, gb_ref, gc_ref,
                  wa_ref, wb_ref, wc_ref, wo_ref, o_ref):
    def branch(o, g, w):
        return jax.nn.sigmoid(g[...].astype(jnp.float32)) * jnp.dot(
            o[...], w[...], preferred_element_type=jnp.float32)

    merged = (branch(oa_ref, ga_ref, wa_ref) + branch(ob_ref, gb_ref, wb_ref)
              + branch(oc_ref, gc_ref, wc_ref))
    o_ref[...] = x_ref[...] + jnp.dot(merged.astype(jnp.bfloat16), wo_ref[...],
                                      preferred_element_type=jnp.float32)


def merge_branches(x, o_a, o_b, o_c, proj, w_a, w_b, w_c, w_out, *, tm=512):
    n, d = x.shape
    tm = _tile(n, tm)
    gate0 = MIX_W // d
    row = lambda width: pl.BlockSpec((tm, width), lambda i: (i, 0))
    full = lambda w: pl.BlockSpec(w.shape, lambda i: (0, 0))
    return pl.pallas_call(
        _merge_kernel,
        grid=(n // tm,),
        in_specs=[row(d), row(HEAD_W), row(HEAD_W), row(HEAD_W),
                  pl.BlockSpec((tm, d), lambda i: (i, gate0)),
                  pl.BlockSpec((tm, d), lambda i: (i, gate0 + 1)),
                  pl.BlockSpec((tm, d), lambda i: (i, gate0 + 2)),
                  full(w_a), full(w_b), full(w_c), full(w_out)],
        out_specs=row(d),
        out_shape=jax.ShapeDtypeStruct((n, d), jnp.float32),
        compiler_params=_cparams("parallel"),
        name="merge_branches",
    )(x, o_a, o_b, o_c, proj, proj, proj, w_a, w_b, w_c, w_out)


def _cross_attn_kernel(x_ref, g_ref, wq_ref, kv_ref, wo_ref, o_ref):
    x = x_ref[...]
    q = jnp.dot(_rms(x, g_ref[...]).astype(jnp.bfloat16), wq_ref[...],
                preferred_element_type=jnp.float32) * (X_DIM ** -0.5 * LOG2E)
    heads = []
    width = X_HEADS * X_DIM
    ones = jnp.ones((kv_ref.shape[0], LANES), jnp.bfloat16)
    for h in range(X_HEADS):
        sl = slice(h * X_DIM, (h + 1) * X_DIM)
        k = kv_ref[:, sl]
        v = kv_ref[:, width + h * X_DIM: width + (h + 1) * X_DIM]
        s = _bdot(q[:, sl], k, _NT)
        p = jnp.exp2(s - jnp.max(s, axis=1, keepdims=True))
        pv = jnp.dot(p.astype(jnp.bfloat16), jnp.concatenate([v, ones], axis=1),
                     preferred_element_type=jnp.float32)
        heads.append(pv[:, :LANES] / pv[:, LANES:])
    o = jnp.concatenate(heads, axis=1).astype(jnp.bfloat16)
    o_ref[...] = x + jnp.dot(o, wo_ref[...], preferred_element_type=jnp.float32)


def cross_attention(x, g, w_q, kv, w_o, batch, seq, n_mem, *, tm=512):
    n, d = x.shape
    tm = _tile(seq, tm)
    per_seq = seq // tm
    full = lambda w: pl.BlockSpec(w.shape, lambda b, i: (0, 0))
    return pl.pallas_call(
        _cross_attn_kernel,
        grid=(batch, per_seq),
        in_specs=[pl.BlockSpec((tm, d), lambda b, i: (b * per_seq + i, 0)),
                  pl.BlockSpec((1, d), lambda b, i: (0, 0)),
                  full(w_q),
                  pl.BlockSpec((n_mem, kv.shape[1]), lambda b, i: (b, 0)),
                  full(w_o)],
        out_specs=pl.BlockSpec((tm, d), lambda b, i: (b * per_seq + i, 0)),
        out_shape=jax.ShapeDtypeStruct((n, d), jnp.float32),
        compiler_params=_cparams("parallel", "parallel"),
        name="cross_attention",
    )(x, g.reshape(1, d), w_q, kv, w_o)


def _swiglu_step(h, wg_ref, wu_ref, wd_ref):
    act = (_silu(jnp.dot(h, wg_ref[...], preferred_element_type=jnp.float32))
           * jnp.dot(h, wu_ref[...], preferred_element_type=jnp.float32))
    return jnp.dot(act.astype(jnp.bfloat16), wd_ref[...], preferred_element_type=jnp.float32)


def _ffn_kernel(x_ref, g_ref, wg_ref, wu_ref, wd_ref, gf_ref, o_ref, h_ref, acc_ref, *, final_norm):
    f = pl.program_id(1)

    @pl.when(f == 0)
    def _():
        x = x_ref[...]
        h = _rms(x, g_ref[...]).astype(h_ref.dtype)
        h_ref[...] = h
        acc_ref[...] = x + _swiglu_step(h, wg_ref, wu_ref, wd_ref)

    @pl.when(f != 0)
    def _():
        acc_ref[...] += _swiglu_step(h_ref[...], wg_ref, wu_ref, wd_ref)

    @pl.when(f == pl.num_programs(1) - 1)
    def _():
        out = acc_ref[...]
        if final_norm:
            out = _rms(out, gf_ref[...])
        o_ref[...] = out


def swiglu_ffn(x, g, w_gate, w_up, w_down, final_gain, *, final_norm, tm=1024, tf=512):
    n, d = x.shape
    ff = w_gate.shape[1]
    tm = _tile(n, tm)
    tf = tf if ff % tf == 0 else ff // 2
    return pl.pallas_call(
        functools.partial(_ffn_kernel, final_norm=final_norm),
        grid=(n // tm, ff // tf),
        in_specs=[pl.BlockSpec((tm, d), lambda i, f: (i, 0)),
                  pl.BlockSpec((1, d), lambda i, f: (0, 0)),
                  pl.BlockSpec((d, tf), lambda i, f: (0, f)),
                  pl.BlockSpec((d, tf), lambda i, f: (0, f)),
                  pl.BlockSpec((tf, d), lambda i, f: (f, 0)),
                  pl.BlockSpec((1, d), lambda i, f: (0, 0))],
        out_specs=pl.BlockSpec((tm, d), lambda i, f: (i, 0)),
        out_shape=jax.ShapeDtypeStruct((n, d), jnp.float32),
        scratch_shapes=[pltpu.VMEM((tm, d), jnp.bfloat16), pltpu.VMEM((tm, d), jnp.float32)],
        compiler_params=_cparams("parallel", "arbitrary"),
        name="swiglu_ffn",
    )(x, g.reshape(1, d), w_gate, w_up, w_down, final_gain.reshape(1, d))


MOE_TILE_ROWS = 1024
ROUTE_ROWS = 512


def _lane_pick(lane, values):
    out = jnp.zeros(lane.shape, values[0].dtype)
    for k, v in enumerate(values):
        out = jnp.where(lane == k, v, out)
    return out


def _route_kernel(x_ref, g_ref, w_ref, wts_ref, idx_ref, cnt_ref, carry_ref, *, tm):
    @pl.when(pl.program_id(0) == 0)
    def _():
        carry_ref[...] = jnp.zeros(carry_ref.shape, jnp.float32)

    logits = jnp.dot(_rms(x_ref[...], g_ref[...]), w_ref[...],
                     preferred_element_type=jnp.float32, precision=lax.Precision.HIGHEST)
    lane = lax.broadcasted_iota(jnp.int32, logits.shape, 1)
    logits = jnp.where(lane < N_EXPERTS, logits, NEG_BIG)
    m1 = jnp.max(logits, axis=1, keepdims=True)
    i1 = jnp.min(jnp.where(logits == m1, lane, LANES), axis=1, keepdims=True)
    rest = jnp.where(lane == i1, NEG_BIG, logits)
    m2 = jnp.max(rest, axis=1, keepdims=True)
    i2 = jnp.min(jnp.where(rest == m2, lane, LANES), axis=1, keepdims=True)
    e2 = jnp.exp(m2 - m1)
    w1 = 1.0 / (1.0 + e2)
    hot1, hot2 = lane == i1, lane == i2
    both = jnp.logical_or(hot1, hot2).astype(jnp.float32)
    r = lax.broadcasted_iota(jnp.int32, (tm, tm), 0)
    c = lax.broadcasted_iota(jnp.int32, (tm, tm), 1)
    earlier = (c < r).astype(jnp.bfloat16)
    prior = carry_ref[...] + jnp.dot(earlier, both.astype(jnp.bfloat16), preferred_element_type=jnp.float32)
    rank1 = jnp.sum(jnp.where(hot1, prior, 0.0), axis=1, keepdims=True)
    rank2 = jnp.sum(jnp.where(hot2, prior, 0.0), axis=1, keepdims=True)
    carry_ref[...] += jnp.sum(both, axis=0, keepdims=True)
    cnt_ref[...] = jnp.broadcast_to(carry_ref[...], cnt_ref.shape)
    wts_ref[...] = _lane_pick(lane, [w1, e2 * w1])
    idx_ref[...] = _lane_pick(lane, [i1, i2, rank1.astype(jnp.int32), rank2.astype(jnp.int32)])


def moe_route(x, g, w_router):
    n, d = x.shape
    tm = _tile(n, ROUTE_ROWS)
    w = jnp.pad(w_router, ((0, 0), (0, LANES - N_EXPERTS)))
    row = pl.BlockSpec((tm, LANES), lambda i: (i, 0))
    return pl.pallas_call(
        functools.partial(_route_kernel, tm=tm),
        grid=(n // tm,),
        in_specs=[pl.BlockSpec((tm, d), lambda i: (i, 0)),
                  pl.BlockSpec((1, d), lambda i: (0, 0)),
                  pl.BlockSpec((d, LANES), lambda i: (0, 0))],
        out_specs=[row, row, pl.BlockSpec((8, LANES), lambda i: (0, 0))],
        out_shape=[jax.ShapeDtypeStruct((n, LANES), jnp.float32),
                   jax.ShapeDtypeStruct((n, LANES), jnp.int32),
                   jax.ShapeDtypeStruct((8, LANES), jnp.float32)],
        scratch_shapes=[pltpu.VMEM((1, LANES), jnp.float32)],
        compiler_params=_cparams("arbitrary"),
        name="moe_route",
    )(x, g.reshape(1, d), w)


def _dest_kernel(idx_ref, cnt_ref, dest_ref, te_ref, *, tile_rows):
    tiles = jnp.floor((cnt_ref[...] + (tile_rows - 1)) / tile_rows)
    er = lax.broadcasted_iota(jnp.int32, (LANES, LANES), 0)
    ec = lax.broadcasted_iota(jnp.int32, (LANES, LANES), 1)
    first_tile = jnp.dot(tiles.astype(jnp.bfloat16), (er < ec).astype(jnp.bfloat16),
                         preferred_element_type=jnp.float32)
    first_row = first_tile[0:1, :] * tile_rows
    idx = idx_ref[...]
    lane = lax.broadcasted_iota(jnp.int32, idx.shape, 1)
    idx_f = idx.astype(jnp.float32)
    dests = []
    for slot in range(2):
        expert = jnp.sum(jnp.where(lane == slot, idx, 0), axis=1, keepdims=True)
        start = jnp.sum(jnp.where(lane == expert, first_row, 0.0), axis=1, keepdims=True)
        dests.append((start + _lane_col(idx_f, 2 + slot)).astype(jnp.int32))
    dest_ref[...] = _lane_pick(lane, dests)
    last_tile = first_tile + tiles
    tau = lax.broadcasted_iota(jnp.int32, te_ref.shape, 1).astype(jnp.float32)
    owner = jnp.zeros(te_ref.shape, jnp.float32)
    for e in range(N_EXPERTS):
        owner = owner + (_lane_col(last_tile, e) <= tau).astype(jnp.float32)
    te_ref[...] = owner.astype(jnp.int32)


def moe_dest(idx, cnt, n_tiles):
    n = idx.shape[0]
    tm = _tile(n, ROUTE_ROWS)
    te_lanes = -(-n_tiles // LANES) * LANES
    return pl.pallas_call(
        functools.partial(_dest_kernel, tile_rows=MOE_TILE_ROWS),
        grid=(n // tm,),
        in_specs=[pl.BlockSpec((tm, LANES), lambda i: (i, 0)),
                  pl.BlockSpec((8, LANES), lambda i: (0, 0))],
        out_specs=[pl.BlockSpec((tm, LANES), lambda i: (i, 0)),
                   pl.BlockSpec((8, te_lanes), lambda i: (0, 0))],
        out_shape=[jax.ShapeDtypeStruct((n, LANES), jnp.int32),
                   jax.ShapeDtypeStruct((8, te_lanes), jnp.int32)],
        compiler_params=_cparams("arbitrary"),
        name="moe_dest",
    )(idx, cnt)


SUBLANES = 8


def _row_tiles(d):
    return (d // (SUBLANES * LANES) * SUBLANES, LANES)


def _dispatch_kernel(dest_ref, x_ref, g_ref, zeros_ref, xs_ref, hp_ref, sem, *, tm):
    del zeros_ref
    hp_ref[...] = _rms(x_ref[...], g_ref[...]).reshape(hp_ref.shape)

    def issue(r, carry):
        for slot in range(2):
            pltpu.make_async_copy(hp_ref.at[r], xs_ref.at[dest_ref[0, 2 * r + slot]], sem).start()
        return carry

    def drain(r, carry):
        for slot in range(2):
            pltpu.make_async_copy(hp_ref.at[0], xs_ref.at[0], sem).wait()
        return carry

    lax.fori_loop(0, tm, issue, 0, unroll=8)
    lax.fori_loop(0, tm, drain, 0, unroll=8)


def moe_dispatch(x, g, dest_pairs, n_rows):
    n, d = x.shape
    tm = dest_pairs.shape[2] // 2
    row = _row_tiles(d)
    zeros = jnp.zeros((n_rows,) + row, jnp.float32)
    return pl.pallas_call(
        functools.partial(_dispatch_kernel, tm=tm),
        grid=(n // tm,),
        in_specs=[pl.BlockSpec((None, 1, 2 * tm), lambda i: (i, 0, 0), memory_space=pltpu.SMEM),
                  pl.BlockSpec((tm, d), lambda i: (i, 0)),
                  pl.BlockSpec((1, d), lambda i: (0, 0)),
                  pl.BlockSpec(memory_space=pl.ANY)],
        out_specs=pl.BlockSpec(memory_space=pl.ANY),
        out_shape=jax.ShapeDtypeStruct((n_rows,) + row, jnp.float32),
        scratch_shapes=[pltpu.VMEM((tm,) + row, jnp.float32), pltpu.SemaphoreType.DMA(())],
        input_output_aliases={3: 0},
        compiler_params=_cparams("arbitrary"),
        name="moe_dispatch",
    )(dest_pairs, x, g.reshape(1, d), zeros)


def _grouped_ffn_kernel(te_ref, xs_ref, wg_ref, wu_ref, wd_ref, ys_ref, h_ref, acc_ref):
    i, f = pl.program_id(0), pl.program_id(1)
    live = te_ref[i] < N_EXPERTS
    first = f == 0

    @pl.when(jnp.logical_and(first, live))
    def _():
        h = xs_ref[...].reshape(h_ref.shape).astype(h_ref.dtype)
        h_ref[...] = h
        acc_ref[...] = _swiglu_step(h, wg_ref, wu_ref, wd_ref)

    @pl.when(jnp.logical_and(first, jnp.logical_not(live)))
    def _():
        acc_ref[...] = jnp.zeros(acc_ref.shape, jnp.float32)

    @pl.when(jnp.logical_and(jnp.logical_not(first), live))
    def _():
        acc_ref[...] += _swiglu_step(h_ref[...], wg_ref, wu_ref, wd_ref)

    @pl.when(f == pl.num_programs(1) - 1)
    def _():
        ys_ref[...] = acc_ref[...].reshape(ys_ref.shape)


def moe_grouped_ffn(xs, tile_expert, w_gate, w_up, w_down, *, tf=512):
    n_rows = xs.shape[0]
    row = xs.shape[1:]
    d = w_gate.shape[1]
    ff = w_gate.shape[2]
    tm = MOE_TILE_ROWS
    expert = lambda i, te: jnp.minimum(te[i], N_EXPERTS - 1)
    rows_spec = pl.BlockSpec((tm,) + row, lambda i, f, te: (i, 0, 0))
    return pl.pallas_call(
        _grouped_ffn_kernel,
        grid_spec=pltpu.PrefetchScalarGridSpec(
            num_scalar_prefetch=1,
            grid=(n_rows // tm, ff // tf),
            in_specs=[rows_spec,
                      pl.BlockSpec((None, d, tf), lambda i, f, te: (expert(i, te), 0, f)),
                      pl.BlockSpec((None, d, tf), lambda i, f, te: (expert(i, te), 0, f)),
                      pl.BlockSpec((None, tf, d), lambda i, f, te: (expert(i, te), f, 0))],
            out_specs=rows_spec,
            scratch_shapes=[pltpu.VMEM((tm, d), jnp.bfloat16), pltpu.VMEM((tm, d), jnp.float32)]),
        out_shape=jax.ShapeDtypeStruct(xs.shape, jnp.float32),
        compiler_params=_cparams("parallel", "arbitrary"),
        name="moe_grouped_ffn",
    )(tile_expert, xs, w_gate, w_up, w_down)


def _combine_kernel(dest_ref, x_ref, wts_ref, gf_ref, ys_ref, o_ref, y1_ref, y2_ref, sem, *, tm, final_norm):
    bufs = (y1_ref, y2_ref)

    def issue(r, carry):
        for slot in range(2):
            pltpu.make_async_copy(ys_ref.at[dest_ref[0, 2 * r + slot]], bufs[slot].at[r], sem).start()
        return carry

    def drain(r, carry):
        for slot in range(2):
            pltpu.make_async_copy(ys_ref.at[0], bufs[slot].at[0], sem).wait()
        return carry

    lax.fori_loop(0, tm, issue, 0, unroll=8)
    lax.fori_loop(0, tm, drain, 0, unroll=8)
    wts = wts_ref[...]
    out = x_ref[...]
    for slot in range(2):
        out = out + _lane_col(wts, slot) * bufs[slot][...].reshape(out.shape)
    if final_norm:
        out = _rms(out, gf_ref[...])
    o_ref[...] = out


def moe_combine(x, wts, dest_pairs, ys, final_gain, *, final_norm):
    n, d = x.shape
    tm = dest_pairs.shape[2] // 2
    return pl.pallas_call(
        functools.partial(_combine_kernel, tm=tm, final_norm=final_norm),
        grid=(n // tm,),
        in_specs=[pl.BlockSpec((None, 1, 2 * tm), lambda i: (i, 0, 0), memory_space=pltpu.SMEM),
                  pl.BlockSpec((tm, d), lambda i: (i, 0)),
                  pl.BlockSpec((tm, LANES), lambda i: (i, 0)),
                  pl.BlockSpec((1, d), lambda i: (0, 0)),
                  pl.BlockSpec(memory_space=pl.ANY)],
        out_specs=pl.BlockSpec((tm, d), lambda i: (i, 0)),
        out_shape=jax.ShapeDtypeStruct((n, d), jnp.float32),
        scratch_shapes=[pltpu.VMEM((tm,) + ys.shape[1:], jnp.float32),
                        pltpu.VMEM((tm,) + ys.shape[1:], jnp.float32),
                        pltpu.SemaphoreType.DMA(())],
        compiler_params=_cparams("arbitrary"),
        name="moe_combine",
    )(dest_pairs, x, wts, final_gain.reshape(1, d), ys)


def moe_swiglu(x, g, w_router, w_gate, w_up, w_down, final_gain, *, final_norm):
    n = x.shape[0]
    n_tiles = -(-2 * n // MOE_TILE_ROWS) + N_EXPERTS
    wts, idx, cnt = moe_route(x, g, w_router)
    dest, tile_expert = moe_dest(idx, cnt, n_tiles)
    tm = _tile(n, ROUTE_ROWS)
    dest_pairs = dest[:, :2].reshape(n // tm, 1, 2 * tm)
    xs = moe_dispatch(x, g, dest_pairs, n_tiles * MOE_TILE_ROWS)
    ys = moe_grouped_ffn(xs, tile_expert[0, :n_tiles], w_gate, w_up, w_down)
    return moe_combine(x, wts, dest_pairs, ys, final_gain, final_norm=final_norm)


def _rope_tables(seq):
    half = DA_DIM // 2
    inv_freq = 1.0 / (ROPE_THETA ** (jnp.arange(0, DA_DIM, 2, dtype=jnp.float32) / DA_DIM))
    ang = jnp.arange(seq, dtype=jnp.float32)[:, None] * inv_freq[None, :]
    cos, sin = jnp.cos(ang), jnp.sin(ang)
    reps = LANES // half
    sign = jnp.tile(jnp.concatenate([-jnp.ones((half,)), jnp.ones((half,))]), reps // 2)
    return jnp.tile(cos, (1, reps)), jnp.tile(sin, (1, reps)) * sign[None, :]


def kernel(x, mem, norm_mix, w_in, gd_conv, lam_q1, lam_k1, lam_q2, lam_k2, da_subln, gd_a_log,
           gd_dt_bias, gd_norm, w_br_a, w_br_b, w_br_c, w_out, norm_x, norm_mem, w_q_x, w_kv_x,
           w_o_x, norm_ffn, w_gate_dense, w_up_dense, w_down_dense, w_router, w_gate_exp,
           w_up_exp, w_down_exp, norm_final):
    batch, seq, d = x.shape
    n_mem = mem.shape[1]
    depth = norm_mix.shape[0]
    bf = jnp.bfloat16
    n = batch * seq
    cos, sin = _rope_tables(seq)
    xf = x.reshape(n, d)
    memf = mem.reshape(batch * n_mem, d)
    ab0 = MIX_W
    ab1 = MIX_W + 2 * GD_HEADS

    for l in range(depth):
        lam_init = 0.8 - 0.6 * math.exp(-0.3 * l)
        w_l = w_in[l]
        w_main = jnp.concatenate([w_l[:, :ab0], w_l[:, ab1:]], axis=1).astype(bf)
        w_ab = jnp.pad(w_l[:, ab0:ab1], ((0, 0), (0, LANES - 2 * GD_HEADS))).astype(bf)
        proj, gate = mixer_in_proj(xf, norm_mix[l], w_main, w_ab, cos, sin, gd_a_log[l], gd_dt_bias[l], seq)

        lam_params = jnp.stack([lam_q1[l], lam_k1[l], lam_q2[l], lam_k2[l]])
        o_a = diff_attention(proj, lam_params, da_subln[l], batch, seq, lam_init)
        o_b = stick_breaking_attention(proj, batch, seq)
        qkv = gd_conv_silu(proj, gd_conv[l], batch, seq)
        u, w, qg, kd, amat = gd_local(qkv, gate, batch, seq)
        o_c = gd_scan(u, w, qg, kd, amat, gate, proj, gd_norm[l], batch, seq)
        xf = merge_branches(xf, o_a, o_b, o_c, proj, w_br_a[l].astype(bf), w_br_b[l].astype(bf),
                            w_br_c[l].astype(bf), w_out[l].astype(bf))

        kv = rms_matmul(memf, norm_mem[l], w_kv_x[l].astype(bf), bf)
        xf = cross_attention(xf, norm_x[l], w_q_x[l].astype(bf), kv, w_o_x[l].astype(bf),
                             batch, seq, n_mem)

        last = l == depth - 1
        i = l // 2
        if l % 2 == 0:
            xf = swiglu_ffn(xf, norm_ffn[l], w_gate_dense[i].astype(bf), w_up_dense[i].astype(bf),
                            w_down_dense[i].astype(bf), norm_final, final_norm=last)
        else:
            xf = moe_swiglu(xf, norm_ffn[l], w_router[i], w_gate_exp[i].astype(bf), w_up_exp[i].astype(bf),
                            w_down_exp[i].astype(bf), norm_final, final_norm=last)
    return xf.reshape(batch, seq, d)
```

```python
import functools
import math

import jax
import jax.numpy as jnp
from jax import lax
from jax.experimental import pallas as pl
from jax.experimental.pallas import tpu as pltpu

EPS = 1e-6
LOG2E = math.log2(math.e)
ROPE_THETA = 10000.0
LANES = 128

DA_HEADS = 4
DA_DIM = 64
SB_HEADS = 4
SB_DIM = 128
GD_HEADS = 4
GD_DIM = 128
GD_CONV = 4
GD_CHUNK = 64
X_HEADS = 4
X_DIM = 128
N_EXPERTS = 8
HEAD_W = 512

COL_DA_Q, COL_DA_K, COL_DA_V = 0, 4, 8
COL_SB_Q, COL_SB_K, COL_SB_V = 12, 16, 20
COL_GD_Q, COL_GD_Z = 24, 36
MIX_W = 5120

SB_LOG2_FLOOR = -110.0 * LOG2E
NEG_BIG = -1e30
VMEM_LIMIT = 48 * 1024 * 1024

_NT = (((1,), (1,)), ((), ()))
_TN = (((0,), (0,)), ((), ()))


def _cparams(*sem):
    return pltpu.CompilerParams(dimension_semantics=sem, vmem_limit_bytes=VMEM_LIMIT)


def _tile(n, pref):
    t = min(n, pref)
    while n % t:
        t //= 2
    return t


def _bdot(a, b, dims=None):
    a = a.astype(jnp.bfloat16)
    b = b.astype(jnp.bfloat16)
    if dims is None:
        return jnp.dot(a, b, preferred_element_type=jnp.float32)
    return lax.dot_general(a, b, dims, preferred_element_type=jnp.float32)


def _dot_split(a, b_exact):
    hi = a.astype(jnp.bfloat16)
    lo = (a - hi.astype(jnp.float32)).astype(jnp.bfloat16)
    return (jnp.dot(hi, b_exact, preferred_element_type=jnp.float32)
            + jnp.dot(lo, b_exact, preferred_element_type=jnp.float32))


def _rms(x, g):
    return x * lax.rsqrt(jnp.mean(x * x, axis=-1, keepdims=True) + EPS) * g


def _softplus(z):
    return jnp.maximum(z, 0.0) + jnp.log1p(jnp.exp(-jnp.abs(z)))


def _silu(x):
    return x * jax.nn.sigmoid(x)


def _lane_tile(x, reps):
    return jnp.concatenate([x] * reps, axis=1)


def _lane_col(x, lane):
    ids = lax.broadcasted_iota(jnp.int32, x.shape, 1)
    return jnp.sum(jnp.where(ids == lane, x, 0.0), axis=1, keepdims=True)


def _rms_matmul_kernel(x_ref, g_ref, w_ref, o_ref, h_ref):
    first = pl.program_id(1) == 0

    @pl.when(first)
    def _():
        h = _rms(x_ref[...], g_ref[...]).astype(h_ref.dtype)
        h_ref[...] = h
        o_ref[...] = jnp.dot(h, w_ref[...], preferred_element_type=jnp.float32).astype(o_ref.dtype)

    @pl.when(jnp.logical_not(first))
    def _():
        o_ref[...] = jnp.dot(h_ref[...], w_ref[...], preferred_element_type=jnp.float32).astype(o_ref.dtype)


def rms_matmul(x, g, w, out_dtype, *, tm=1024, tn=1024):
    n, d = x.shape
    nout = w.shape[1]
    tm, tn = _tile(n, tm), _tile(nout, tn)
    return pl.pallas_call(
        _rms_matmul_kernel,
        grid=(n // tm, nout // tn),
        in_specs=[pl.BlockSpec((tm, d), lambda i, j: (i, 0)),
                  pl.BlockSpec((1, d), lambda i, j: (0, 0)),
                  pl.BlockSpec((d, tn), lambda i, j: (0, j))],
        out_specs=pl.BlockSpec((tm, tn), lambda i, j: (i, j)),
        out_shape=jax.ShapeDtypeStruct((n, nout), out_dtype),
        scratch_shapes=[pltpu.VMEM((tm, d), w.dtype)],
        compiler_params=_cparams("parallel", "arbitrary"),
        name="rms_matmul",
    )(x, g.reshape(1, d), w)


def _gd_gates(ab, a_log, dt_bias):
    is_decay_lane = lax.broadcasted_iota(jnp.int32, (GD_CHUNK, LANES), 1) < GD_HEADS
    g = -jnp.exp(a_log) * _softplus(ab + dt_bias)
    beta = jax.nn.sigmoid(ab)
    r = lax.broadcasted_iota(jnp.int32, (GD_CHUNK, GD_CHUNK), 0)
    c = lax.broadcasted_iota(jnp.int32, (GD_CHUNK, GD_CHUNK), 1)
    tril = (c <= r).astype(jnp.bfloat16)
    chunks = []
    for ci in range(ab.shape[0] // GD_CHUNK):
        sl = slice(ci * GD_CHUNK, (ci + 1) * GD_CHUNK)
        hi = g[sl].astype(jnp.bfloat16)
        mid_f = g[sl] - hi.astype(jnp.float32)
        mid = mid_f.astype(jnp.bfloat16)
        lo = (mid_f - mid.astype(jnp.float32)).astype(jnp.bfloat16)
        gc = (jnp.dot(tril, hi, preferred_element_type=jnp.float32)
              + jnp.dot(tril, mid, preferred_element_type=jnp.float32)
              + jnp.dot(tril, lo, preferred_element_type=jnp.float32))
        chunks.append(jnp.where(is_decay_lane, gc, beta[sl]))
    return jnp.concatenate(chunks, axis=0)


def _mixer_proj_kernel(x_ref, g_ref, w_ref, wab_ref, cos_ref, sin_ref, alog_ref, dtb_ref,
                       o_ref, gate_ref, h_ref):
    first = pl.program_id(1) == 0

    @pl.when(first)
    def _():
        h = _rms(x_ref[...], g_ref[...]).astype(h_ref.dtype)
        h_ref[...] = h
        acc = jnp.dot(h, w_ref[...], preferred_element_type=jnp.float32)
        cos, sin = cos_ref[...], sin_ref[...]
        lane = lax.broadcasted_iota(jnp.int32, cos.shape, 1)
        first_half = (lane % DA_DIM) < (DA_DIM // 2)
        for blk in range(2 * DA_HEADS):
            sl = slice(blk * LANES, (blk + 1) * LANES)
            t = acc[:, sl]
            partner = jnp.where(first_half, pltpu.roll(t, LANES - DA_DIM // 2, 1),
                                pltpu.roll(t, DA_DIM // 2, 1))
            rot = t * cos + partner * sin
            if blk < DA_HEADS:
                rot = rot * (DA_DIM ** -0.5 * LOG2E)
            o_ref[:, sl] = rot.astype(o_ref.dtype)
        ab = jnp.dot(h, wab_ref[...], preferred_element_type=jnp.float32)
        gate_ref[...] = _gd_gates(ab, alog_ref[...], dtb_ref[...])

    @pl.when(jnp.logical_not(first))
    def _():
        o_ref[...] = jnp.dot(h_ref[...], w_ref[...], preferred_element_type=jnp.float32).astype(o_ref.dtype)


def mixer_in_proj(x, g, w_main, w_ab, cos, sin, a_log, dt_bias, seq):
    n, d = x.shape
    nout = w_main.shape[1]
    tm = _tile(seq, 1024)
    tn = 2 * DA_HEADS * LANES
    per_seq = seq // tm
    pad = jnp.zeros((LANES - GD_HEADS,), jnp.float32)
    alog = jnp.concatenate([a_log, pad]).reshape(1, LANES)
    dtb = jnp.concatenate([dt_bias, pad]).reshape(1, LANES)
    vec = lambda width: pl.BlockSpec((1, width), lambda i, j: (0, 0))
    table = pl.BlockSpec((tm, LANES), lambda i, j: (i % per_seq, 0))
    return pl.pallas_call(
        _mixer_proj_kernel,
        grid=(n // tm, nout // tn),
        in_specs=[pl.BlockSpec((tm, d), lambda i, j: (i, 0)), vec(d),
                  pl.BlockSpec((d, tn), lambda i, j: (0, j)),
                  pl.BlockSpec((d, LANES), lambda i, j: (0, 0)),
                  table, table, vec(LANES), vec(LANES)],
        out_specs=[pl.BlockSpec((tm, tn), lambda i, j: (i, j)),
                   pl.BlockSpec((tm, LANES), lambda i, j: (i, 0))],
        out_shape=[jax.ShapeDtypeStruct((n, nout), jnp.bfloat16),
                   jax.ShapeDtypeStruct((n, LANES), jnp.float32)],
        scratch_shapes=[pltpu.VMEM((tm, d), jnp.bfloat16)],
        compiler_params=_cparams("parallel", "arbitrary"),
        name="mixer_in_proj",
    )(x, g.reshape(1, d), w_main, w_ab, cos, sin, alog, dtb)


def _diff_attn_kernel(lam_ref, q_ref, k_ref, v_ref, subln_ref, o_ref,
                      qs_ref, sa_ref, sb_ref, ma_ref, mb_ref, m_ref, acc_ref, *, t, lam_init):
    i = pl.program_id(2)
    q = q_ref[...]
    lane = lax.broadcasted_iota(jnp.int32, q.shape, 1)
    zero = jnp.zeros_like(q)
    qs_ref[:t, :] = jnp.where(lane < DA_DIM, q, zero)
    qs_ref[t:, :] = jnp.where(lane >= DA_DIM, q, zero)
    m_ref[...] = jnp.full(m_ref.shape, NEG_BIG, jnp.float32)
    acc_ref[...] = jnp.zeros(acc_ref.shape, jnp.float32)
    reps = t // LANES
    ones = jnp.ones((t, LANES), jnp.bfloat16)

    def scores(j, s_ref, mx_ref, masked=False):
        k = k_ref[pl.ds(pl.multiple_of(jnp.maximum(j, 0) * t, t), t), :]
        s = lax.dot_general(qs_ref[...], k, _NT, preferred_element_type=jnp.float32)
        if masked:
            row = lax.broadcasted_iota(jnp.int32, s.shape, 0) % t
            col = lax.broadcasted_iota(jnp.int32, s.shape, 1)
            s = jnp.where(col <= row, s, NEG_BIG)
        s_ref[...] = s
        mx_ref[...] = jnp.broadcast_to(jnp.max(s, axis=1, keepdims=True), mx_ref.shape)

    def consume(j, s_ref, mx_ref):
        m_old = m_ref[...]
        m_new = jnp.maximum(m_old, mx_ref[...])
        alpha = jnp.exp2(m_old - m_new)
        p = jnp.exp2(s_ref[...] - _lane_tile(m_new, reps))
        v1 = jnp.concatenate([v_ref[pl.ds(pl.multiple_of(j * t, t), t), :], ones], axis=1)
        acc_ref[...] = _lane_tile(alpha, 2) * acc_ref[...] + jnp.dot(
            p.astype(jnp.bfloat16), v1, preferred_element_type=jnp.float32)
        m_ref[...] = m_new

    scores(i, sa_ref, ma_ref, masked=True)
    scores(i - 1, sb_ref, mb_ref)
    consume(i, sa_ref, ma_ref)

    def earlier_pair(n, carry):
        j = i - 1 - 2 * n
        scores(j - 1, sa_ref, ma_ref)
        consume(j, sb_ref, mb_ref)
        scores(j - 2, sb_ref, mb_ref)
        consume(j - 1, sa_ref, ma_ref)
        return carry

    lax.fori_loop(0, i // 2, earlier_pair, 0)

    @pl.when(i % 2 == 1)
    def _():
        consume(0, sb_ref, mb_ref)

    lq1, lk1, lq2, lk2 = (lam_ref[r:r + 1, :] for r in range(4))
    lam = (jnp.exp(jnp.sum(lq1 * lk1, axis=1, keepdims=True))
           - jnp.exp(jnp.sum(lq2 * lk2, axis=1, keepdims=True)) + lam_init)
    acc = acc_ref[:, :LANES] / acc_ref[:, LANES:]
    o = acc[:t] - lam * acc[t:]
    o_ref[...] = (_rms(o, subln_ref[...]) * (1.0 - lam_init)).astype(o_ref.dtype)


def diff_attention(proj, lam_params, subln, batch, seq, lam_init, *, t=512):
    n = proj.shape[0]
    t = _tile(seq, t)
    nq = seq // t
    return pl.pallas_call(
        functools.partial(_diff_attn_kernel, t=t, lam_init=lam_init),
        grid=(batch, DA_HEADS, nq),
        in_specs=[pl.BlockSpec((4, DA_DIM), lambda b, h, i: (0, 0)),
                  pl.BlockSpec((t, LANES), lambda b, h, i: (b * nq + i, COL_DA_Q + h)),
                  pl.BlockSpec((seq, LANES), lambda b, h, i: (b, COL_DA_K + h)),
                  pl.BlockSpec((seq, LANES), lambda b, h, i: (b, COL_DA_V + h)),
                  pl.BlockSpec((1, LANES), lambda b, h, i: (0, 0))],
        out_specs=pl.BlockSpec((t, LANES), lambda b, h, i: (b * nq + i, h)),
        out_shape=jax.ShapeDtypeStruct((n, HEAD_W), jnp.bfloat16),
        scratch_shapes=[pltpu.VMEM((2 * t, LANES), jnp.bfloat16),
                        pltpu.VMEM((2 * t, t), jnp.float32),
                        pltpu.VMEM((2 * t, t), jnp.float32),
                        pltpu.VMEM((2 * t, LANES), jnp.float32),
                        pltpu.VMEM((2 * t, LANES), jnp.float32),
                        pltpu.VMEM((2 * t, LANES), jnp.float32),
                        pltpu.VMEM((2 * t, 2 * LANES), jnp.float32)],
        compiler_params=_cparams("parallel", "parallel", "arbitrary"),
        name="diff_attention",
    )(lam_params, proj, proj, proj, subln.reshape(1, LANES))


def _sb_attn_kernel(q_ref, k_ref, v_ref, o_ref, za_ref, zb_ref, carry_ref, acc_ref, *, tq):
    i = pl.program_id(2)
    qs_val = (q_ref[...].astype(jnp.float32) * (SB_DIM ** -0.5 * LOG2E)).astype(jnp.bfloat16)
    carry_ref[...] = jnp.zeros(carry_ref.shape, jnp.float32)
    acc_ref[...] = jnp.zeros(acc_ref.shape, jnp.float32)
    r = lax.broadcasted_iota(jnp.int32, (tq, tq), 0)
    c = lax.broadcasted_iota(jnp.int32, (tq, tq), 1)
    later = (r > c).astype(jnp.bfloat16)
    reps = tq // LANES

    def tile_rows(j):
        return pl.ds(pl.multiple_of(jnp.maximum(j, 0) * tq, tq), tq)

    def logits(j, z_ref):
        z_ref[...] = lax.dot_general(qs_val, k_ref[tile_rows(j), :], _NT, preferred_element_type=jnp.float32)

    def consume(j, z_ref, masked):
        z = z_ref[...]
        sp = jnp.maximum(z, 0.0) + jnp.log2(1.0 + jnp.exp2(-jnp.abs(z)))
        log_fail = jnp.where(c < r, -sp, 0.0) if masked else -sp
        log_remain = _lane_tile(carry_ref[...], reps) + _dot_split(log_fail, later)
        attn = jnp.exp2(z - sp + log_remain)
        if masked:
            attn = jnp.where(c < r, attn, 0.0)
        acc_ref[...] += jnp.dot(attn.astype(jnp.bfloat16), v_ref[tile_rows(j), :],
                                preferred_element_type=jnp.float32)
        carry_ref[...] += jnp.sum(log_fail, axis=1, keepdims=True)

    def consume_if_exists(j, z_ref):
        carry_ref[...] = jnp.where(j < 0, NEG_BIG, carry_ref[...])
        consume(j, z_ref, False)

    logits(i, za_ref)
    logits(i - 1, zb_ref)
    consume(i, za_ref, True)
    logits(i - 2, za_ref)
    consume_if_exists(i - 1, zb_ref)

    def cond(state):
        j, alive = state
        return jnp.logical_and(j >= 0, alive)

    def body(state):
        j, _ = state
        logits(j - 1, zb_ref)
        consume(j, za_ref, False)
        logits(j - 2, za_ref)
        consume_if_exists(j - 1, zb_ref)
        return j - 2, jnp.max(carry_ref[...]) > SB_LOG2_FLOOR

    lax.while_loop(cond, body, (i - 2, jnp.max(carry_ref[...]) > SB_LOG2_FLOOR))
    o_ref[...] = acc_ref[...].astype(o_ref.dtype)


def stick_breaking_attention(proj, batch, seq, *, tq=256):
    n = proj.shape[0]
    tq = _tile(seq, tq)
    nq = seq // tq
    return pl.pallas_call(
        functools.partial(_sb_attn_kernel, tq=tq),
        grid=(batch, SB_HEADS, nq),
        in_specs=[pl.BlockSpec((tq, LANES), lambda b, h, i: (b * nq + i, COL_SB_Q + h)),
                  pl.BlockSpec((seq, LANES), lambda b, h, i: (b, COL_SB_K + h)),
                  pl.BlockSpec((seq, LANES), lambda b, h, i: (b, COL_SB_V + h))],
        out_specs=pl.BlockSpec((tq, LANES), lambda b, h, i: (b * nq + i, h)),
        out_shape=jax.ShapeDtypeStruct((n, HEAD_W), jnp.bfloat16),
        scratch_shapes=[pltpu.VMEM((tq, tq), jnp.float32),
                        pltpu.VMEM((tq, tq), jnp.float32),
                        pltpu.VMEM((tq, LANES), jnp.float32),
                        pltpu.VMEM((tq, LANES), jnp.float32)],
        compiler_params=_cparams("parallel", "parallel", "arbitrary"),
        name="stick_breaking_attention",
    )(proj, proj, proj)


def _gd_conv_kernel(x_ref, w_ref, o_ref):
    c = pl.program_id(1)
    w = w_ref[...]
    is_q = c < GD_HEADS
    is_v = c >= 2 * GD_HEADS

    def conv_norm(x, causal_mask):
        y = x * w[GD_CONV - 1:GD_CONV, :]
        for back in range(1, GD_CONV):
            shifted = pltpu.roll(x, back, 0)
            if causal_mask:
                row = lax.broadcasted_iota(jnp.int32, x.shape, 0)
                shifted = jnp.where(row >= back, shifted, 0.0)
            y = y + shifted * w[GD_CONV - 1 - back:GD_CONV - back, :]
        y = _silu(y)
        inv = lax.rsqrt(jnp.sum(y * y, axis=1, keepdims=True) + EPS)
        scale = jnp.where(is_v, 1.0, inv * jnp.where(is_q, GD_DIM ** -0.5, 1.0))
        return (y * scale).astype(o_ref.dtype)

    o_ref[...] = conv_norm(x_ref[...].astype(jnp.float32), False)
    o_ref[:16, :] = conv_norm(x_ref[:16, :].astype(jnp.float32), True)


def gd_conv_silu(proj, conv_w, batch, seq):
    n = proj.shape[0]
    ncol = 3 * GD_HEADS
    return pl.pallas_call(
        _gd_conv_kernel,
        grid=(batch, ncol),
        in_specs=[pl.BlockSpec((seq, LANES), lambda b, c: (b, COL_GD_Q + c)),
                  pl.BlockSpec((GD_CONV, LANES), lambda b, c: (0, c))],
        out_specs=pl.BlockSpec((seq, LANES), lambda b, c: (b, c)),
        out_shape=jax.ShapeDtypeStruct((n, ncol * LANES), jnp.float32),
        compiler_params=_cparams("parallel", "parallel"),
        name="gd_conv",
    )(proj, conv_w)


def _unit_lower_inverse(lmats, r, c):
    eye = (r == c).astype(jnp.float32)
    diag16 = (r // 16) == (c // 16)
    ps = [jnp.where(diag16, lm, 0.0) for lm in lmats]
    ts = [eye - p for p in ps]
    for _ in range(3):
        ps = [_bdot(p, p) for p in ps]
        ts = [t + _bdot(t, p) for t, p in zip(ts, ps)]
    for blk in (32, 64):
        band = jnp.logical_and((r // blk) == (c // blk), (r // (blk // 2)) != (c // (blk // 2)))
        tcs = [_bdot(t, jnp.where(band, lm, 0.0)) for t, lm in zip(ts, lmats)]
        ts = [t - _bdot(tc, t) for t, tc in zip(ts, tcs)]
    return ts


def _gd_local_kernel(q_ref, k_ref, v_ref, gate_ref, u_ref, w_ref, qg_ref, kd_ref, a_ref, *, rows):
    h = pl.program_id(1)
    gate = gate_ref[...]
    gc_all = _lane_col(gate, h)
    beta_all = _lane_col(gate, GD_HEADS + h)
    r = lax.broadcasted_iota(jnp.int32, (GD_CHUNK, GD_CHUNK), 0)
    c = lax.broadcasted_iota(jnp.int32, (GD_CHUNK, GD_CHUNK), 1)
    sls = [slice(ci * GD_CHUNK, (ci + 1) * GD_CHUNK) for ci in range(rows // GD_CHUNK)]
    qs = [q_ref[sl, :] for sl in sls]
    ks = [k_ref[sl, :] for sl in sls]
    gcs = [gc_all[sl] for sl in sls]
    kbs = [k * beta_all[sl] for k, sl in zip(ks, sls)]
    kks = [_bdot(kb, k, _NT) for kb, k in zip(kbs, ks)]
    qks = [_bdot(q, k, _NT) for q, k in zip(qs, ks)]
    decays = []
    for gc in gcs:
        gc_rows = jnp.broadcast_to(gc, (GD_CHUNK, LANES))
        gc_cols = jnp.transpose(gc_rows)[0:1, :GD_CHUNK]
        decays.append(jnp.exp(jnp.where(c <= r, gc_rows[:, :GD_CHUNK] - gc_cols, 0.0)))
    tmats = _unit_lower_inverse([jnp.where(c < r, kk * d, 0.0) for kk, d in zip(kks, decays)], r, c)
    egcs = [jnp.exp(gc) for gc in gcs]
    uws = [_bdot(t, jnp.concatenate([v_ref[sl, :] * beta_all[sl], kb * egc], axis=1))
           for t, sl, kb, egc in zip(tmats, sls, kbs, egcs)]
    for sl, q, k, gc, egc, qk, d, uw in zip(sls, qs, ks, gcs, egcs, qks, decays, uws):
        u_ref[sl, :] = uw[:, :LANES]
        w_ref[sl, :] = uw[:, LANES:].astype(w_ref.dtype)
        intra = jnp.where(c <= r, qk * d, 0.0)
        a_ref[sl, :] = jnp.concatenate([intra, jnp.zeros_like(intra)], axis=1).astype(a_ref.dtype)
        qg_ref[sl, :] = (q * egc).astype(qg_ref.dtype)
        g_last = gc[GD_CHUNK - 1:GD_CHUNK, :]
        kd_ref[sl, :] = (k * jnp.exp(g_last - gc)).astype(kd_ref.dtype)


def gd_local(qkv, gate, batch, seq, *, rows=2048):
    n = qkv.shape[0]
    rows = _tile(seq, rows)
    nr = seq // rows
    spec = lambda off: pl.BlockSpec((rows, LANES), lambda b, h, i: (b * nr + i, off + h))
    out_spec = pl.BlockSpec((rows, LANES), lambda b, h, i: (b * nr + i, h))
    bf = jax.ShapeDtypeStruct((n, HEAD_W), jnp.bfloat16)
    return pl.pallas_call(
        functools.partial(_gd_local_kernel, rows=rows),
        grid=(batch, GD_HEADS, nr),
        in_specs=[spec(0), spec(GD_HEADS), spec(2 * GD_HEADS),
                  pl.BlockSpec((rows, LANES), lambda b, h, i: (b * nr + i, 0))],
        out_specs=[out_spec] * 5,
        out_shape=[jax.ShapeDtypeStruct((n, HEAD_W), jnp.float32), bf, bf, bf, bf],
        compiler_params=_cparams("parallel", "parallel", "parallel"),
        name="gd_local",
    )(qkv, qkv, qkv, gate)


def _gd_scan_kernel(u_ref, w_ref, qg_ref, kd_ref, a_ref, gate_ref, z_ref, g_ref, o_ref, state_ref, *, rows, nb):
    @pl.when(pl.program_id(1) == 0)
    def _():
        state_ref[...] = jnp.zeros(state_ref.shape, jnp.float32)

    gain = g_ref[...]
    chains = [(b, h, slice(h * LANES, (h + 1) * LANES)) for b in range(nb) for h in range(GD_HEADS)]

    def chunk(ci, carry):
        start = pl.multiple_of(ci * GD_CHUNK, GD_CHUNK)
        sl = pl.ds(start, GD_CHUNK)
        states = [state_ref[b * GD_HEADS + h] for b, h, _ in chains]
        sbs = [s.astype(jnp.bfloat16) for s in states]
        wq = [jnp.dot(jnp.concatenate([w_ref[b, sl, cs], qg_ref[b, sl, cs]], axis=0), sb,
                      preferred_element_type=jnp.float32) for (b, _, cs), sb in zip(chains, sbs)]
        vbs = [(u_ref[b, sl, cs] - x[:GD_CHUNK]).astype(jnp.bfloat16) for (b, _, cs), x in zip(chains, wq)]
        decay = [jnp.exp(gate_ref[b, pl.ds(start + GD_CHUNK - 1, 1), :]) for b in range(nb)]
        for (b, h, cs), state, vb in zip(chains, states, vbs):
            state_ref[b * GD_HEADS + h] = state * _lane_col(decay[b], h) + lax.dot_general(
                kd_ref[b, sl, cs], vb, _TN, preferred_element_type=jnp.float32)
        outs = [x[GD_CHUNK:] + jnp.dot(a_ref[b, sl, cs][:, :GD_CHUNK], vb, preferred_element_type=jnp.float32)
                for (b, _, cs), x, vb in zip(chains, wq, vbs)]
        for (b, _, cs), out in zip(chains, outs):
            o_ref[b, sl, cs] = (_rms(out, gain) * _silu(z_ref[b, sl, cs].astype(jnp.float32))).astype(o_ref.dtype)
        return carry

    lax.fori_loop(0, rows // GD_CHUNK, chunk, 0)


def gd_scan(u, w, qg, kd, amat, gate, proj, norm_g, batch, seq, *, rows=512, nb=4):
    rows = _tile(seq, rows)
    nb = _tile(batch, nb)
    per_seq = lambda t: t.reshape(batch, seq, t.shape[-1])
    spec = pl.BlockSpec((nb, rows, HEAD_W), lambda b, i: (b, i, 0))
    out = pl.pallas_call(
        functools.partial(_gd_scan_kernel, rows=rows, nb=nb),
        grid=(batch // nb, seq // rows),
        in_specs=[spec, spec, spec, spec, spec,
                  pl.BlockSpec((nb, rows, LANES), lambda b, i: (b, i, 0)),
                  pl.BlockSpec((nb, rows, HEAD_W), lambda b, i: (b, i, COL_GD_Z // GD_HEADS)),
                  pl.BlockSpec((1, LANES), lambda b, i: (0, 0))],
        out_specs=spec,
        out_shape=jax.ShapeDtypeStruct((batch, seq, HEAD_W), jnp.bfloat16),
        scratch_shapes=[pltpu.VMEM((nb * GD_HEADS, GD_DIM, GD_DIM), jnp.float32)],
        compiler_params=_cparams("parallel", "arbitrary"),
        name="gd_scan",
    )(per_seq(u), per_seq(w), per_seq(qg), per_seq(kd), per_seq(amat), per_seq(gate), per_seq(proj),
      norm_g.reshape(1, LANES))
    return out.reshape(batch * seq, HEAD_W)


def _merge_kernel(x_ref, oa_ref, ob_ref, oc_ref, ga_ref, gb_ref, gc_ref,
                  wa_ref, wb_ref, wc_ref, wo_ref, o_ref):
    def branch(o, g, w):
        return jax.nn.sigmoid(g[...].astype(jnp.float32)) * jnp.dot(
            o[...], w[...], preferred_element_type=jnp.float32)

    merged = (branch(oa_ref, ga_ref, wa_ref) + branch(ob_ref, gb_ref, wb_ref)
              + branch(oc_ref, gc_ref, wc_ref))
    o_ref[...] = x_ref[...] + jnp.dot(merged.astype(jnp.bfloat16), wo_ref[...],
                                      preferred_element_type=jnp.float32)


def merge_branches(x, o_a, o_b, o_c, proj, w_a, w_b, w_c, w_out, *, tm=512):
    n, d = x.shape
    tm = _tile(n, tm)
    gate0 = MIX_W // d
    row = lambda width: pl.BlockSpec((tm, width), lambda i: (i, 0))
    full = lambda w: pl.BlockSpec(w.shape, lambda i: (0, 0))
    return pl.pallas_call(
        _merge_kernel,
        grid=(n // tm,),
        in_specs=[row(d), row(HEAD_W), row(HEAD_W), row(HEAD_W),
                  pl.BlockSpec((tm, d), lambda i: (i, gate0)),
                  pl.BlockSpec((tm, d), lambda i: (i, gate0 + 1)),
                  pl.BlockSpec((tm, d), lambda i: (i, gate0 + 2)),
                  full(w_a), full(w_b), full(w_c), full(w_out)],
        out_specs=row(d),
        out_shape=jax.ShapeDtypeStruct((n, d), jnp.float32),
        compiler_params=_cparams("parallel"),
        name="merge_branches",
    )(x, o_a, o_b, o_c, proj, proj, proj, w_a, w_b, w_c, w_out)


def _cross_attn_kernel(x_ref, g_ref, wq_ref, kv_ref, wo_ref, o_ref):
    x = x_ref[...]
    q = jnp.dot(_rms(x, g_ref[...]).astype(jnp.bfloat16), wq_ref[...],
                preferred_element_type=jnp.float32) * (X_DIM ** -0.5 * LOG2E)
    heads = []
    width = X_HEADS * X_DIM
    ones = jnp.ones((kv_ref.shape[0], LANES), jnp.bfloat16)
    for h in range(X_HEADS):
        sl = slice(h * X_DIM, (h + 1) * X_DIM)
        k = kv_ref[:, sl]
        v = kv_ref[:, width + h * X_DIM: width + (h + 1) * X_DIM]
        s = _bdot(q[:, sl], k, _NT)
        p = jnp.exp2(s - jnp.max(s, axis=1, keepdims=True))
        pv = jnp.dot(p.astype(jnp.bfloat16), jnp.concatenate([v, ones], axis=1),
                     preferred_element_type=jnp.float32)
        heads.append(pv[:, :LANES] / pv[:, LANES:])
    o = jnp.concatenate(heads, axis=1).astype(jnp.bfloat16)
    o_ref[...] = x + jnp.dot(o, wo_ref[...], preferred_element_type=jnp.float32)


def cross_attention(x, g, w_q, kv, w_o, batch, seq, n_mem, *, tm=512):
    n, d = x.shape
    tm = _tile(seq, tm)
    per_seq = seq // tm
    full = lambda w: pl.BlockSpec(w.shape, lambda b, i: (0, 0))
    return pl.pallas_call(
        _cross_attn_kernel,
        grid=(batch, per_seq),
        in_specs=[pl.BlockSpec((tm, d), lambda b, i: (b * per_seq + i, 0)),
                  pl.BlockSpec((1, d), lambda b, i: (0, 0)),
                  full(w_q),
                  pl.BlockSpec((n_mem, kv.shape[1]), lambda b, i: (b, 0)),
                  full(w_o)],
        out_specs=pl.BlockSpec((tm, d), lambda b, i: (b * per_seq + i, 0)),
        out_shape=jax.ShapeDtypeStruct((n, d), jnp.float32),
        compiler_params=_cparams("parallel", "parallel"),
        name="cross_attention",
    )(x, g.reshape(1, d), w_q, kv, w_o)


def _swiglu_step(h, wg_ref, wu_ref, wd_ref):
    act = (_silu(jnp.dot(h, wg_ref[...], preferred_element_type=jnp.float32))
           * jnp.dot(h, wu_ref[...], preferred_element_type=jnp.float32))
    return jnp.dot(act.astype(jnp.bfloat16), wd_ref[...], preferred_element_type=jnp.float32)


def _ffn_kernel(x_ref, g_ref, wg_ref, wu_ref, wd_ref, gf_ref, o_ref, h_ref, acc_ref, *, final_norm):
    f = pl.program_id(1)

    @pl.when(f == 0)
    def _():
        x = x_ref[...]
        h = _rms(x, g_ref[...]).astype(h_ref.dtype)
        h_ref[...] = h
        acc_ref[...] = x + _swiglu_step(h, wg_ref, wu_ref, wd_ref)

    @pl.when(f != 0)
    def _():
        acc_ref[...] += _swiglu_step(h_ref[...], wg_ref, wu_ref, wd_ref)

    @pl.when(f == pl.num_programs(1) - 1)
    def _():
        out = acc_ref[...]
        if final_norm:
            out = _rms(out, gf_ref[...])
        o_ref[...] = out


def swiglu_ffn(x, g, w_gate, w_up, w_down, final_gain, *, final_norm, tm=1024, tf=512):
    n, d = x.shape
    ff = w_gate.shape[1]
    tm = _tile(n, tm)
    tf = tf if ff % tf == 0 else ff // 2
    return pl.pallas_call(
        functools.partial(_ffn_kernel, final_norm=final_norm),
        grid=(n // tm, ff // tf),
        in_specs=[pl.BlockSpec((tm, d), lambda i, f: (i, 0)),
                  pl.BlockSpec((1, d), lambda i, f: (0, 0)),
                  pl.BlockSpec((d, tf), lambda i, f: (0, f)),
                  pl.BlockSpec((d, tf), lambda i, f: (0, f)),
                  pl.BlockSpec((tf, d), lambda i, f: (f, 0)),
                  pl.BlockSpec((1, d), lambda i, f: (0, 0))],
        out_specs=pl.BlockSpec((tm, d), lambda i, f: (i, 0)),
        out_shape=jax.ShapeDtypeStruct((n, d), jnp.float32),
        scratch_shapes=[pltpu.VMEM((tm, d), jnp.bfloat16), pltpu.VMEM((tm, d), jnp.float32)],
        compiler_params=_cparams("parallel", "arbitrary"),
        name="swiglu_ffn",
    )(x, g.reshape(1, d), w_gate, w_up, w_down, final_gain.reshape(1, d))


MOE_TILE_ROWS = 1024
ROUTE_ROWS = 512


def _lane_pick(lane, values):
    out = jnp.zeros(lane.shape, values[0].dtype)
    for k, v in enumerate(values):
        out = jnp.where(lane == k, v, out)
    return out


def _route_kernel(x_ref, g_ref, w_ref, wts_ref, idx_ref, cnt_ref, carry_ref, *, tm):
    @pl.when(pl.program_id(0) == 0)
    def _():
        carry_ref[...] = jnp.zeros(carry_ref.shape, jnp.float32)

    logits = jnp.dot(_rms(x_ref[...], g_ref[...]), w_ref[...],
                     preferred_element_type=jnp.float32, precision=lax.Precision.HIGHEST)
    lane = lax.broadcasted_iota(jnp.int32, logits.shape, 1)
    logits = jnp.where(lane < N_EXPERTS, logits, NEG_BIG)
    m1 = jnp.max(logits, axis=1, keepdims=True)
    i1 = jnp.min(jnp.where(logits == m1, lane, LANES), axis=1, keepdims=True)
    rest = jnp.where(lane == i1, NEG_BIG, logits)
    m2 = jnp.max(rest, axis=1, keepdims=True)
    i2 = jnp.min(jnp.where(rest == m2, lane, LANES), axis=1, keepdims=True)
    e2 = jnp.exp(m2 - m1)
    w1 = 1.0 / (1.0 + e2)
    hot1, hot2 = lane == i1, lane == i2
    both = jnp.logical_or(hot1, hot2).astype(jnp.float32)
    r = lax.broadcasted_iota(jnp.int32, (tm, tm), 0)
    c = lax.broadcasted_iota(jnp.int32, (tm, tm), 1)
    earlier = (c < r).astype(jnp.bfloat16)
    prior = carry_ref[...] + jnp.dot(earlier, both.astype(jnp.bfloat16), preferred_element_type=jnp.float32)
    rank1 = jnp.sum(jnp.where(hot1, prior, 0.0), axis=1, keepdims=True)
    rank2 = jnp.sum(jnp.where(hot2, prior, 0.0), axis=1, keepdims=True)
    carry_ref[...] += jnp.sum(both, axis=0, keepdims=True)
    cnt_ref[...] = jnp.broadcast_to(carry_ref[...], cnt_ref.shape)
    wts_ref[...] = _lane_pick(lane, [w1, e2 * w1])
    idx_ref[...] = _lane_pick(lane, [i1, i2, rank1.astype(jnp.int32), rank2.astype(jnp.int32)])


def moe_route(x, g, w_router):
    n, d = x.shape
    tm = _tile(n, ROUTE_ROWS)
    w = jnp.pad(w_router, ((0, 0), (0, LANES - N_EXPERTS)))
    row = pl.BlockSpec((tm, LANES), lambda i: (i, 0))
    return pl.pallas_call(
        functools.partial(_route_kernel, tm=tm),
        grid=(n // tm,),
        in_specs=[pl.BlockSpec((tm, d), lambda i: (i, 0)),
                  pl.BlockSpec((1, d), lambda i: (0, 0)),
                  pl.BlockSpec((d, LANES), lambda i: (0, 0))],
        out_specs=[row, row, pl.BlockSpec((8, LANES), lambda i: (0, 0))],
        out_shape=[jax.ShapeDtypeStruct((n, LANES), jnp.float32),
                   jax.ShapeDtypeStruct((n, LANES), jnp.int32),
                   jax.ShapeDtypeStruct((8, LANES), jnp.float32)],
        scratch_shapes=[pltpu.VMEM((1, LANES), jnp.float32)],
        compiler_params=_cparams("arbitrary"),
        name="moe_route",
    )(x, g.reshape(1, d), w)


def _dest_kernel(idx_ref, cnt_ref, dest_ref, te_ref, *, tile_rows):
    tiles = jnp.floor((cnt_ref[...] + (tile_rows - 1)) / tile_rows)
    er = lax.broadcasted_iota(jnp.int32, (LANES, LANES), 0)
    ec = lax.broadcasted_iota(jnp.int32, (LANES, LANES), 1)
    first_tile = jnp.dot(tiles.astype(jnp.bfloat16), (er < ec).astype(jnp.bfloat16),
                         preferred_element_type=jnp.float32)
    first_row = first_tile[0:1, :] * tile_rows
    idx = idx_ref[...]
    lane = lax.broadcasted_iota(jnp.int32, idx.shape, 1)
    idx_f = idx.astype(jnp.float32)
    dests = []
    for slot in range(2):
        expert = jnp.sum(jnp.where(lane == slot, idx, 0), axis=1, keepdims=True)
        start = jnp.sum(jnp.where(lane == expert, first_row, 0.0), axis=1, keepdims=True)
        dests.append((start + _lane_col(idx_f, 2 + slot)).astype(jnp.int32))
    dest_ref[...] = _lane_pick(lane, dests)
    last_tile = first_tile + tiles
    tau = lax.broadcasted_iota(jnp.int32, te_ref.shape, 1).astype(jnp.float32)
    owner = jnp.zeros(te_ref.shape, jnp.float32)
    for e in range(N_EXPERTS):
        owner = owner + (_lane_col(last_tile, e) <= tau).astype(jnp.float32)
    te_ref[...] = owner.astype(jnp.int32)


def moe_dest(idx, cnt, n_tiles):
    n = idx.shape[0]
    tm = _tile(n, ROUTE_ROWS)
    te_lanes = -(-n_tiles // LANES) * LANES
    return pl.pallas_call(
        functools.partial(_dest_kernel, tile_rows=MOE_TILE_ROWS),
        grid=(n // tm,),
        in_specs=[pl.BlockSpec((tm, LANES), lambda i: (i, 0)),
                  pl.BlockSpec((8, LANES), lambda i: (0, 0))],
        out_specs=[pl.BlockSpec((tm, LANES), lambda i: (i, 0)),
                   pl.BlockSpec((8, te_lanes), lambda i: (0, 0))],
        out_shape=[jax.ShapeDtypeStruct((n, LANES), jnp.int32),
                   jax.ShapeDtypeStruct((8, te_lanes), jnp.int32)],
        compiler_params=_cparams("arbitrary"),
        name="moe_dest",
    )(idx, cnt)


SUBLANES = 8


def _row_tiles(d):
    return (d // (SUBLANES * LANES) * SUBLANES, LANES)


def _dispatch_kernel(dest_ref, x_ref, g_ref, zeros_ref, xs_ref, hp_ref, sem, *, tm):
    del zeros_ref
    hp_ref[...] = _rms(x_ref[...], g_ref[...]).reshape(hp_ref.shape)

    def issue(r, carry):
        for slot in range(2):
            pltpu.make_async_copy(hp_ref.at[r], xs_ref.at[dest_ref[0, 2 * r + slot]], sem).start(priority=slot)
        return carry

    def drain(r, carry):
        for slot in range(2):
            pltpu.make_async_copy(hp_ref.at[0], xs_ref.at[0], sem).wait()
        return carry

    lax.fori_loop(0, tm, issue, 0, unroll=8)
    lax.fori_loop(0, tm, drain, 0, unroll=8)


def moe_dispatch(x, g, dest_pairs, n_rows):
    n, d = x.shape
    tm = dest_pairs.shape[2] // 2
    row = _row_tiles(d)
    zeros = jnp.zeros((n_rows,) + row, jnp.float32)
    return pl.pallas_call(
        functools.partial(_dispatch_kernel, tm=tm),
        grid=(n // tm,),
        in_specs=[pl.BlockSpec((None, 1, 2 * tm), lambda i: (i, 0, 0), memory_space=pltpu.SMEM),
                  pl.BlockSpec((tm, d), lambda i: (i, 0)),
                  pl.BlockSpec((1, d), lambda i: (0, 0)),
                  pl.BlockSpec(memory_space=pl.ANY)],
        out_specs=pl.BlockSpec(memory_space=pl.ANY),
        out_shape=jax.ShapeDtypeStruct((n_rows,) + row, jnp.float32),
        scratch_shapes=[pltpu.VMEM((tm,) + row, jnp.float32), pltpu.SemaphoreType.DMA(())],
        input_output_aliases={3: 0},
        compiler_params=_cparams("arbitrary"),
        name="moe_dispatch",
    )(dest_pairs, x, g.reshape(1, d), zeros)


def _grouped_ffn_kernel(te_ref, xs_ref, wg_ref, wu_ref, wd_ref, ys_ref, h_ref, acc_ref):
    i, f = pl.program_id(0), pl.program_id(1)
    live = te_ref[i] < N_EXPERTS
    first = f == 0

    @pl.when(jnp.logical_and(first, live))
    def _():
        h = xs_ref[...].reshape(h_ref.shape).astype(h_ref.dtype)
        h_ref[...] = h
        acc_ref[...] = _swiglu_step(h, wg_ref, wu_ref, wd_ref)

    @pl.when(jnp.logical_and(first, jnp.logical_not(live)))
    def _():
        acc_ref[...] = jnp.zeros(acc_ref.shape, jnp.float32)

    @pl.when(jnp.logical_and(jnp.logical_not(first), live))
    def _():
        acc_ref[...] += _swiglu_step(h_ref[...], wg_ref, wu_ref, wd_ref)

    @pl.when(f == pl.num_programs(1) - 1)
    def _():
        ys_ref[...] = acc_ref[...].reshape(ys_ref.shape)


def moe_grouped_ffn(xs, tile_expert, w_gate, w_up, w_down, *, tf=512):
    n_rows = xs.shape[0]
    row = xs.shape[1:]
    d = w_gate.shape[1]
    ff = w_gate.shape[2]
    tm = MOE_TILE_ROWS
    expert = lambda i, te: jnp.minimum(te[i], N_EXPERTS - 1)
    rows_spec = pl.BlockSpec((tm,) + row, lambda i, f, te: (i, 0, 0))
    return pl.pallas_call(
        _grouped_ffn_kernel,
        grid_spec=pltpu.PrefetchScalarGridSpec(
            num_scalar_prefetch=1,
            grid=(n_rows // tm, ff // tf),
            in_specs=[rows_spec,
                      pl.BlockSpec((None, d, tf), lambda i, f, te: (expert(i, te), 0, f)),
                      pl.BlockSpec((None, d, tf), lambda i, f, te: (expert(i, te), 0, f)),
                      pl.BlockSpec((None, tf, d), lambda i, f, te: (expert(i, te), f, 0))],
            out_specs=rows_spec,
            scratch_shapes=[pltpu.VMEM((tm, d), jnp.bfloat16), pltpu.VMEM((tm, d), jnp.float32)]),
        out_shape=jax.ShapeDtypeStruct(xs.shape, jnp.float32),
        compiler_params=_cparams("parallel", "arbitrary"),
        name="moe_grouped_ffn",
    )(tile_expert, xs, w_gate, w_up, w_down)


def _combine_kernel(dest_ref, x_ref, wts_ref, gf_ref, ys_ref, o_ref, y1_ref, y2_ref, sem, *, tm, final_norm):
    bufs = (y1_ref, y2_ref)

    def issue(r, carry):
        for slot in range(2):
            pltpu.make_async_copy(ys_ref.at[dest_ref[0, 2 * r + slot]], bufs[slot].at[r], sem).start(priority=slot)
        return carry

    def drain(r, carry):
        for slot in range(2):
            pltpu.make_async_copy(ys_ref.at[0], bufs[slot].at[0], sem).wait()
        return carry

    lax.fori_loop(0, tm, issue, 0, unroll=8)
    lax.fori_loop(0, tm, drain, 0, unroll=8)
    wts = wts_ref[...]
    out = x_ref[...]
    for slot in range(2):
        out = out + _lane_col(wts, slot) * bufs[slot][...].reshape(out.shape)
    if final_norm:
        out = _rms(out, gf_ref[...])
    o_ref[...] = out


def moe_combine(x, wts, dest_pairs, ys, final_gain, *, final_norm):
    n, d = x.shape
    tm = dest_pairs.shape[2] // 2
    return pl.pallas_call(
        functools.partial(_combine_kernel, tm=tm, final_norm=final_norm),
        grid=(n // tm,),
        in_specs=[pl.BlockSpec((None, 1, 2 * tm), lambda i: (i, 0, 0), memory_space=pltpu.SMEM),
                  pl.BlockSpec((tm, d), lambda i: (i, 0)),
                  pl.BlockSpec((tm, LANES), lambda i: (i, 0)),
                  pl.BlockSpec((1, d), lambda i: (0, 0)),
                  pl.BlockSpec(memory_space=pl.ANY)],
        out_specs=pl.BlockSpec((tm, d), lambda i: (i, 0)),
        out_shape=jax.ShapeDtypeStruct((n, d), jnp.float32),
        scratch_shapes=[pltpu.VMEM((tm,) + ys.shape[1:], jnp.float32),
                        pltpu.VMEM((tm,) + ys.shape[1:], jnp.float32),
                        pltpu.SemaphoreType.DMA(())],
        compiler_params=_cparams("arbitrary"),
        name="moe_combine",
    )(dest_pairs, x, wts, final_gain.reshape(1, d), ys)


def moe_swiglu(x, g, w_router, w_gate, w_up, w_down, final_gain, *, final_norm):
    n = x.shape[0]
    n_tiles = -(-2 * n // MOE_TILE_ROWS) + N_EXPERTS
    wts, idx, cnt = moe_route(x, g, w_router)
    dest, tile_expert = moe_dest(idx, cnt, n_tiles)
    tm = _tile(n, ROUTE_ROWS)
    dest_pairs = dest[:, :2].reshape(n // tm, 1, 2 * tm)
    xs = moe_dispatch(x, g, dest_pairs, n_tiles * MOE_TILE_ROWS)
    ys = moe_grouped_ffn(xs, tile_expert[0, :n_tiles], w_gate, w_up, w_down)
    return moe_combine(x, wts, dest_pairs, ys, final_gain, final_norm=final_norm)


def _rope_tables(seq):
    half = DA_DIM // 2
    inv_freq = 1.0 / (ROPE_THETA ** (jnp.arange(0, DA_DIM, 2, dtype=jnp.float32) / DA_DIM))
    ang = jnp.arange(seq, dtype=jnp.float32)[:, None] * inv_freq[None, :]
    cos, sin = jnp.cos(ang), jnp.sin(ang)
    reps = LANES // half
    sign = jnp.tile(jnp.concatenate([-jnp.ones((half,)), jnp.ones((half,))]), reps // 2)
    return jnp.tile(cos, (1, reps)), jnp.tile(sin, (1, reps)) * sign[None, :]


def kernel(x, mem, norm_mix, w_in, gd_conv, lam_q1, lam_k1, lam_q2, lam_k2, da_subln, gd_a_log,
           gd_dt_bias, gd_norm, w_br_a, w_br_b, w_br_c, w_out, norm_x, norm_mem, w_q_x, w_kv_x,
           w_o_x, norm_ffn, w_gate_dense, w_up_dense, w_down_dense, w_router, w_gate_exp,
           w_up_exp, w_down_exp, norm_final):
    batch, seq, d = x.shape
    n_mem = mem.shape[1]
    depth = norm_mix.shape[0]
    bf = jnp.bfloat16
    n = batch * seq
    cos, sin = _rope_tables(seq)
    xf = x.reshape(n, d)
    memf = mem.reshape(batch * n_mem, d)
    ab0 = MIX_W
    ab1 = MIX_W + 2 * GD_HEADS

    for l in range(depth):
        lam_init = 0.8 - 0.6 * math.exp(-0.3 * l)
        w_l = w_in[l]
        w_main = jnp.concatenate([w_l[:, :ab0], w_l[:, ab1:]], axis=1).astype(bf)
        w_ab = jnp.pad(w_l[:, ab0:ab1], ((0, 0), (0, LANES - 2 * GD_HEADS))).astype(bf)
        proj, gate = mixer_in_proj(xf, norm_mix[l], w_main, w_ab, cos, sin, gd_a_log[l], gd_dt_bias[l], seq)

        lam_params = jnp.stack([lam_q1[l], lam_k1[l], lam_q2[l], lam_k2[l]])
        o_a = diff_attention(proj, lam_params, da_subln[l], batch, seq, lam_init)
        o_b = stick_breaking_attention(proj, batch, seq)
        qkv = gd_conv_silu(proj, gd_conv[l], batch, seq)
        u, w, qg, kd, amat = gd_local(qkv, gate, batch, seq)
        o_c = gd_scan(u, w, qg, kd, amat, gate, proj, gd_norm[l], batch, seq)
        xf = merge_branches(xf, o_a, o_b, o_c, proj, w_br_a[l].astype(bf), w_br_b[l].astype(bf),
                            w_br_c[l].astype(bf), w_out[l].astype(bf))

        kv = rms_matmul(memf, norm_mem[l], w_kv_x[l].astype(bf), bf)
        xf = cross_attention(xf, norm_x[l], w_q_x[l].astype(bf), kv, w_o_x[l].astype(bf),
                             batch, seq, n_mem)

        last = l == depth - 1
        i = l // 2
        if l % 2 == 0:
            xf = swiglu_ffn(xf, norm_ffn[l], w_gate_dense[i].astype(bf), w_up_dense[i].astype(bf),
                            w_down_dense[i].astype(bf), norm_final, final_norm=last)
        else:
            xf = moe_swiglu(xf, norm_ffn[l], w_router[i], w_gate_exp[i].astype(bf), w_up_exp[i].astype(bf),
                            w_down_exp[i].astype(bf), norm_final, final_norm=last)
    return xf.reshape(batch, seq, d)
```

```python
import functools
import math

import jax
import jax.numpy as jnp
from jax import lax
from jax.experimental import pallas as pl
from jax.experimental.pallas import tpu as pltpu

EPS = 1e-6
LOG2E = math.log2(math.e)
ROPE_THETA = 10000.0
LANES = 128

DA_HEADS = 4
DA_DIM = 64
SB_HEADS = 4
SB_DIM = 128
GD_HEADS = 4
GD_DIM = 128
GD_CONV = 4
GD_CHUNK = 64
X_HEADS = 4
X_DIM = 128
N_EXPERTS = 8
HEAD_W = 512

COL_DA_Q, COL_DA_K, COL_DA_V = 0, 4, 8
COL_SB_Q, COL_SB_K, COL_SB_V = 12, 16, 20
COL_GD_Q, COL_GD_Z = 24, 36
MIX_W = 5120

SB_LOG2_FLOOR = -110.0 * LOG2E
NEG_BIG = -1e30
VMEM_LIMIT = 48 * 1024 * 1024

_NT = (((1,), (1,)), ((), ()))
_TN = (((0,), (0,)), ((), ()))


def _cparams(*sem):
    return pltpu.CompilerParams(dimension_semantics=sem, vmem_limit_bytes=VMEM_LIMIT)


def _tile(n, pref):
    t = min(n, pref)
    while n % t:
        t //= 2
    return t


def _bdot(a, b, dims=None):
    a = a.astype(jnp.bfloat16)
    b = b.astype(jnp.bfloat16)
    if dims is None:
        return jnp.dot(a, b, preferred_element_type=jnp.float32)
    return lax.dot_general(a, b, dims, preferred_element_type=jnp.float32)


def _dot_split(a, b_exact):
    hi = a.astype(jnp.bfloat16)
    lo = (a - hi.astype(jnp.float32)).astype(jnp.bfloat16)
    return (jnp.dot(hi, b_exact, preferred_element_type=jnp.float32)
            + jnp.dot(lo, b_exact, preferred_element_type=jnp.float32))


def _rms(x, g):
    return x * lax.rsqrt(jnp.mean(x * x, axis=-1, keepdims=True) + EPS) * g


def _softplus(z):
    return jnp.maximum(z, 0.0) + jnp.log1p(jnp.exp(-jnp.abs(z)))


def _silu(x):
    return x * jax.nn.sigmoid(x)


def _lane_tile(x, reps):
    return jnp.concatenate([x] * reps, axis=1)


def _lane_col(x, lane):
    ids = lax.broadcasted_iota(jnp.int32, x.shape, 1)
    return jnp.sum(jnp.where(ids == lane, x, 0.0), axis=1, keepdims=True)


def _rms_matmul_kernel(x_ref, g_ref, w_ref, o_ref, h_ref):
    first = pl.program_id(1) == 0

    @pl.when(first)
    def _():
        h = _rms(x_ref[...], g_ref[...]).astype(h_ref.dtype)
        h_ref[...] = h
        o_ref[...] = jnp.dot(h, w_ref[...], preferred_element_type=jnp.float32).astype(o_ref.dtype)

    @pl.when(jnp.logical_not(first))
    def _():
        o_ref[...] = jnp.dot(h_ref[...], w_ref[...], preferred_element_type=jnp.float32).astype(o_ref.dtype)


def rms_matmul(x, g, w, out_dtype, *, tm=1024, tn=1024):
    n, d = x.shape
    nout = w.shape[1]
    tm, tn = _tile(n, tm), _tile(nout, tn)
    return pl.pallas_call(
        _rms_matmul_kernel,
        grid=(n // tm, nout // tn),
        in_specs=[pl.BlockSpec((tm, d), lambda i, j: (i, 0)),
                  pl.BlockSpec((1, d), lambda i, j: (0, 0)),
                  pl.BlockSpec((d, tn), lambda i, j: (0, j))],
        out_specs=pl.BlockSpec((tm, tn), lambda i, j: (i, j)),
        out_shape=jax.ShapeDtypeStruct((n, nout), out_dtype),
        scratch_shapes=[pltpu.VMEM((tm, d), w.dtype)],
        compiler_params=_cparams("parallel", "arbitrary"),
        name="rms_matmul",
    )(x, g.reshape(1, d), w)


def _gd_gates(ab, a_log, dt_bias):
    is_decay_lane = lax.broadcasted_iota(jnp.int32, (GD_CHUNK, LANES), 1) < GD_HEADS
    g = -jnp.exp(a_log) * _softplus(ab + dt_bias)
    beta = jax.nn.sigmoid(ab)
    r = lax.broadcasted_iota(jnp.int32, (GD_CHUNK, GD_CHUNK), 0)
    c = lax.broadcasted_iota(jnp.int32, (GD_CHUNK, GD_CHUNK), 1)
    tril = (c <= r).astype(jnp.bfloat16)
    chunks = []
    for ci in range(ab.shape[0] // GD_CHUNK):
        sl = slice(ci * GD_CHUNK, (ci + 1) * GD_CHUNK)
        hi = g[sl].astype(jnp.bfloat16)
        mid_f = g[sl] - hi.astype(jnp.float32)
        mid = mid_f.astype(jnp.bfloat16)
        lo = (mid_f - mid.astype(jnp.float32)).astype(jnp.bfloat16)
        gc = (jnp.dot(tril, hi, preferred_element_type=jnp.float32)
              + jnp.dot(tril, mid, preferred_element_type=jnp.float32)
              + jnp.dot(tril, lo, preferred_element_type=jnp.float32))
        chunks.append(jnp.where(is_decay_lane, gc, beta[sl]))
    return jnp.concatenate(chunks, axis=0)


def _mixer_proj_kernel(x_ref, g_ref, w_ref, wab_ref, cos_ref, sin_ref, alog_ref, dtb_ref,
                       o_ref, gate_ref, h_ref):
    first = pl.program_id(1) == 0

    @pl.when(first)
    def _():
        h = _rms(x_ref[...], g_ref[...]).astype(h_ref.dtype)
        h_ref[...] = h
        acc = jnp.dot(h, w_ref[...], preferred_element_type=jnp.float32)
        cos, sin = cos_ref[...], sin_ref[...]
        lane = lax.broadcasted_iota(jnp.int32, cos.shape, 1)
        first_half = (lane % DA_DIM) < (DA_DIM // 2)
        for blk in range(2 * DA_HEADS):
            sl = slice(blk * LANES, (blk + 1) * LANES)
            t = acc[:, sl]
            partner = jnp.where(first_half, pltpu.roll(t, LANES - DA_DIM // 2, 1),
                                pltpu.roll(t, DA_DIM // 2, 1))
            rot = t * cos + partner * sin
            if blk < DA_HEADS:
                rot = rot * (DA_DIM ** -0.5 * LOG2E)
            o_ref[:, sl] = rot.astype(o_ref.dtype)
        ab = jnp.dot(h, wab_ref[...], preferred_element_type=jnp.float32)
        gate_ref[...] = _gd_gates(ab, alog_ref[...], dtb_ref[...])

    @pl.when(jnp.logical_not(first))
    def _():
        o_ref[...] = jnp.dot(h_ref[...], w_ref[...], preferred_element_type=jnp.float32).astype(o_ref.dtype)


def mixer_in_proj(x, g, w_main, w_ab, cos, sin, a_log, dt_bias, seq):
    n, d = x.shape
    nout = w_main.shape[1]
    tm = _tile(seq, 1024)
    tn = 2 * DA_HEADS * LANES
    per_seq = seq // tm
    pad = jnp.zeros((LANES - GD_HEADS,), jnp.float32)
    alog = jnp.concatenate([a_log, pad]).reshape(1, LANES)
    dtb = jnp.concatenate([dt_bias, pad]).reshape(1, LANES)
    vec = lambda width: pl.BlockSpec((1, width), lambda i, j: (0, 0))
    table = pl.BlockSpec((tm, LANES), lambda i, j: (i % per_seq, 0))
    return pl.pallas_call(
        _mixer_proj_kernel,
        grid=(n // tm, nout // tn),
        in_specs=[pl.BlockSpec((tm, d), lambda i, j: (i, 0)), vec(d),
                  pl.BlockSpec((d, tn), lambda i, j: (0, j)),
                  pl.BlockSpec((d, LANES), lambda i, j: (0, 0)),
                  table, table, vec(LANES), vec(LANES)],
        out_specs=[pl.BlockSpec((tm, tn), lambda i, j: (i, j)),
                   pl.BlockSpec((tm, LANES), lambda i, j: (i, 0))],
        out_shape=[jax.ShapeDtypeStruct((n, nout), jnp.bfloat16),
                   jax.ShapeDtypeStruct((n, LANES), jnp.float32)],
        scratch_shapes=[pltpu.VMEM((tm, d), jnp.bfloat16)],
        compiler_params=_cparams("parallel", "arbitrary"),
        name="mixer_in_proj",
    )(x, g.reshape(1, d), w_main, w_ab, cos, sin, alog, dtb)


def _diff_attn_kernel(lam_ref, q_ref, k_ref, v_ref, subln_ref, o_ref,
                      qs_ref, sa_ref, sb_ref, ma_ref, mb_ref, m_ref, acc_ref, *, t, lam_init):
    i = pl.program_id(2)
    q = q_ref[...]
    lane = lax.broadcasted_iota(jnp.int32, q.shape, 1)
    zero = jnp.zeros_like(q)
    qs_ref[:t, :] = jnp.where(lane < DA_DIM, q, zero)
    qs_ref[t:, :] = jnp.where(lane >= DA_DIM, q, zero)
    m_ref[...] = jnp.full(m_ref.shape, NEG_BIG, jnp.float32)
    acc_ref[...] = jnp.zeros(acc_ref.shape, jnp.float32)
    reps = t // LANES
    ones = jnp.ones((t, LANES), jnp.bfloat16)

    def scores(j, s_ref, mx_ref, masked=False):
        k = k_ref[pl.ds(pl.multiple_of(jnp.maximum(j, 0) * t, t), t), :]
        s = lax.dot_general(qs_ref[...], k, _NT, preferred_element_type=jnp.float32)
        if masked:
            row = lax.broadcasted_iota(jnp.int32, s.shape, 0) % t
            col = lax.broadcasted_iota(jnp.int32, s.shape, 1)
            s = jnp.where(col <= row, s, NEG_BIG)
        s_ref[...] = s
        mx_ref[...] = jnp.broadcast_to(jnp.max(s, axis=1, keepdims=True), mx_ref.shape)

    def consume(j, s_ref, mx_ref):
        m_old = m_ref[...]
        m_new = jnp.maximum(m_old, mx_ref[...])
        alpha = jnp.exp2(m_old - m_new)
        p = jnp.exp2(s_ref[...] - _lane_tile(m_new, reps))
        v1 = jnp.concatenate([v_ref[pl.ds(pl.multiple_of(j * t, t), t), :], ones], axis=1)
        acc_ref[...] = _lane_tile(alpha, 2) * acc_ref[...] + jnp.dot(
            p.astype(jnp.bfloat16), v1, preferred_element_type=jnp.float32)
        m_ref[...] = m_new

    scores(i, sa_ref, ma_ref, masked=True)
    scores(i - 1, sb_ref, mb_ref)
    consume(i, sa_ref, ma_ref)

    def earlier_pair(n, carry):
        j = i - 1 - 2 * n
        scores(j - 1, sa_ref, ma_ref)
        consume(j, sb_ref, mb_ref)
        scores(j - 2, sb_ref, mb_ref)
        consume(j - 1, sa_ref, ma_ref)
        return carry

    lax.fori_loop(0, i // 2, earlier_pair, 0)

    @pl.when(i % 2 == 1)
    def _():
        consume(0, sb_ref, mb_ref)

    lq1, lk1, lq2, lk2 = (lam_ref[r:r + 1, :] for r in range(4))
    lam = (jnp.exp(jnp.sum(lq1 * lk1, axis=1, keepdims=True))
           - jnp.exp(jnp.sum(lq2 * lk2, axis=1, keepdims=True)) + lam_init)
    acc = acc_ref[:, :LANES] / acc_ref[:, LANES:]
    o = acc[:t] - lam * acc[t:]
    o_ref[...] = (_rms(o, subln_ref[...]) * (1.0 - lam_init)).astype(o_ref.dtype)


def diff_attention(proj, lam_params, subln, batch, seq, lam_init, *, t=512):
    n = proj.shape[0]
    t = _tile(seq, t)
    nq = seq // t
    return pl.pallas_call(
        functools.partial(_diff_attn_kernel, t=t, lam_init=lam_init),
        grid=(batch, DA_HEADS, nq),
        in_specs=[pl.BlockSpec((4, DA_DIM), lambda b, h, i: (0, 0)),
                  pl.BlockSpec((t, LANES), lambda b, h, i: (b * nq + i, COL_DA_Q + h)),
                  pl.BlockSpec((seq, LANES), lambda b, h, i: (b, COL_DA_K + h)),
                  pl.BlockSpec((seq, LANES), lambda b, h, i: (b, COL_DA_V + h)),
                  pl.BlockSpec((1, LANES), lambda b, h, i: (0, 0))],
        out_specs=pl.BlockSpec((t, LANES), lambda b, h, i: (b * nq + i, h)),
        out_shape=jax.ShapeDtypeStruct((n, HEAD_W), jnp.bfloat16),
        scratch_shapes=[pltpu.VMEM((2 * t, LANES), jnp.bfloat16),
                        pltpu.VMEM((2 * t, t), jnp.float32),
                        pltpu.VMEM((2 * t, t), jnp.float32),
                        pltpu.VMEM((2 * t, LANES), jnp.float32),
                        pltpu.VMEM((2 * t, LANES), jnp.float32),
                        pltpu.VMEM((2 * t, LANES), jnp.float32),
                        pltpu.VMEM((2 * t, 2 * LANES), jnp.float32)],
        compiler_params=_cparams("parallel", "parallel", "arbitrary"),
        name="diff_attention",
    )(lam_params, proj, proj, proj, subln.reshape(1, LANES))


def _sb_attn_kernel(q_ref, k_ref, v_ref, o_ref, za_ref, zb_ref, carry_ref, acc_ref, *, tq):
    i = pl.program_id(2)
    qs_val = (q_ref[...].astype(jnp.float32) * (SB_DIM ** -0.5 * LOG2E)).astype(jnp.bfloat16)
    carry_ref[...] = jnp.zeros(carry_ref.shape, jnp.float32)
    acc_ref[...] = jnp.zeros(acc_ref.shape, jnp.float32)
    r = lax.broadcasted_iota(jnp.int32, (tq, tq), 0)
    c = lax.broadcasted_iota(jnp.int32, (tq, tq), 1)
    later = (r > c).astype(jnp.bfloat16)
    reps = tq // LANES

    def tile_rows(j):
        return pl.ds(pl.multiple_of(jnp.maximum(j, 0) * tq, tq), tq)

    def logits(j, z_ref):
        z_ref[...] = lax.dot_general(qs_val, k_ref[tile_rows(j), :], _NT, preferred_element_type=jnp.float32)

    def consume(j, z_ref, masked):
        z = z_ref[...]
        sp = jnp.maximum(z, 0.0) + jnp.log2(1.0 + jnp.exp2(-jnp.abs(z)))
        log_fail = jnp.where(c < r, -sp, 0.0) if masked else -sp
        log_remain = _lane_tile(carry_ref[...], reps) + _dot_split(log_fail, later)
        attn = jnp.exp2(z - sp + log_remain)
        if masked:
            attn = jnp.where(c < r, attn, 0.0)
        acc_ref[...] += jnp.dot(attn.astype(jnp.bfloat16), v_ref[tile_rows(j), :],
                                preferred_element_type=jnp.float32)
        carry_ref[...] += jnp.sum(log_fail, axis=1, keepdims=True)

    def consume_if_exists(j, z_ref):
        carry_ref[...] = jnp.where(j < 0, NEG_BIG, carry_ref[...])
        consume(j, z_ref, False)

    logits(i, za_ref)
    logits(i - 1, zb_ref)
    consume(i, za_ref, True)
    logits(i - 2, za_ref)
    consume_if_exists(i - 1, zb_ref)

    def cond(state):
        j, alive = state
        return jnp.logical_and(j >= 0, alive)

    def body(state):
        j, _ = state
        logits(j - 1, zb_ref)
        consume(j, za_ref, False)
        logits(j - 2, za_ref)
        consume_if_exists(j - 1, zb_ref)
        return j - 2, jnp.max(carry_ref[...]) > SB_LOG2_FLOOR

    lax.while_loop(cond, body, (i - 2, jnp.max(carry_ref[...]) > SB_LOG2_FLOOR))
    o_ref[...] = acc_ref[...].astype(o_ref.dtype)


def stick_breaking_attention(proj, batch, seq, *, tq=256):
    n = proj.shape[0]
    tq = _tile(seq, tq)
    nq = seq // tq
    return pl.pallas_call(
        functools.partial(_sb_attn_kernel, tq=tq),
        grid=(batch, SB_HEADS, nq),
        in_specs=[pl.BlockSpec((tq, LANES), lambda b, h, i: (b * nq + i, COL_SB_Q + h)),
                  pl.BlockSpec((seq, LANES), lambda b, h, i: (b, COL_SB_K + h)),
                  pl.BlockSpec((seq, LANES), lambda b, h, i: (b, COL_SB_V + h))],
        out_specs=pl.BlockSpec((tq, LANES), lambda b, h, i: (b * nq + i, h)),
        out_shape=jax.ShapeDtypeStruct((n, HEAD_W), jnp.bfloat16),
        scratch_shapes=[pltpu.VMEM((tq, tq), jnp.float32),
                        pltpu.VMEM((tq, tq), jnp.float32),
                        pltpu.VMEM((tq, LANES), jnp.float32),
                        pltpu.VMEM((tq, LANES), jnp.float32)],
        compiler_params=_cparams("parallel", "parallel", "arbitrary"),
        name="stick_breaking_attention",
    )(proj, proj, proj)


def _gd_conv_kernel(x_ref, w_ref, o_ref):
    c = pl.program_id(1)
    w = w_ref[...]
    is_q = c < GD_HEADS
    is_v = c >= 2 * GD_HEADS

    def conv_norm(x, causal_mask):
        y = x * w[GD_CONV - 1:GD_CONV, :]
        for back in range(1, GD_CONV):
            shifted = pltpu.roll(x, back, 0)
            if causal_mask:
                row = lax.broadcasted_iota(jnp.int32, x.shape, 0)
                shifted = jnp.where(row >= back, shifted, 0.0)
            y = y + shifted * w[GD_CONV - 1 - back:GD_CONV - back, :]
        y = _silu(y)
        inv = lax.rsqrt(jnp.sum(y * y, axis=1, keepdims=True) + EPS)
        scale = jnp.where(is_v, 1.0, inv * jnp.where(is_q, GD_DIM ** -0.5, 1.0))
        return (y * scale).astype(o_ref.dtype)

    o_ref[...] = conv_norm(x_ref[...].astype(jnp.float32), False)
    o_ref[:16, :] = conv_norm(x_ref[:16, :].astype(jnp.float32), True)


def gd_conv_silu(proj, conv_w, batch, seq):
    n = proj.shape[0]
    ncol = 3 * GD_HEADS
    return pl.pallas_call(
        _gd_conv_kernel,
        grid=(batch, ncol),
        in_specs=[pl.BlockSpec((seq, LANES), lambda b, c: (b, COL_GD_Q + c)),
                  pl.BlockSpec((GD_CONV, LANES), lambda b, c: (0, c))],
        out_specs=pl.BlockSpec((seq, LANES), lambda b, c: (b, c)),
        out_shape=jax.ShapeDtypeStruct((n, ncol * LANES), jnp.float32),
        compiler_params=_cparams("parallel", "parallel"),
        name="gd_conv",
    )(proj, conv_w)


def _unit_lower_inverse(lmats, r, c):
    eye = (r == c).astype(jnp.float32)
    diag16 = (r // 16) == (c // 16)
    ps = [jnp.where(diag16, lm, 0.0) for lm in lmats]
    ts = [eye - p for p in ps]
    for _ in range(3):
        ps = [_bdot(p, p) for p in ps]
        ts = [t + _bdot(t, p) for t, p in zip(ts, ps)]
    for blk in (32, 64):
        band = jnp.logical_and((r // blk) == (c // blk), (r // (blk // 2)) != (c // (blk // 2)))
        tcs = [_bdot(t, jnp.where(band, lm, 0.0)) for t, lm in zip(ts, lmats)]
        ts = [t - _bdot(tc, t) for t, tc in zip(ts, tcs)]
    return ts


def _gd_local_kernel(q_ref, k_ref, v_ref, gate_ref, u_ref, w_ref, qg_ref, kd_ref, a_ref, *, rows):
    h = pl.program_id(1)
    gate = gate_ref[...]
    gc_all = _lane_col(gate, h)
    beta_all = _lane_col(gate, GD_HEADS + h)
    r = lax.broadcasted_iota(jnp.int32, (GD_CHUNK, GD_CHUNK), 0)
    c = lax.broadcasted_iota(jnp.int32, (GD_CHUNK, GD_CHUNK), 1)
    sls = [slice(ci * GD_CHUNK, (ci + 1) * GD_CHUNK) for ci in range(rows // GD_CHUNK)]
    qs = [q_ref[sl, :] for sl in sls]
    ks = [k_ref[sl, :] for sl in sls]
    gcs = [gc_all[sl] for sl in sls]
    kbs = [k * beta_all[sl] for k, sl in zip(ks, sls)]
    kks = [_bdot(kb, k, _NT) for kb, k in zip(kbs, ks)]
    qks = [_bdot(q, k, _NT) for q, k in zip(qs, ks)]
    decays = []
    for gc in gcs:
        gc_rows = jnp.broadcast_to(gc, (GD_CHUNK, LANES))
        gc_cols = jnp.transpose(gc_rows)[0:1, :GD_CHUNK]
        decays.append(jnp.exp(jnp.where(c <= r, gc_rows[:, :GD_CHUNK] - gc_cols, 0.0)))
    tmats = _unit_lower_inverse([jnp.where(c < r, kk * d, 0.0) for kk, d in zip(kks, decays)], r, c)
    egcs = [jnp.exp(gc) for gc in gcs]
    uws = [_bdot(t, jnp.concatenate([v_ref[sl, :] * beta_all[sl], kb * egc], axis=1))
           for t, sl, kb, egc in zip(tmats, sls, kbs, egcs)]
    for sl, q, k, gc, egc, qk, d, uw in zip(sls, qs, ks, gcs, egcs, qks, decays, uws):
        u_ref[sl, :] = uw[:, :LANES]
        w_ref[sl, :] = uw[:, LANES:].astype(w_ref.dtype)
        intra = jnp.where(c <= r, qk * d, 0.0)
        a_ref[sl, :] = jnp.concatenate([intra, jnp.zeros_like(intra)], axis=1).astype(a_ref.dtype)
        qg_ref[sl, :] = (q * egc).astype(qg_ref.dtype)
        g_last = gc[GD_CHUNK - 1:GD_CHUNK, :]
        kd_ref[sl, :] = (k * jnp.exp(g_last - gc)).astype(kd_ref.dtype)


def gd_local(qkv, gate, batch, seq, *, rows=2048):
    n = qkv.shape[0]
    rows = _tile(seq, rows)
    nr = seq // rows
    spec = lambda off: pl.BlockSpec((rows, LANES), lambda b, h, i: (b * nr + i, off + h))
    out_spec = pl.BlockSpec((rows, LANES), lambda b, h, i: (b * nr + i, h))
    bf = jax.ShapeDtypeStruct((n, HEAD_W), jnp.bfloat16)
    return pl.pallas_call(
        functools.partial(_gd_local_kernel, rows=rows),
        grid=(batch, GD_HEADS, nr),
        in_specs=[spec(0), spec(GD_HEADS), spec(2 * GD_HEADS),
                  pl.BlockSpec((rows, LANES), lambda b, h, i: (b * nr + i, 0))],
        out_specs=[out_spec] * 5,
        out_shape=[jax.ShapeDtypeStruct((n, HEAD_W), jnp.float32), bf, bf, bf, bf],
        compiler_params=_cparams("parallel", "parallel", "parallel"),
        name="gd_local",
    )(qkv, qkv, qkv, gate)


def _gd_scan_kernel(u_ref, w_ref, qg_ref, kd_ref, a_ref, gate_ref, z_ref, g_ref, o_ref, state_ref, *, rows, nb):
    @pl.when(pl.program_id(1) == 0)
    def _():
        state_ref[...] = jnp.zeros(state_ref.shape, jnp.float32)

    gain = g_ref[...]
    chains = [(b, h, slice(h * LANES, (h + 1) * LANES)) for b in range(nb) for h in range(GD_HEADS)]

    def chunk(ci, carry):
        start = pl.multiple_of(ci * GD_CHUNK, GD_CHUNK)
        sl = pl.ds(start, GD_CHUNK)
        states = [state_ref[b * GD_HEADS + h] for b, h, _ in chains]
        sbs = [s.astype(jnp.bfloat16) for s in states]
        wq = [jnp.dot(jnp.concatenate([w_ref[b, sl, cs], qg_ref[b, sl, cs]], axis=0), sb,
                      preferred_element_type=jnp.float32) for (b, _, cs), sb in zip(chains, sbs)]
        vbs = [(u_ref[b, sl, cs] - x[:GD_CHUNK]).astype(jnp.bfloat16) for (b, _, cs), x in zip(chains, wq)]
        decay = [jnp.exp(gate_ref[b, pl.ds(start + GD_CHUNK - 1, 1), :]) for b in range(nb)]
        for (b, h, cs), state, vb in zip(chains, states, vbs):
            state_ref[b * GD_HEADS + h] = state * _lane_col(decay[b], h) + lax.dot_general(
                kd_ref[b, sl, cs], vb, _TN, preferred_element_type=jnp.float32)
        outs = [x[GD_CHUNK:] + jnp.dot(a_ref[b, sl, cs][:, :GD_CHUNK], vb, preferred_element_type=jnp.float32)
                for (b, _, cs), x, vb in zip(chains, wq, vbs)]
        for (b, _, cs), out in zip(chains, outs):
            o_ref[b, sl, cs] = (_rms(out, gain) * _silu(z_ref[b, sl, cs].astype(jnp.float32))).astype(o_ref.dtype)
        return carry

    lax.fori_loop(0, rows // GD_CHUNK, chunk, 0)


def gd_scan(u, w, qg, kd, amat, gate, proj, norm_g, batch, seq, *, rows=512, nb=4):
    rows = _tile(seq, rows)
    nb = _tile(batch, nb)
    per_seq = lambda t: t.reshape(batch, seq, t.shape[-1])
    spec = pl.BlockSpec((nb, rows, HEAD_W), lambda b, i: (b, i, 0))
    out = pl.pallas_call(
        functools.partial(_gd_scan_kernel, rows=rows, nb=nb),
        grid=(batch // nb, seq // rows),
        in_specs=[spec, spec, spec, spec, spec,
                  pl.BlockSpec((nb, rows, LANES), lambda b, i: (b, i, 0)),
                  pl.BlockSpec((nb, rows, HEAD_W), lambda b, i: (b, i, COL_GD_Z // GD_HEADS)),
                  pl.BlockSpec((1, LANES), lambda b, i: (0, 0))],
        out_specs=spec,
        out_shape=jax.ShapeDtypeStruct((batch, seq, HEAD_W), jnp.bfloat16),
        scratch_shapes=[pltpu.VMEM((nb * GD_HEADS, GD_DIM, GD_DIM), jnp.float32)],
        compiler_params=_cparams("parallel", "arbitrary"),
        name="gd_scan",
    )(per_seq(u), per_seq(w), per_seq(qg), per_seq(kd), per_seq(amat), per_seq(gate), per_seq(proj),
      norm_g.reshape(1, LANES))
    return out.reshape(batch * seq, HEAD_W)


def _merge_kernel(x_ref, oa_ref, ob_ref, oc_ref, ga_ref, gb_ref, gc_ref,
                  wa_ref, wb_ref, wc_ref, wo_ref, o_ref):
    def branch(o, g, w):
        return jax.nn.sigmoid(g[...].astype(jnp.float32)) * jnp.dot(
            o[...], w[...], preferred_element_type=jnp.float32)

    merged = (branch(oa_ref, ga_ref, wa_ref) + branch(ob_ref, gb_ref, wb_ref)
              + branch(oc_ref, gc_ref, wc_ref))
    o_ref[...] = x_ref[...] + jnp.dot(merged.astype(jnp.bfloat16), wo_ref[...],
                                      preferred_element_type=jnp.float32)


def merge_branches(x, o_a, o_b, o_c, proj, w_a, w_b, w_c, w_out, *, tm=512):
    n, d = x.shape
    tm = _tile(n, tm)
    gate0 = MIX_W // d
    row = lambda width: pl.BlockSpec((tm, width), lambda i: (i, 0))
    full = lambda w: pl.BlockSpec(w.shape, lambda i: (0, 0))
    return pl.pallas_call(
        _merge_kernel,
        grid=(n // tm,),
        in_specs=[row(d), row(HEAD_W), row(HEAD_W), row(HEAD_W),
                  pl.BlockSpec((tm, d), lambda i: (i, gate0)),
                  pl.BlockSpec((tm, d), lambda i: (i, gate0 + 1)),
                  pl.BlockSpec((tm, d), lambda i: (i, gate0 + 2)),
                  full(w_a), full(w_b), full(w_c), full(w_out)],
        out_specs=row(d),
        out_shape=jax.ShapeDtypeStruct((n, d), jnp.float32),
        compiler_params=_cparams("parallel"),
        name="merge_branches",
    )(x, o_a, o_b, o_c, proj, proj, proj, w_a, w_b, w_c, w_out)


def _cross_attn_kernel(x_ref, g_ref, wq_ref, kv_ref, wo_ref, o_ref):
    x = x_ref[...]
    q = jnp.dot(_rms(x, g_ref[...]).astype(jnp.bfloat16), wq_ref[...],
                preferred_element_type=jnp.float32) * (X_DIM ** -0.5 * LOG2E)
    heads = []
    width = X_HEADS * X_DIM
    ones = jnp.ones((kv_ref.shape[0], LANES), jnp.bfloat16)
    for h in range(X_HEADS):
        sl = slice(h * X_DIM, (h + 1) * X_DIM)
        k = kv_ref[:, sl]
        v = kv_ref[:, width + h * X_DIM: width + (h + 1) * X_DIM]
        s = _bdot(q[:, sl], k, _NT)
        p = jnp.exp2(s - jnp.max(s, axis=1, keepdims=True))
        pv = jnp.dot(p.astype(jnp.bfloat16), jnp.concatenate([v, ones], axis=1),
                     preferred_element_type=jnp.float32)
        heads.append(pv[:, :LANES] / pv[:, LANES:])
    o = jnp.concatenate(heads, axis=1).astype(jnp.bfloat16)
    o_ref[...] = x + jnp.dot(o, wo_ref[...], preferred_element_type=jnp.float32)


def cross_attention(x, g, w_q, kv, w_o, batch, seq, n_mem, *, tm=512):
    n, d = x.shape
    tm = _tile(seq, tm)
    per_seq = seq // tm
    full = lambda w: pl.BlockSpec(w.shape, lambda b, i: (0, 0))
    return pl.pallas_call(
        _cross_attn_kernel,
        grid=(batch, per_seq),
        in_specs=[pl.BlockSpec((tm, d), lambda b, i: (b * per_seq + i, 0)),
                  pl.BlockSpec((1, d), lambda b, i: (0, 0)),
                  full(w_q),
                  pl.BlockSpec((n_mem, kv.shape[1]), lambda b, i: (b, 0)),
                  full(w_o)],
        out_specs=pl.BlockSpec((tm, d), lambda b, i: (b * per_seq + i, 0)),
        out_shape=jax.ShapeDtypeStruct((n, d), jnp.float32),
        compiler_params=_cparams("parallel", "parallel"),
        name="cross_attention",
    )(x, g.reshape(1, d), w_q, kv, w_o)


def _swiglu_step(h, wg_ref, wu_ref, wd_ref):
    act = (_silu(jnp.dot(h, wg_ref[...], preferred_element_type=jnp.float32))
           * jnp.dot(h, wu_ref[...], preferred_element_type=jnp.float32))
    return jnp.dot(act.astype(jnp.bfloat16), wd_ref[...], preferred_element_type=jnp.float32)


def _ffn_kernel(x_ref, g_ref, wg_ref, wu_ref, wd_ref, gf_ref, o_ref, h_ref, acc_ref, *, final_norm):
    f = pl.program_id(1)

    @pl.when(f == 0)
    def _():
        x = x_ref[...]
        h = _rms(x, g_ref[...]).astype(h_ref.dtype)
        h_ref[...] = h
        acc_ref[...] = x + _swiglu_step(h, wg_ref, wu_ref, wd_ref)

    @pl.when(f != 0)
    def _():
        acc_ref[...] += _swiglu_step(h_ref[...], wg_ref, wu_ref, wd_ref)

    @pl.when(f == pl.num_programs(1) - 1)
    def _():
        out = acc_ref[...]
        if final_norm:
            out = _rms(out, gf_ref[...])
        o_ref[...] = out


def swiglu_ffn(x, g, w_gate, w_up, w_down, final_gain, *, final_norm, tm=1024, tf=512):
    n, d = x.shape
    ff = w_gate.shape[1]
    tm = _tile(n, tm)
    tf = tf if ff % tf == 0 else ff // 2
    return pl.pallas_call(
        functools.partial(_ffn_kernel, final_norm=final_norm),
        grid=(n // tm, ff // tf),
        in_specs=[pl.BlockSpec((tm, d), lambda i, f: (i, 0)),
                  pl.BlockSpec((1, d), lambda i, f: (0, 0)),
                  pl.BlockSpec((d, tf), lambda i, f: (0, f)),
                  pl.BlockSpec((d, tf), lambda i, f: (0, f)),
                  pl.BlockSpec((tf, d), lambda i, f: (f, 0)),
                  pl.BlockSpec((1, d), lambda i, f: (0, 0))],
        out_specs=pl.BlockSpec((tm, d), lambda i, f: (i, 0)),
        out_shape=jax.ShapeDtypeStruct((n, d), jnp.float32),
        scratch_shapes=[pltpu.VMEM((tm, d), jnp.bfloat16), pltpu.VMEM((tm, d), jnp.float32)],
        compiler_params=_cparams("parallel", "arbitrary"),
        name="swiglu_ffn",
    )(x, g.reshape(1, d), w_gate, w_up, w_down, final_gain.reshape(1, d))


MOE_TILE_ROWS = 1024
ROUTE_ROWS = 512


def _lane_pick(lane, values):
    out = jnp.zeros(lane.shape, values[0].dtype)
    for k, v in enumerate(values):
        out = jnp.where(lane == k, v, out)
    return out


def _route_kernel(x_ref, g_ref, w_ref, wts_ref, idx_ref, cnt_ref, blank_ref, carry_ref, *, tm):
    @pl.when(pl.program_id(0) == 0)
    def _():
        carry_ref[...] = jnp.zeros(carry_ref.shape, jnp.float32)

    blank_ref[...] = jnp.zeros(blank_ref.shape, jnp.float32)

    logits = jnp.dot(_rms(x_ref[...], g_ref[...]), w_ref[...],
                     preferred_element_type=jnp.float32, precision=lax.Precision.HIGHEST)
    lane = lax.broadcasted_iota(jnp.int32, logits.shape, 1)
    logits = jnp.where(lane < N_EXPERTS, logits, NEG_BIG)
    m1 = jnp.max(logits, axis=1, keepdims=True)
    i1 = jnp.min(jnp.where(logits == m1, lane, LANES), axis=1, keepdims=True)
    rest = jnp.where(lane == i1, NEG_BIG, logits)
    m2 = jnp.max(rest, axis=1, keepdims=True)
    i2 = jnp.min(jnp.where(rest == m2, lane, LANES), axis=1, keepdims=True)
    e2 = jnp.exp(m2 - m1)
    w1 = 1.0 / (1.0 + e2)
    hot1, hot2 = lane == i1, lane == i2
    both = jnp.logical_or(hot1, hot2).astype(jnp.float32)
    r = lax.broadcasted_iota(jnp.int32, (tm, tm), 0)
    c = lax.broadcasted_iota(jnp.int32, (tm, tm), 1)
    earlier = (c < r).astype(jnp.bfloat16)
    prior = carry_ref[...] + jnp.dot(earlier, both.astype(jnp.bfloat16), preferred_element_type=jnp.float32)
    rank1 = jnp.sum(jnp.where(hot1, prior, 0.0), axis=1, keepdims=True)
    rank2 = jnp.sum(jnp.where(hot2, prior, 0.0), axis=1, keepdims=True)
    carry_ref[...] += jnp.sum(both, axis=0, keepdims=True)
    cnt_ref[...] = jnp.broadcast_to(carry_ref[...], cnt_ref.shape)
    wts_ref[...] = _lane_pick(lane, [w1, e2 * w1])
    idx_ref[...] = _lane_pick(lane, [i1, i2, rank1.astype(jnp.int32), rank2.astype(jnp.int32)])


def moe_route(x, g, w_router, n_rows):
    n, d = x.shape
    tm = _tile(n, ROUTE_ROWS)
    steps = n // tm
    assert n_rows % steps == 0
    blank_rows = (n_rows // steps,) + _row_tiles(d)
    w = jnp.pad(w_router, ((0, 0), (0, LANES - N_EXPERTS)))
    row = pl.BlockSpec((tm, LANES), lambda i: (i, 0))
    return pl.pallas_call(
        functools.partial(_route_kernel, tm=tm),
        grid=(steps,),
        in_specs=[pl.BlockSpec((tm, d), lambda i: (i, 0)),
                  pl.BlockSpec((1, d), lambda i: (0, 0)),
                  pl.BlockSpec((d, LANES), lambda i: (0, 0))],
        out_specs=[row, row, pl.BlockSpec((8, LANES), lambda i: (0, 0)),
                   pl.BlockSpec(blank_rows, lambda i: (i, 0, 0))],
        out_shape=[jax.ShapeDtypeStruct((n, LANES), jnp.float32),
                   jax.ShapeDtypeStruct((n, LANES), jnp.int32),
                   jax.ShapeDtypeStruct((8, LANES), jnp.float32),
                   jax.ShapeDtypeStruct((n_rows,) + _row_tiles(d), jnp.float32)],
        scratch_shapes=[pltpu.VMEM((1, LANES), jnp.float32)],
        compiler_params=_cparams("arbitrary"),
        name="moe_route",
    )(x, g.reshape(1, d), w)


def _dest_kernel(idx_ref, cnt_ref, dest_ref, te_ref, *, tile_rows):
    tiles = jnp.floor((cnt_ref[...] + (tile_rows - 1)) / tile_rows)
    er = lax.broadcasted_iota(jnp.int32, (LANES, LANES), 0)
    ec = lax.broadcasted_iota(jnp.int32, (LANES, LANES), 1)
    first_tile = jnp.dot(tiles.astype(jnp.bfloat16), (er < ec).astype(jnp.bfloat16),
                         preferred_element_type=jnp.float32)
    first_row = first_tile[0:1, :] * tile_rows
    idx = idx_ref[...]
    lane = lax.broadcasted_iota(jnp.int32, idx.shape, 1)
    idx_f = idx.astype(jnp.float32)
    dests = []
    for slot in range(2):
        expert = jnp.sum(jnp.where(lane == slot, idx, 0), axis=1, keepdims=True)
        start = jnp.sum(jnp.where(lane == expert, first_row, 0.0), axis=1, keepdims=True)
        dests.append((start + _lane_col(idx_f, 2 + slot)).astype(jnp.int32))
    dest_ref[...] = _lane_pick(lane, dests)
    last_tile = first_tile + tiles
    tau = lax.broadcasted_iota(jnp.int32, te_ref.shape, 1).astype(jnp.float32)
    owner = jnp.zeros(te_ref.shape, jnp.float32)
    for e in range(N_EXPERTS):
        owner = owner + (_lane_col(last_tile, e) <= tau).astype(jnp.float32)
    te_ref[...] = owner.astype(jnp.int32)


def moe_dest(idx, cnt, n_tiles):
    n = idx.shape[0]
    tm = _tile(n, ROUTE_ROWS)
    te_lanes = -(-n_tiles // LANES) * LANES
    return pl.pallas_call(
        functools.partial(_dest_kernel, tile_rows=MOE_TILE_ROWS),
        grid=(n // tm,),
        in_specs=[pl.BlockSpec((tm, LANES), lambda i: (i, 0)),
                  pl.BlockSpec((8, LANES), lambda i: (0, 0))],
        out_specs=[pl.BlockSpec((tm, LANES), lambda i: (i, 0)),
                   pl.BlockSpec((8, te_lanes), lambda i: (0, 0))],
        out_shape=[jax.ShapeDtypeStruct((n, LANES), jnp.int32),
                   jax.ShapeDtypeStruct((8, te_lanes), jnp.int32)],
        compiler_params=_cparams("arbitrary"),
        name="moe_dest",
    )(idx, cnt)


SUBLANES = 8


def _row_tiles(d):
    return (d // (SUBLANES * LANES) * SUBLANES, LANES)


def _dispatch_kernel(dest_ref, x_ref, g_ref, zeros_ref, xs_ref, hp_ref, sem, *, tm):
    del zeros_ref
    hp_ref[...] = _rms(x_ref[...], g_ref[...]).reshape(hp_ref.shape)

    def issue(r, carry):
        for slot in range(2):
            pltpu.make_async_copy(hp_ref.at[r], xs_ref.at[dest_ref[0, 2 * r + slot]], sem).start(priority=slot)
        return carry

    def drain(r, carry):
        for slot in range(2):
            pltpu.make_async_copy(hp_ref.at[0], xs_ref.at[0], sem).wait()
        return carry

    lax.fori_loop(0, tm, issue, 0, unroll=8)
    lax.fori_loop(0, tm, drain, 0, unroll=8)


def moe_dispatch(x, g, dest_pairs, zeros):
    n, d = x.shape
    tm = dest_pairs.shape[2] // 2
    n_rows, row = zeros.shape[0], zeros.shape[1:]
    return pl.pallas_call(
        functools.partial(_dispatch_kernel, tm=tm),
        grid=(n // tm,),
        in_specs=[pl.BlockSpec((None, 1, 2 * tm), lambda i: (i, 0, 0), memory_space=pltpu.SMEM),
                  pl.BlockSpec((tm, d), lambda i: (i, 0)),
                  pl.BlockSpec((1, d), lambda i: (0, 0)),
                  pl.BlockSpec(memory_space=pl.ANY)],
        out_specs=pl.BlockSpec(memory_space=pl.ANY),
        out_shape=jax.ShapeDtypeStruct((n_rows,) + row, jnp.float32),
        scratch_shapes=[pltpu.VMEM((tm,) + row, jnp.float32), pltpu.SemaphoreType.DMA(())],
        input_output_aliases={3: 0},
        compiler_params=_cparams("arbitrary"),
        name="moe_dispatch",
    )(dest_pairs, x, g.reshape(1, d), zeros)


def _grouped_ffn_kernel(te_ref, xs_ref, wg_ref, wu_ref, wd_ref, ys_ref, h_ref, acc_ref):
    i, f = pl.program_id(0), pl.program_id(1)
    live = te_ref[i] < N_EXPERTS
    first = f == 0

    @pl.when(jnp.logical_and(first, live))
    def _():
        h = xs_ref[...].reshape(h_ref.shape).astype(h_ref.dtype)
        h_ref[...] = h
        acc_ref[...] = _swiglu_step(h, wg_ref, wu_ref, wd_ref)

    @pl.when(jnp.logical_and(first, jnp.logical_not(live)))
    def _():
        acc_ref[...] = jnp.zeros(acc_ref.shape, jnp.float32)

    @pl.when(jnp.logical_and(jnp.logical_not(first), live))
    def _():
        acc_ref[...] += _swiglu_step(h_ref[...], wg_ref, wu_ref, wd_ref)

    @pl.when(f == pl.num_programs(1) - 1)
    def _():
        ys_ref[...] = acc_ref[...].reshape(ys_ref.shape)


def moe_grouped_ffn(xs, tile_expert, w_gate, w_up, w_down, *, tf=512):
    n_rows = xs.shape[0]
    row = xs.shape[1:]
    d = w_gate.shape[1]
    ff = w_gate.shape[2]
    tm = MOE_TILE_ROWS
    expert = lambda i, te: jnp.minimum(te[i], N_EXPERTS - 1)
    rows_spec = pl.BlockSpec((tm,) + row, lambda i, f, te: (i, 0, 0))
    return pl.pallas_call(
        _grouped_ffn_kernel,
        grid_spec=pltpu.PrefetchScalarGridSpec(
            num_scalar_prefetch=1,
            grid=(n_rows // tm, ff // tf),
            in_specs=[rows_spec,
                      pl.BlockSpec((None, d, tf), lambda i, f, te: (expert(i, te), 0, f)),
                      pl.BlockSpec((None, d, tf), lambda i, f, te: (expert(i, te), 0, f)),
                      pl.BlockSpec((None, tf, d), lambda i, f, te: (expert(i, te), f, 0))],
            out_specs=rows_spec,
            scratch_shapes=[pltpu.VMEM((tm, d), jnp.bfloat16), pltpu.VMEM((tm, d), jnp.float32)]),
        out_shape=jax.ShapeDtypeStruct(xs.shape, jnp.float32),
        compiler_params=_cparams("parallel", "arbitrary"),
        name="moe_grouped_ffn",
    )(tile_expert, xs, w_gate, w_up, w_down)


def _combine_kernel(dest_ref, x_ref, wts_ref, gf_ref, ys_ref, o_ref, y1_ref, y2_ref, sem, *, tm, final_norm):
    bufs = (y1_ref, y2_ref)

    def issue(r, carry):
        for slot in range(2):
            pltpu.make_async_copy(ys_ref.at[dest_ref[0, 2 * r + slot]], bufs[slot].at[r], sem).start(priority=slot)
        return carry

    def drain(r, carry):
        for slot in range(2):
            pltpu.make_async_copy(ys_ref.at[0], bufs[slot].at[0], sem).wait()
        return carry

    lax.fori_loop(0, tm, issue, 0, unroll=8)
    lax.fori_loop(0, tm, drain, 0, unroll=8)
    wts = wts_ref[...]
    out = x_ref[...]
    for slot in range(2):
        out = out + _lane_col(wts, slot) * bufs[slot][...].reshape(out.shape)
    if final_norm:
        out = _rms(out, gf_ref[...])
    o_ref[...] = out


def moe_combine(x, wts, dest_pairs, ys, final_gain, *, final_norm):
    n, d = x.shape
    tm = dest_pairs.shape[2] // 2
    return pl.pallas_call(
        functools.partial(_combine_kernel, tm=tm, final_norm=final_norm),
        grid=(n // tm,),
        in_specs=[pl.BlockSpec((None, 1, 2 * tm), lambda i: (i, 0, 0), memory_space=pltpu.SMEM),
                  pl.BlockSpec((tm, d), lambda i: (i, 0)),
                  pl.BlockSpec((tm, LANES), lambda i: (i, 0)),
                  pl.BlockSpec((1, d), lambda i: (0, 0)),
                  pl.BlockSpec(memory_space=pl.ANY)],
        out_specs=pl.BlockSpec((tm, d), lambda i: (i, 0)),
        out_shape=jax.ShapeDtypeStruct((n, d), jnp.float32),
        scratch_shapes=[pltpu.VMEM((tm,) + ys.shape[1:], jnp.float32),
                        pltpu.VMEM((tm,) + ys.shape[1:], jnp.float32),
                        pltpu.SemaphoreType.DMA(())],
        compiler_params=_cparams("arbitrary"),
        name="moe_combine",
    )(dest_pairs, x, wts, final_gain.reshape(1, d), ys)


def moe_swiglu(x, g, w_router, w_gate, w_up, w_down, final_gain, *, final_norm):
    n = x.shape[0]
    n_tiles = -(-2 * n // MOE_TILE_ROWS) + N_EXPERTS
    wts, idx, cnt, blank = moe_route(x, g, w_router, n_tiles * MOE_TILE_ROWS)
    dest, tile_expert = moe_dest(idx, cnt, n_tiles)
    tm = _tile(n, ROUTE_ROWS)
    dest_pairs = dest[:, :2].reshape(n // tm, 1, 2 * tm)
    xs = moe_dispatch(x, g, dest_pairs, blank)
    ys = moe_grouped_ffn(xs, tile_expert[0, :n_tiles], w_gate, w_up, w_down)
    return moe_combine(x, wts, dest_pairs, ys, final_gain, final_norm=final_norm)


def _rope_tables(seq):
    half = DA_DIM // 2
    inv_freq = 1.0 / (ROPE_THETA ** (jnp.arange(0, DA_DIM, 2, dtype=jnp.float32) / DA_DIM))
    ang = jnp.arange(seq, dtype=jnp.float32)[:, None] * inv_freq[None, :]
    cos, sin = jnp.cos(ang), jnp.sin(ang)
    reps = LANES // half
    sign = jnp.tile(jnp.concatenate([-jnp.ones((half,)), jnp.ones((half,))]), reps // 2)
    return jnp.tile(cos, (1, reps)), jnp.tile(sin, (1, reps)) * sign[None, :]


def kernel(x, mem, norm_mix, w_in, gd_conv, lam_q1, lam_k1, lam_q2, lam_k2, da_subln, gd_a_log,
           gd_dt_bias, gd_norm, w_br_a, w_br_b, w_br_c, w_out, norm_x, norm_mem, w_q_x, w_kv_x,
           w_o_x, norm_ffn, w_gate_dense, w_up_dense, w_down_dense, w_router, w_gate_exp,
           w_up_exp, w_down_exp, norm_final):
    batch, seq, d = x.shape
    n_mem = mem.shape[1]
    depth = norm_mix.shape[0]
    bf = jnp.bfloat16
    n = batch * seq
    cos, sin = _rope_tables(seq)
    xf = x.reshape(n, d)
    memf = mem.reshape(batch * n_mem, d)
    ab0 = MIX_W
    ab1 = MIX_W + 2 * GD_HEADS

    for l in range(depth):
        lam_init = 0.8 - 0.6 * math.exp(-0.3 * l)
        w_l = w_in[l]
        w_main = jnp.concatenate([w_l[:, :ab0], w_l[:, ab1:]], axis=1).astype(bf)
        w_ab = jnp.pad(w_l[:, ab0:ab1], ((0, 0), (0, LANES - 2 * GD_HEADS))).astype(bf)
        proj, gate = mixer_in_proj(xf, norm_mix[l], w_main, w_ab, cos, sin, gd_a_log[l], gd_dt_bias[l], seq)

        lam_params = jnp.stack([lam_q1[l], lam_k1[l], lam_q2[l], lam_k2[l]])
        o_a = diff_attention(proj, lam_params, da_subln[l], batch, seq, lam_init)
        o_b = stick_breaking_attention(proj, batch, seq)
        qkv = gd_conv_silu(proj, gd_conv[l], batch, seq)
        u, w, qg, kd, amat = gd_local(qkv, gate, batch, seq)
        o_c = gd_scan(u, w, qg, kd, amat, gate, proj, gd_norm[l], batch, seq)
        xf = merge_branches(xf, o_a, o_b, o_c, proj, w_br_a[l].astype(bf), w_br_b[l].astype(bf),
                            w_br_c[l].astype(bf), w_out[l].astype(bf))

        kv = rms_matmul(memf, norm_mem[l], w_kv_x[l].astype(bf), bf)
        xf = cross_attention(xf, norm_x[l], w_q_x[l].astype(bf), kv, w_o_x[l].astype(bf),
                             batch, seq, n_mem)

        last = l == depth - 1
        i = l // 2
        if l % 2 == 0:
            xf = swiglu_ffn(xf, norm_ffn[l], w_gate_dense[i].astype(bf), w_up_dense[i].astype(bf),
                            w_down_dense[i].astype(bf), norm_final, final_norm=last)
        else:
            xf = moe_swiglu(xf, norm_ffn[l], w_router[i], w_gate_exp[i].astype(bf), w_up_exp[i].astype(bf),
                            w_down_exp[i].astype(bf), norm_final, final_norm=last)
    return xf.reshape(batch, seq, d)
```
